```python
import math
import jax
import jax.numpy as jnp
from jax import lax
import numpy as np

D_MODEL = 1024
BATCH = 2
SEQ = 8192
DEPTH = 4
DEC_BATCH = 128
DEC_SEQ = 1
PAST_LEN = 2048
PAGE_SIZE = 128

ATTN_HEAD_DIM = 64
ATTN_PATTERNS = ((128, 1), (512, 4), (2048, 16))
ATTN_HEADS_PER_GROUP = 4
ATTN_HEADS = ATTN_HEADS_PER_GROUP * len(ATTN_PATTERNS)
ATTN_WIDTH = ATTN_HEADS * ATTN_HEAD_DIM
ATTN_OUT = ATTN_HEADS_PER_GROUP * ATTN_HEAD_DIM
ATTN_QBLOCK = 128
N_BUCKETS = 32
BUCKET_MAX_DIST = 2048
NEG_INF = -1e30
SSD_HEADS = 12
SSD_HEAD_DIM = 64
SSD_D_INNER = SSD_HEADS * SSD_HEAD_DIM
SSD_GROUPS = 4
SSD_STATE = 64
SSD_CONV = 4
SSD_CONV_CH = SSD_D_INNER + 2 * SSD_GROUPS * SSD_STATE
SSD_CHUNK = 128
S5_GROUP_CH = 16
S5_GROUPS = 48
S5_WIDTH = S5_GROUPS * S5_GROUP_CH
S5_STATE = 64
D_FF = 2816
PE_DIM = 256
N_BRANCHES = 3
RMS_EPS = 1e-6
IN_COLS = 3 * ATTN_WIDTH + SSD_D_INNER + SSD_CONV_CH + SSD_HEADS + S5_WIDTH + N_BRANCHES * D_MODEL

kernel_name = 'dilated_ssd_s5_gated_parallel_decoder_step'


def in_proj_splits():
    widths = [ATTN_WIDTH] * 3 + [SSD_D_INNER, SSD_CONV_CH, SSD_HEADS, S5_WIDTH] + [D_MODEL] * N_BRANCHES
    return [int(s) for s in np.cumsum(widths)[:-1]]


def rmsnorm(x, g):
    xf = x.astype(jnp.float32)
    inv = lax.rsqrt(jnp.mean(xf * xf, axis=-1, keepdims=True) + RMS_EPS)
    return (xf * inv).astype(x.dtype) * g


def swiglu(x, w_gate, w_up, w_down):
    return (jax.nn.silu(x @ w_gate) * (x @ w_up)) @ w_down


def t5_bucket(dist):
    max_exact = N_BUCKETS // 2
    d = np.asarray(dist).astype(np.int32)
    df = np.maximum(d, 1).astype(np.float32)
    large = max_exact + (np.log(df / max_exact) / math.log(BUCKET_MAX_DIST / max_exact)
                         * (N_BUCKETS - max_exact)).astype(np.int32)
    return np.where(d < max_exact, d, np.minimum(large, N_BUCKETS - 1)).astype(np.int32)


def dilated_attention(q, kv_all, rel_bias):
    n, L = q.shape[0], q.shape[1]
    hpg = ATTN_HEADS_PER_GROUP
    qb = math.gcd(L, ATTN_QBLOCK)
    nb = L // qb
    q_blocks = (q * (ATTN_HEAD_DIM ** -0.5)).reshape(n, nb, qb, ATTN_HEADS, ATTN_HEAD_DIM).transpose(1, 0, 2, 3, 4)
    statics = []
    for g, (w, d) in enumerate(ATTN_PATTERNS):
        offs = np.arange(w // d + 1) * d
        bias = rel_bias[t5_bucket(offs)][:, g * hpg:(g + 1) * hpg].T.astype(jnp.float32)
        statics.append((offs, bias))

    def block(args):
        qblk, b = args
        i = b * qb + jnp.arange(qb)
        lses, outs = [], []
        for g in range(len(ATTN_PATTERNS)):
            k_all, v_all = kv_all[g]
            past = k_all.shape[1] - L
            offs, bias = statics[g]
            idx = past + i[:, None] - offs[None, :]
            valid = idx >= 0
            idx = jnp.maximum(idx, 0)
            kg = jnp.take(k_all, idx, axis=1)
            vg = jnp.take(v_all, idx, axis=1)
            qg = qblk[:, :, g * hpg:(g + 1) * hpg]
            s = jnp.einsum('nqhd,nqkhd->nqhk', qg, kg).astype(jnp.float32) + bias[None, None]
            s = jnp.where(valid[None, :, None, :], s, NEG_INF)
            m = jnp.max(s, axis=-1, keepdims=True)
            e = jnp.exp(s - m)
            den = jnp.sum(e, axis=-1, keepdims=True)
            outs.append(jnp.einsum('nqhk,nqkhd->nqhd', e / den, vg.astype(jnp.float32)))
            lses.append((m + jnp.log(den))[..., 0])
        wts = jax.nn.softmax(jnp.stack(lses, axis=0), axis=0)
        return jnp.sum(wts[..., None] * jnp.stack(outs, axis=0), axis=0)

    out = lax.map(block, (q_blocks, jnp.arange(nb)))
    return out.transpose(1, 0, 2, 3, 4).reshape(n, L, ATTN_OUT).astype(q.dtype)


def ssd_scan(x, dt, a, bm, cm, h0):
    n, L, H, P = x.shape
    S = bm.shape[-1]
    Q = math.gcd(L, SSD_CHUNK)
    nc = L // Q
    xf = x.astype(jnp.float32).reshape(n, nc, Q, H, P)
    bf = bm.astype(jnp.float32).reshape(n, nc, Q, H, S)
    cf = cm.astype(jnp.float32).reshape(n, nc, Q, H, S)
    dtc = dt.reshape(n, nc, Q, H)
    cum = jnp.cumsum(dtc * a, axis=2)
    causal = np.tril(np.ones((Q, Q), dtype=bool))[None, None, :, :, None]
    diff = cum[:, :, :, None, :] - cum[:, :, None, :, :]
    decay = jnp.where(causal, jnp.exp(jnp.where(causal, diff, 0.0)), 0.0)
    scores = jnp.einsum('ncihs,ncjhs->ncijh', cf, bf) * decay * dtc[:, :, None, :, :]
    y_diag = jnp.einsum('ncijh,ncjhp->ncihp', scores, xf)
    w_end = jnp.exp(cum[:, :, -1:, :] - cum) * dtc
    chunk_states = jnp.einsum('ncjh,ncjhs,ncjhp->nchps', w_end, bf, xf)
    chunk_decay = jnp.exp(cum[:, :, -1, :])

    def step(h, inp):
        st, dec = inp
        return dec[:, :, None, None] * h + st, h

    h_last, h_in = lax.scan(step, h0.astype(jnp.float32),
                            (chunk_states.transpose(1, 0, 2, 3, 4), chunk_decay.transpose(1, 0, 2)))
    h_in = h_in.transpose(1, 0, 2, 3, 4)
    y_off = jnp.einsum('ncihs,nchps->ncihp', cf, h_in) * jnp.exp(cum)[..., None]
    return (y_diag + y_off).reshape(n, L, H, P), h_last


def ssd_mixer(z, xbc, dt_raw, conv_state, ssm_state, conv_w, conv_b, dt_bias, a_log, d_skip, norm_g):
    n, L, _ = xbc.shape
    xpad = jnp.concatenate([conv_state.astype(xbc.dtype), xbc], axis=1)
    new_conv = xpad[:, L:]
    conv = lax.conv_general_dilated(xpad, conv_w.astype(xpad.dtype)[:, None, :], window_strides=(1,),
                                    padding='VALID', dimension_numbers=('NWC', 'WIO', 'NWC'),
                                    feature_group_count=SSD_CONV_CH)
    xbc_c = jax.nn.silu(conv + conv_b)
    gs = SSD_GROUPS * SSD_STATE
    xs = xbc_c[..., :SSD_D_INNER].reshape(n, L, SSD_HEADS, SSD_HEAD_DIM)
    rep = SSD_HEADS // SSD_GROUPS
    bm = jnp.repeat(xbc_c[..., SSD_D_INNER:SSD_D_INNER + gs].reshape(n, L, SSD_GROUPS, SSD_STATE), rep, axis=2)
    cm = jnp.repeat(xbc_c[..., SSD_D_INNER + gs:].reshape(n, L, SSD_GROUPS, SSD_STATE), rep, axis=2)
    dt = jax.nn.softplus((dt_raw + dt_bias).astype(jnp.float32))
    a = -jnp.exp(a_log.astype(jnp.float32))
    y, h_last = ssd_scan(xs, dt, a, bm, cm, ssm_state)
    y = (y + d_skip.astype(jnp.float32)[:, None] * xs.astype(jnp.float32)).reshape(n, L, SSD_D_INNER)
    y = rmsnorm(y * jax.nn.silu(z.astype(jnp.float32)), norm_g)
    return y.astype(xbc.dtype), new_conv, h_last


def s5_mixer(u, h0, a_re, a_im, b_re, b_im, c_re, c_im, d_skip, log_dt, w_glu):
    n, L, _ = u.shape
    f32 = jnp.float32
    uf = u.astype(f32).reshape(n, L, S5_GROUPS, S5_GROUP_CH)
    delta = jnp.exp(log_dt.astype(f32))[:, None]
    lr, li = a_re.astype(f32), a_im.astype(f32)
    mag = jnp.exp(lr * delta)
    ab_re, ab_im = mag * jnp.cos(li * delta), mag * jnp.sin(li * delta)
    nr, ni = ab_re - 1.0, ab_im
    den = lr * lr + li * li
    f_re, f_im = (nr * lr + ni * li) / den, (ni * lr - nr * li) / den
    br, bi = b_re.astype(f32), b_im.astype(f32)
    bb_re = f_re[..., None] * br - f_im[..., None] * bi
    bb_im = f_re[..., None] * bi + f_im[..., None] * br
    bu_re = jnp.einsum('nlgc,gsc->nlgs', uf, bb_re)
    bu_im = jnp.einsum('nlgc,gsc->nlgs', uf, bb_im)
    h0_re, h0_im = h0[:, 0].astype(f32), h0[:, 1].astype(f32)
    bu_re = bu_re.at[:, 0].add(ab_re * h0_re - ab_im * h0_im)
    bu_im = bu_im.at[:, 0].add(ab_re * h0_im + ab_im * h0_re)
    a_seq_re = jnp.broadcast_to(ab_re[None, None], (1, L, S5_GROUPS, S5_STATE))
    a_seq_im = jnp.broadcast_to(ab_im[None, None], (1, L, S5_GROUPS, S5_STATE))

    def combine(e1, e2):
        a1r, a1i, b1r, b1i = e1
        a2r, a2i, b2r, b2i = e2
        return (a2r * a1r - a2i * a1i, a2r * a1i + a2i * a1r,
                a2r * b1r - a2i * b1i + b2r, a2r * b1i + a2i * b1r + b2i)

    _, _, h_re, h_im = lax.associative_scan(combine, (a_seq_re, a_seq_im, bu_re, bu_im), axis=1)
    y = jnp.einsum('nlgs,gcs->nlgc', h_re, c_re.astype(f32)) - jnp.einsum('nlgs,gcs->nlgc', h_im, c_im.astype(f32))
    y = y.reshape(n, L, S5_WIDTH) + d_skip.astype(f32) * u.astype(f32)
    gl = jax.nn.gelu(y).astype(u.dtype) @ w_glu
    out = gl[..., :D_MODEL] * jax.nn.sigmoid(gl[..., D_MODEL:])
    new_state = jnp.stack([h_re[:, -1], h_im[:, -1]], axis=1)
    return out.astype(u.dtype), new_state


def trunk_layer(x, p, kv_past, conv_state, ssm_state, s5_state, rel_bias, lw):
    n, L, _ = x.shape
    hpg = ATTN_HEADS_PER_GROUP
    x = x + 0.5 * swiglu(rmsnorm(x, lw['ffn1_norm']), lw['ffn1_w_gate'], lw['ffn1_w_up'], lw['ffn1_w_down'])
    h = rmsnorm(x, lw['mix_norm'])
    q, k, v, z, xbc, dt_raw, u, g_attn, g_ssd, g_s5 = jnp.split(h @ lw['w_in'], in_proj_splits(), axis=-1)
    q = q.reshape(n, L, ATTN_HEADS, ATTN_HEAD_DIM)
    k = k.reshape(n, L, ATTN_HEADS, ATTN_HEAD_DIM)
    v = v.reshape(n, L, ATTN_HEADS, ATTN_HEAD_DIM)
    kv_all, kv_new = [], []
    for g in range(len(ATTN_PATTERNS)):
        kg, vg = k[:, :, g * hpg:(g + 1) * hpg], v[:, :, g * hpg:(g + 1) * hpg]
        kv_new.append(jnp.stack([kg, vg], axis=1))
        past = kv_past[g].astype(x.dtype)
        kv_all.append((jnp.concatenate([past[:, 0], kg], axis=1), jnp.concatenate([past[:, 1], vg], axis=1)))
    attn = dilated_attention(q, kv_all, rel_bias)
    y_ssd, new_conv, new_ssm = ssd_mixer(z, xbc, dt_raw, conv_state, ssm_state, lw['ssd_conv_w'], lw['ssd_conv_b'],
                                         lw['ssd_dt_bias'], lw['ssd_a_log'], lw['ssd_d'], lw['ssd_norm'])
    b_s5, new_s5 = s5_mixer(u, s5_state, lw['s5_a_re'], lw['s5_a_im'], lw['s5_b_re'], lw['s5_b_im'],
                            lw['s5_c_re'], lw['s5_c_im'], lw['s5_d'], lw['s5_log_dt'], lw['w_s5_glu'])
    merged = (jax.nn.sigmoid(g_attn) * (attn @ lw['w_branch_attn'])
              + jax.nn.sigmoid(g_ssd) * (y_ssd @ lw['w_branch_ssd'])
              + jax.nn.sigmoid(g_s5) * b_s5)
    x = x + merged @ lw['w_out']
    x = x + 0.5 * swiglu(rmsnorm(x, lw['ffn2_norm']), lw['ffn2_w_gate'], lw['ffn2_w_up'], lw['ffn2_w_down'])
    x = x + jax.nn.sigmoid(rmsnorm(x, lw['pe_norm']) @ lw['w_pe_gate']) * (p @ lw['w_pe_proj'])
    return x, kv_new, new_conv, new_ssm, new_s5


def setup_inputs(seed: int = 0) -> dict:
    key = jax.random.key(seed)
    keys = iter(jax.random.split(key, 64))

    def nrm(shape, scale=1.0):
        return jax.random.normal(next(keys), shape, jnp.float32) * scale

    def gain(shape):
        return 1.0 + nrm(shape, 0.02)

    def log_uniform(shape, lo, hi):
        return jnp.exp(jax.random.uniform(next(keys), shape, jnp.float32, math.log(lo), math.log(hi)))

    Lr, D, F = DEPTH, D_MODEL, D_FF
    hg, hd = ATTN_HEADS_PER_GROUP, ATTN_HEAD_DIM
    bufs = [min(w, PAST_LEN) for w, _ in ATTN_PATTERNS]
    dt0 = log_uniform((Lr, SSD_HEADS), 1e-3, 1e-1)
    s5_n = jnp.arange(S5_STATE, dtype=jnp.float32)
    return {
        'x_prompt': nrm((BATCH, SEQ, D)),
        'x_sample': nrm((DEC_BATCH, DEC_SEQ, D)),
        'cache_kv_w128': nrm((Lr, DEC_BATCH, 2, bufs[0], hg, hd)),
        'cache_kv_w512': nrm((Lr, DEC_BATCH, 2, bufs[1], hg, hd)),
        'cache_kv_w2048': nrm((Lr, DEC_BATCH, 2, bufs[2], hg, hd)),
        'state_ssd': nrm((Lr, DEC_BATCH, SSD_HEADS, SSD_HEAD_DIM, SSD_STATE), 0.5),
        'state_conv': nrm((Lr, DEC_BATCH, SSD_CONV - 1, SSD_CONV_CH)),
        'state_s5': nrm((Lr, DEC_BATCH, 2, S5_GROUPS, S5_STATE), 0.5),
        'p_prompt': nrm((Lr, BATCH, SEQ, PE_DIM)),
        'p_sample': nrm((Lr, DEC_BATCH, DEC_SEQ, PE_DIM)),
        'attn_rel_bias': nrm((N_BUCKETS, ATTN_HEADS), 0.1),
        'ffn1_norm': gain((Lr, D)),
        'ffn1_w_gate': nrm((Lr, D, F), D ** -0.5),
        'ffn1_w_up': nrm((Lr, D, F), D ** -0.5),
        'ffn1_w_down': nrm((Lr, F, D), F ** -0.5),
        'mix_norm': gain((Lr, D)),
        'w_in': nrm((Lr, D, IN_COLS), D ** -0.5),
        'ssd_conv_w': nrm((Lr, SSD_CONV, SSD_CONV_CH), SSD_CONV ** -0.5),
        'ssd_conv_b': nrm((Lr, SSD_CONV_CH), 0.02),
        'ssd_dt_bias': dt0 + jnp.log(-jnp.expm1(-dt0)),
        'ssd_a_log': jnp.log(jax.random.uniform(next(keys), (Lr, SSD_HEADS), jnp.float32, 1.0, 16.0)),
        'ssd_d': 1.0 + nrm((Lr, SSD_HEADS), 0.1),
        'ssd_norm': gain((Lr, SSD_D_INNER)),
        's5_a_re': -0.5 + nrm((Lr, S5_GROUPS, S5_STATE), 0.01),
        's5_a_im': math.pi * s5_n + nrm((Lr, S5_GROUPS, S5_STATE), 0.01),
        's5_b_re': nrm((Lr, S5_GROUPS, S5_STATE, S5_GROUP_CH), (2 * S5_GROUP_CH) ** -0.5),
        's5_b_im': nrm((Lr, S5_GROUPS, S5_STATE, S5_GROUP_CH), (2 * S5_GROUP_CH) ** -0.5),
        's5_c_re': nrm((Lr, S5_GROUPS, S5_GROUP_CH, S5_STATE), (2 * S5_STATE) ** -0.5),
        's5_c_im': nrm((Lr, S5_GROUPS, S5_GROUP_CH, S5_STATE), (2 * S5_STATE) ** -0.5),
        's5_d': nrm((Lr, S5_WIDTH), 0.5),
        's5_log_dt': jnp.log(log_uniform((Lr, S5_GROUPS), 1e-3, 1e-1)),
        'w_s5_glu': nrm((Lr, S5_WIDTH, 2 * D), S5_WIDTH ** -0.5),
        'w_branch_attn': nrm((Lr, ATTN_OUT, D), ATTN_OUT ** -0.5),
        'w_branch_ssd': nrm((Lr, SSD_D_INNER, D), SSD_D_INNER ** -0.5),
        'w_out': nrm((Lr, D, D), D ** -0.5),
        'ffn2_norm': gain((Lr, D)),
        'ffn2_w_gate': nrm((Lr, D, F), D ** -0.5),
        'ffn2_w_up': nrm((Lr, D, F), D ** -0.5),
        'ffn2_w_down': nrm((Lr, F, D), F ** -0.5),
        'pe_norm': gain((Lr, D)),
        'w_pe_gate': nrm((Lr, D, D), D ** -0.5),
        'w_pe_proj': nrm((Lr, PE_DIM, D), PE_DIM ** -0.5),
        'final_norm': gain((D,)),
    }


def reference(x_prompt, x_sample, cache_kv_w128, cache_kv_w512, cache_kv_w2048, state_ssd, state_conv, state_s5,
              p_prompt, p_sample, attn_rel_bias, ffn1_norm, ffn1_w_gate, ffn1_w_up, ffn1_w_down, mix_norm, w_in,
              ssd_conv_w, ssd_conv_b, ssd_dt_bias, ssd_a_log, ssd_d, ssd_norm, s5_a_re, s5_a_im, s5_b_re, s5_b_im,
              s5_c_re, s5_c_im, s5_d, s5_log_dt, w_s5_glu, w_branch_attn, w_branch_ssd, w_out, ffn2_norm,
              ffn2_w_gate, ffn2_w_up, ffn2_w_down, pe_norm, w_pe_gate, w_pe_proj, final_norm):
    xp, xs = x_prompt, x_sample
    n_p, len_p = x_prompt.shape[0], x_prompt.shape[1]
    caches = (cache_kv_w128, cache_kv_w512, cache_kv_w2048)
    kv_p = [[], [], []]
    kv_s = [[], [], []]
    ssd_p, ssd_s, conv_p, conv_s, s5_p, s5_s = [], [], [], [], [], []
    for i in range(DEPTH):
        lw = {
            'ffn1_norm': ffn1_norm[i], 'ffn1_w_gate': ffn1_w_gate[i], 'ffn1_w_up': ffn1_w_up[i],
            'ffn1_w_down': ffn1_w_down[i], 'mix_norm': mix_norm[i], 'w_in': w_in[i],
            'ssd_conv_w': ssd_conv_w[i], 'ssd_conv_b': ssd_conv_b[i], 'ssd_dt_bias': ssd_dt_bias[i],
            'ssd_a_log': ssd_a_log[i], 'ssd_d': ssd_d[i], 'ssd_norm': ssd_norm[i],
            's5_a_re': s5_a_re[i], 's5_a_im': s5_a_im[i], 's5_b_re': s5_b_re[i], 's5_b_im': s5_b_im[i],
            's5_c_re': s5_c_re[i], 's5_c_im': s5_c_im[i], 's5_d': s5_d[i], 's5_log_dt': s5_log_dt[i],
            'w_s5_glu': w_s5_glu[i], 'w_branch_attn': w_branch_attn[i], 'w_branch_ssd': w_branch_ssd[i],
            'w_out': w_out[i], 'ffn2_norm': ffn2_norm[i], 'ffn2_w_gate': ffn2_w_gate[i],
            'ffn2_w_up': ffn2_w_up[i], 'ffn2_w_down': ffn2_w_down[i], 'pe_norm': pe_norm[i],
            'w_pe_gate': w_pe_gate[i], 'w_pe_proj': w_pe_proj[i],
        }
        empty_kv = [jnp.zeros((n_p, 2, 0, ATTN_HEADS_PER_GROUP, ATTN_HEAD_DIM), xp.dtype) for _ in ATTN_PATTERNS]
        xp, kvn, cn, sn, s5n = trunk_layer(
            xp, p_prompt[i], empty_kv,
            jnp.zeros((n_p, SSD_CONV - 1, SSD_CONV_CH), xp.dtype),
            jnp.zeros((n_p, SSD_HEADS, SSD_HEAD_DIM, SSD_STATE), jnp.float32),
            jnp.zeros((n_p, 2, S5_GROUPS, S5_STATE), jnp.float32),
            attn_rel_bias, lw)
        for g, (w, _) in enumerate(ATTN_PATTERNS):
            kv_p[g].append(kvn[g][:, :, len_p - min(w, len_p):])
        conv_p.append(cn)
        ssd_p.append(sn)
        s5_p.append(s5n)
        xs, kvn, cn, sn, s5n = trunk_layer(
            xs, p_sample[i], [c[i] for c in caches], state_conv[i], state_ssd[i], state_s5[i],
            attn_rel_bias, lw)
        for g in range(len(ATTN_PATTERNS)):
            kv_s[g].append(kvn[g])
        conv_s.append(cn)
        ssd_s.append(sn)
        s5_s.append(s5n)
    y_prompt = rmsnorm(xp, final_norm)
    y_sample = rmsnorm(xs, final_norm)
    return (y_prompt, y_sample,
            jnp.stack(kv_p[0], 0), jnp.stack(kv_s[0], 0),
            jnp.stack(kv_p[1], 0), jnp.stack(kv_s[1], 0),
            jnp.stack(kv_p[2], 0), jnp.stack(kv_s[2], 0),
            jnp.stack(ssd_p, 0), jnp.stack(ssd_s, 0),
            jnp.stack(conv_p, 0), jnp.stack(conv_s, 0),
            jnp.stack(s5_p, 0), jnp.stack(s5_s, 0))
```

```python
import functools
import math

import numpy as np
import jax
import jax.numpy as jnp
from jax import lax
from jax.experimental import pallas as pl
from jax.experimental.pallas import tpu as pltpu

F32 = jnp.float32
BF16 = jnp.bfloat16

D_MODEL = 1024
DEPTH = 4
HEAD_DIM = 64
ATTN_PATTERNS = ((128, 1), (512, 4), (2048, 16))
HPG = 4
GROUP_W = HPG * HEAD_DIM
ATTN_W = 3 * GROUP_W
QBLK = 128
N_BUCKETS = 32
BUCKET_MAX_DIST = 2048
NEG_INF = -1e30
SSD_HEADS = 12
SSD_P = 64
SSD_S = 64
SSD_GROUPS = 4
SSD_INNER = SSD_HEADS * SSD_P
SSD_CONV_CH = SSD_INNER + 2 * SSD_GROUPS * SSD_S
SSD_CHUNK = 128
S5_GROUPS = 48
S5_CH = 16
S5_STATE = 64
S5_W = S5_GROUPS * S5_CH
S5_TC = 8
S5_GB = 6
S5_BLK = 8 * S5_STATE
D_FF = 2816
PE_DIM = 256
RMS_EPS = 1e-6
LANES = 128
DT_PAD = LANES
OFF_Q, OFF_K, OFF_V = 0, ATTN_W, 2 * ATTN_W
OFF_Z = 3 * ATTN_W
OFF_XBC = OFF_Z + SSD_INNER
OFF_DT = OFF_XBC + SSD_CONV_CH
OFF_U = OFF_DT + DT_PAD
OFF_G = OFF_U + S5_W
IN_COLS_PACKED = OFF_G + 3 * D_MODEL
VMEM_LIMIT = 56 * 1024 * 1024


def _params(*sem):
    return pltpu.CompilerParams(dimension_semantics=sem, vmem_limit_bytes=VMEM_LIMIT)


def _rms(x, g):
    inv = lax.rsqrt(jnp.mean(x * x, axis=-1, keepdims=True) + RMS_EPS)
    return (x * inv) * g


def _dot(a, b):
    return jnp.dot(a, b, preferred_element_type=F32)


def _dot_nt(a, b):
    return lax.dot_general(a, b, (((1,), (1,)), ((), ())), preferred_element_type=F32)


def _dot_tn(a, b):
    return lax.dot_general(a, b, (((0,), (0,)), ((), ())), preferred_element_type=F32)


def _sigmoid(x):
    return jax.nn.sigmoid(x)


def _layer_spec(arr, layer):
    nd = arr.ndim - 1
    return pl.BlockSpec((None,) + tuple(arr.shape[1:]),
                        lambda *_: (layer,) + (0,) * nd,
                        pipeline_mode=pl.Buffered(1))


def _full_spec(arr):
    nd = arr.ndim
    return pl.BlockSpec(tuple(arr.shape), lambda *_: (0,) * nd, pipeline_mode=pl.Buffered(1))


def _ffn_body(*refs, has_pe, has_final):
    it = iter(refs)
    x_ref, g_ref, wg_ref, wu_ref, wd_ref = (next(it) for _ in range(5))
    if has_pe:
        p_ref, pn_ref, wpg_ref, wpp_ref = (next(it) for _ in range(4))
    if has_final:
        fn_ref = next(it)
    o_ref = next(it)
    act_ref = next(it)
    x = x_ref[...]
    h = _rms(x, g_ref[...]).astype(BF16)
    half = D_FF // 2
    for f0 in (0, half):
        gate = _dot(h, wg_ref[:, f0:f0 + half])
        up = _dot(h, wu_ref[:, f0:f0 + half])
        act_ref[:, f0:f0 + half] = (gate * _sigmoid(gate) * up).astype(BF16)
    y = x + 0.5 * _dot(act_ref[...], wd_ref[...])
    if has_pe:
        h2 = _rms(y, pn_ref[...]).astype(BF16)
        gt = _sigmoid(_dot(h2, wpg_ref[...]))
        y = y + gt * _dot(p_ref[...].astype(BF16), wpp_ref[...])
    if has_final:
        y = _rms(y, fn_ref[...])
    o_ref[...] = y


def _ffn(x, layer, tm, norm, wg, wu, wd, pe=None, final=None):
    m = x.shape[0]
    row = lambda i: (i, 0)
    args = [x, norm, wg, wu, wd]
    specs = [pl.BlockSpec((tm, D_MODEL), row), _layer_spec(norm, layer), _layer_spec(wg, layer),
             _layer_spec(wu, layer), _layer_spec(wd, layer)]
    if pe is not None:
        p_all, pn, wpg, wpp = pe
        args += [p_all, pn, wpg, wpp]
        specs += [pl.BlockSpec((None, tm, PE_DIM), lambda i: (layer, i, 0)), _layer_spec(pn, layer),
                  _layer_spec(wpg, layer), _layer_spec(wpp, layer)]
    if final is not None:
        args.append(final)
        specs.append(_full_spec(final))
    return pl.pallas_call(
        functools.partial(_ffn_body, has_pe=pe is not None, has_final=final is not None),
        grid=(m // tm,),
        in_specs=specs,
        out_specs=pl.BlockSpec((tm, D_MODEL), row),
        out_shape=jax.ShapeDtypeStruct((m, D_MODEL), F32),
        scratch_shapes=[pltpu.VMEM((tm, D_FF), BF16)],
        compiler_params=_params("arbitrary"),
        name="ffn",
    )(*args)


def _inproj_body(x_ref, g_ref, w_ref, qkv_ref, z_ref, xbc_ref, dt_ref, u_ref, gates_ref):
    h = _rms(x_ref[...], g_ref[...]).astype(BF16)
    for c0 in range(0, OFF_Z, ATTN_W):
        qkv_ref[:, c0:c0 + ATTN_W] = _dot(h, w_ref[:, c0:c0 + ATTN_W])
    z_ref[...] = _dot(h, w_ref[:, OFF_Z:OFF_XBC]).astype(BF16)
    xbc_ref[...] = _dot(h, w_ref[:, OFF_XBC:OFF_DT])
    dt_ref[...] = _dot(h, w_ref[:, OFF_DT:OFF_U])
    u_ref[...] = _dot(h, w_ref[:, OFF_U:OFF_G]).astype(BF16)
    for k in range(3):
        c0 = OFF_G + k * D_MODEL
        gates_ref[:, k * D_MODEL:(k + 1) * D_MODEL] = _dot(h, w_ref[:, c0:c0 + D_MODEL]).astype(BF16)


def _inproj(x, layer, tm, norm, w):
    m = x.shape[0]
    row = lambda i: (i, 0)
    widths = ((3 * ATTN_W, F32), (SSD_INNER, BF16), (SSD_CONV_CH, F32), (DT_PAD, F32),
              (S5_W, BF16), (3 * D_MODEL, BF16))
    return pl.pallas_call(
        _inproj_body,
        grid=(m // tm,),
        in_specs=[pl.BlockSpec((tm, D_MODEL), row), _layer_spec(norm, layer), _layer_spec(w, layer)],
        out_specs=[pl.BlockSpec((tm, wd), row) for wd, _ in widths],
        out_shape=[jax.ShapeDtypeStruct((m, wd), dt) for wd, dt in widths],
        compiler_params=_params("arbitrary"),
        name="inproj",
    )(x, norm, w)


def _merge_body(x_ref, o0, l0, o1, l1, o2, l2, yssd_ref, ys5_ref, gates_ref,
                wba_ref, wbs_ref, wglu_ref, wout_ref, out_ref):
    la, lb, lc = l0[...], l1[...], l2[...]
    mx = jnp.maximum(jnp.maximum(la, lb), lc)
    ea, eb, ec = jnp.exp(la - mx), jnp.exp(lb - mx), jnp.exp(lc - mx)
    attn = (ea * o0[...] + eb * o1[...] + ec * o2[...]) / (ea + eb + ec)
    acc = _sigmoid(gates_ref[:, 0:D_MODEL].astype(F32)) * _dot(attn.astype(BF16), wba_ref[...])
    acc = acc + _sigmoid(gates_ref[:, D_MODEL:2 * D_MODEL].astype(F32)) * _dot(yssd_ref[...], wbs_ref[...])
    gl = _dot(jax.nn.gelu(ys5_ref[...].astype(F32)).astype(BF16), wglu_ref[...])
    s5 = gl[:, :D_MODEL] * _sigmoid(gl[:, D_MODEL:])
    acc = acc + _sigmoid(gates_ref[:, 2 * D_MODEL:].astype(F32)) * s5
    out_ref[...] = x_ref[...] + _dot(acc.astype(BF16), wout_ref[...])


def _merge(x, layer, tm, attn_parts, y_ssd, y_s5, gates, wba, wbs, wglu, wout):
    m = x.shape[0]
    row = lambda i: (i, 0)
    args = [x] + list(attn_parts) + [y_ssd, y_s5, gates, wba, wbs, wglu, wout]
    specs = ([pl.BlockSpec((tm, D_MODEL), row)] + [pl.BlockSpec((tm, GROUP_W), row)] * 6
             + [pl.BlockSpec((tm, SSD_INNER), row), pl.BlockSpec((tm, S5_W), row),
                pl.BlockSpec((tm, 3 * D_MODEL), row)]
             + [_layer_spec(w, layer) for w in (wba, wbs, wglu, wout)])
    return pl.pallas_call(
        _merge_body,
        grid=(m // tm,),
        in_specs=specs,
        out_specs=pl.BlockSpec((tm, D_MODEL), row),
        out_shape=jax.ShapeDtypeStruct((m, D_MODEL), F32),
        compiler_params=_params("arbitrary"),
        name="merge",
    )(*args)


def _t5_bucket(dist):
    max_exact = N_BUCKETS // 2
    d = np.asarray(dist).astype(np.int32)
    df = np.maximum(d, 1).astype(np.float32)
    large = max_exact + (np.log(df / max_exact) / math.log(BUCKET_MAX_DIST / max_exact)
                         * (N_BUCKETS - max_exact)).astype(np.int32)
    return np.where(d < max_exact, d, np.minimum(large, N_BUCKETS - 1)).astype(np.int32)


def _prompt_bias(rel_bias, g, d):
    i = np.arange(QBLK)[:, None]
    c = np.arange(2 * QBLK)[None, :]
    steps = i + QBLK - c
    valid = (steps >= 0) & (steps <= QBLK)
    bucket = _t5_bucket(np.clip(steps, 0, QBLK) * d)
    tab = rel_bias[:, g * HPG:(g + 1) * HPG].astype(F32)
    b = jnp.transpose(tab[bucket], (2, 0, 1))
    return jnp.where(valid[None], b, NEG_INF)


def _attn_prompt_body(q_ref, kp_ref, kc_ref, vp_ref, vc_ref, bias_ref, o_ref, l_ref):
    first = pl.program_id(2) == 0
    pen = jnp.where(first, NEG_INF, 0.0).astype(F32)
    q = (q_ref[...] * (HEAD_DIM ** -0.5)).astype(BF16)
    kp, kc = kp_ref[...].astype(BF16), kc_ref[...].astype(BF16)
    vp, vc = vp_ref[...].astype(BF16), vc_ref[...].astype(BF16)
    for h in range(HPG):
        sl = slice(h * HEAD_DIM, (h + 1) * HEAD_DIM)
        qh = q[:, sl]
        sp = _dot_nt(qh, kp[:, sl]) + bias_ref[h, :, 0:QBLK] + pen
        sc = _dot_nt(qh, kc[:, sl]) + bias_ref[h, :, QBLK:2 * QBLK]
        m = jnp.maximum(jnp.max(sp, axis=-1, keepdims=True), jnp.max(sc, axis=-1, keepdims=True))
        ep, ec = jnp.exp(sp - m), jnp.exp(sc - m)
        den = jnp.sum(ep, axis=-1, keepdims=True) + jnp.sum(ec, axis=-1, keepdims=True)
        o = (_dot(ep.astype(BF16), vp[:, sl]) + _dot(ec.astype(BF16), vc[:, sl])) / den
        o_ref[:, sl] = o
        l_ref[:, sl] = jnp.broadcast_to(m + jnp.log(den), (QBLK, HEAD_DIM))


def _attn_prompt(qkv, bias, g, d, n, length):
    rows = length // d
    nb = rows // QBLK
    ncb = (3 * ATTN_W) // GROUP_W
    view = qkv.reshape(n, rows, d * 3 * ATTN_W)
    blk = (None, QBLK, GROUP_W)
    cur = lambda off: pl.BlockSpec(blk, lambda b, r, i: (b, i, r * ncb + off + g))
    prev = lambda off: pl.BlockSpec(blk, lambda b, r, i: (b, jnp.maximum(i - 1, 0), r * ncb + off + g))
    out_spec = pl.BlockSpec(blk, lambda b, r, i: (b, i, r))
    out_sds = jax.ShapeDtypeStruct((n, rows, d * GROUP_W), F32)
    o, l = pl.pallas_call(
        _attn_prompt_body,
        grid=(n, d, nb),
        in_specs=[cur(0), prev(3), cur(3), prev(6), cur(6), _full_spec(bias)],
        out_specs=[out_spec, out_spec],
        out_shape=[out_sds, out_sds],
        compiler_params=_params("arbitrary", "arbitrary", "arbitrary"),
        name=f"attn_prompt_g{g}",
    )(view, view, view, view, view, bias)
    return o.reshape(n * length, GROUP_W), l.reshape(n * length, GROUP_W)


def _sample_bias(rel_bias, g, d):
    steps = QBLK - np.arange(QBLK)
    tab = rel_bias[:, g * HPG:(g + 1) * HPG].astype(F32)
    cache_b = jnp.transpose(tab[_t5_bucket(steps * d)], (1, 0))
    self_b = jnp.broadcast_to(tab[0][:, None], (HPG, QBLK))
    pad = jnp.zeros((8 - HPG, QBLK), F32)
    return jnp.concatenate([cache_b, pad], 0), jnp.concatenate([self_b, pad], 0)


def _attn_sample_body(qkv_ref, kv_ref, cb_ref, sb_ref, o_ref, l_ref, *, g, bs):
    base = pl.program_id(0) * bs
    rowi = lax.broadcasted_iota(jnp.int32, (8, GROUP_W), 0)
    lane = lax.broadcasted_iota(jnp.int32, (8, GROUP_W), 1)
    own = (lane // HEAD_DIM) == rowi
    for j in range(bs):
        r = pl.ds(base + j, 1)
        q = qkv_ref[r, OFF_Q + g * GROUP_W:OFF_Q + (g + 1) * GROUP_W] * (HEAD_DIM ** -0.5)
        kn = qkv_ref[r, OFF_K + g * GROUP_W:OFF_K + (g + 1) * GROUP_W]
        vn = qkv_ref[r, OFF_V + g * GROUP_W:OFF_V + (g + 1) * GROUP_W]
        qblk = jnp.where(own, jnp.broadcast_to(q, (8, GROUP_W)), 0.0)
        s = _dot_nt(qblk.astype(BF16), kv_ref[j, 0].astype(BF16)) + cb_ref[...]
        s_self = jnp.sum(qblk * kn, axis=-1, keepdims=True) + sb_ref[:, 0:1]
        m = jnp.maximum(jnp.max(s, axis=-1, keepdims=True), s_self)
        e = jnp.exp(s - m)
        e_self = jnp.exp(s_self - m)
        den = jnp.sum(e, axis=-1, keepdims=True) + e_self
        o = (_dot(e.astype(BF16), kv_ref[j, 1].astype(BF16)) + e_self * vn) / den
        o_ref[pl.ds(j, 1), :] = jnp.sum(jnp.where(own, o, 0.0), axis=0, keepdims=True)
        lse = m + jnp.log(den)
        l_ref[pl.ds(j, 1), :] = jnp.sum(jnp.where(own, lse, 0.0), axis=0, keepdims=True)


def _attn_sample(qkv, cache, layer, biases, g, d, bs=8):
    n = qkv.shape[0]
    view = cache.reshape(DEPTH, n, 2, QBLK, d * GROUP_W)
    cb, sb = biases
    out_sds = jax.ShapeDtypeStruct((n, GROUP_W), F32)
    return pl.pallas_call(
        functools.partial(_attn_sample_body, g=g, bs=bs),
        grid=(n // bs,),
        in_specs=[_full_spec(qkv),
                  pl.BlockSpec((None, bs, 2, QBLK, GROUP_W), lambda s: (layer, s, 0, 0, 0)),
                  _full_spec(cb), _full_spec(sb)],
        out_specs=[pl.BlockSpec((bs, GROUP_W), lambda s: (s, 0))] * 2,
        out_shape=[out_sds, out_sds],
        compiler_params=_params("arbitrary"),
        name=f"attn_sample_g{g}",
    )(qkv, view, cb, sb)


def _softplus(x):
    return jnp.maximum(x, 0.0) + jnp.log(1.0 + jnp.exp(-jnp.abs(x)))


def _split3(x):
    hi = x.astype(BF16)
    r1 = x - hi.astype(F32)
    mid = r1.astype(BF16)
    lo = (r1 - mid.astype(F32)).astype(BF16)
    return hi, mid, lo


def _ssd_prompt_body(xbc_ref, z_ref, dt_ref, cw_ref, cb_ref, dtb_ref, alog_ref, dsk_ref, ng_ref,
                     y_ref, st_ref, xs_ref, state_ref, ybuf_ref):
    c = pl.program_id(1)
    q = SSD_CHUNK

    @pl.when(c == 0)
    def _():
        state_ref[...] = jnp.zeros_like(state_ref)
        xs_ref[0:8, :] = jnp.zeros((8, SSD_CONV_CH), F32)

    xs_ref[8:8 + q, :] = xbc_ref[...]
    conv = cb_ref[...]
    for k in range(4):
        conv = conv + cw_ref[k:k + 1, :] * xs_ref[5 + k:5 + k + q, :]
    xs_ref[0:8, :] = xs_ref[q:q + 8, :]
    xc = conv * _sigmoid(conv)

    dt = _softplus(dt_ref[...] + dtb_ref[...])
    a = -jnp.exp(alog_ref[...])
    da = dt * a
    ri = lax.broadcasted_iota(jnp.int32, (q, q), 0)
    ci = lax.broadcasted_iota(jnp.int32, (q, q), 1)
    causal = ri >= ci
    tril = jnp.where(causal, 1.0, 0.0).astype(BF16)
    cum = sum(_dot(tril, part) for part in _split3(da))
    cum_t = cum.T
    dt_t = dt.T
    cum_last = cum[q - 1:q, :]
    w_end = jnp.exp(cum_last - cum) * dt
    e_cum = jnp.exp(cum)
    e_last = jnp.exp(cum_last)

    for g in range(SSD_GROUPS):
        bg = xc[:, SSD_INNER + g * SSD_S:SSD_INNER + (g + 1) * SSD_S].astype(BF16)
        cg = xc[:, SSD_INNER + (SSD_GROUPS + g) * SSD_S:SSD_INNER + (SSD_GROUPS + g + 1) * SSD_S].astype(BF16)
        gmat = _dot_nt(cg, bg)
        for hh in range(SSD_HEADS // SSD_GROUPS):
            h = g * (SSD_HEADS // SSD_GROUPS) + hh
            sl = slice(h * SSD_P, (h + 1) * SSD_P)
            diff = cum[:, h:h + 1] - cum_t[h:h + 1, :]
            decay = jnp.where(causal, jnp.exp(jnp.where(causal, diff, 0.0)), 0.0)
            scores = gmat * decay * dt_t[h:h + 1, :]
            xh = xc[:, sl]
            hin = state_ref[h]
            y = _dot(scores.astype(BF16), xh.astype(BF16))
            y = y + _dot_nt(cg, hin.astype(BF16)) * e_cum[:, h:h + 1]
            y = y + dsk_ref[:, sl] * xh
            st = _dot_tn((xh * w_end[:, h:h + 1]).astype(BF16), bg)
            state_ref[h] = e_last[:, h:h + 1] * hin + st
            ybuf_ref[:, sl] = y

    zf = z_ref[...].astype(F32)
    yg = ybuf_ref[...] * (zf * _sigmoid(zf))
    y_ref[...] = _rms(yg, ng_ref[...]).astype(BF16)
    st_ref[...] = state_ref[...]


def _ssd_prompt(xbc, z, dt, layer, n, length, cw, cb, dtb, alog, dsk, ng):
    nc = length // SSD_CHUNK
    blk = lambda w: pl.BlockSpec((None, SSD_CHUNK, w), lambda b, c: (b, c, 0))
    return pl.pallas_call(
        _ssd_prompt_body,
        grid=(n, nc),
        in_specs=[blk(SSD_CONV_CH), blk(SSD_INNER), blk(DT_PAD)]
                 + [_layer_spec(w, layer) for w in (cw, cb, dtb, alog, dsk, ng)],
        out_specs=[blk(SSD_INNER),
                   pl.BlockSpec((None, SSD_HEADS, SSD_P, SSD_S), lambda b, c: (b, 0, 0, 0))],
        out_shape=[jax.ShapeDtypeStruct((n, length, SSD_INNER), BF16),
                   jax.ShapeDtypeStruct((n, SSD_HEADS, SSD_P, SSD_S), F32)],
        scratch_shapes=[pltpu.VMEM((SSD_CHUNK + 8, SSD_CONV_CH), F32),
                        pltpu.VMEM((SSD_HEADS, SSD_P, SSD_S), F32),
                        pltpu.VMEM((SSD_CHUNK, SSD_INNER), F32)],
        compiler_params=_params("arbitrary", "arbitrary"),
        name="ssd_prompt",
    )(xbc.reshape(n, length, SSD_CONV_CH), z.reshape(n, length, SSD_INNER),
      dt.reshape(n, length, DT_PAD), cw, cb, dtb, alog, dsk, ng)


def _ssd_conv_sample_body(xbc_ref, cs_ref, dt_ref, cw_ref, cb_ref, dtb_ref, xc_ref, dts_ref):
    conv = cb_ref[...] + cw_ref[3:4, :] * xbc_ref[...]
    for k in range(3):
        conv = conv + cw_ref[k:k + 1, :] * cs_ref[:, k * SSD_CONV_CH:(k + 1) * SSD_CONV_CH]
    xc_ref[...] = conv * _sigmoid(conv)
    dts_ref[...] = _softplus(dt_ref[...] + dtb_ref[...])


def _ssd_conv_sample(xbc, conv_state, dt, layer, cw, cb, dtb):
    n = xbc.shape[0]
    cs = conv_state.reshape(DEPTH, n, 3 * SSD_CONV_CH)
    return pl.pallas_call(
        _ssd_conv_sample_body,
        grid=(1,),
        in_specs=[_full_spec(xbc), _layer_spec(cs, layer), _full_spec(dt)]
                 + [_layer_spec(w, layer) for w in (cw, cb, dtb)],
        out_specs=[pl.BlockSpec((n, SSD_CONV_CH), lambda i: (0, 0)),
                   pl.BlockSpec((n, DT_PAD), lambda i: (0, 0))],
        out_shape=[jax.ShapeDtypeStruct((n, SSD_CONV_CH), F32), jax.ShapeDtypeStruct((n, DT_PAD), F32)],
        compiler_params=_params("arbitrary"),
        name="ssd_conv_sample",
    )(xbc, cs, dt, cw, cb, dtb)


def _ssd_state_sample_body(h0_ref, x_ref, b_ref, c_ref, dt_ref, alog_ref, dsk_ref, y_ref, hn_ref):
    n = x_ref.shape[0]
    ps = SSD_P * SSD_S
    x, bv, cv = x_ref[...], b_ref[...], c_ref[...]
    dt = dt_ref[...]
    dec = jnp.exp(dt * (-jnp.exp(alog_ref[...])))
    lane = lax.broadcasted_iota(jnp.int32, (SSD_P, ps), 1)
    rowp = lax.broadcasted_iota(jnp.int32, (SSD_P, ps), 0)
    rep = jnp.where(lane // SSD_S == rowp, 1.0, 0.0).astype(BF16)
    c2 = jnp.concatenate([cv, cv], axis=1)
    b2 = jnp.concatenate([bv, bv], axis=1)
    ct = jnp.concatenate([c2] * (ps // (2 * SSD_S)), axis=1)
    bt = jnp.concatenate([b2] * (ps // (2 * SSD_S)), axis=1)
    h0 = h0_ref[...]
    y_off = _dot_nt((h0 * ct).astype(BF16), rep) * dec[:, 0:SSD_P]
    cb = jnp.sum(cv * bv, axis=-1, keepdims=True)
    y_ref[...] = cb * dt[:, 0:SSD_P] * x + y_off + dsk_ref[:, 0:SSD_P] * x
    xrep = _dot(x.astype(BF16), rep)
    dtw = jnp.concatenate([dt] * (ps // LANES), axis=1)
    decw = jnp.concatenate([dec] * (ps // LANES), axis=1)
    hn_ref[...] = decw * h0 + dtw * xrep * bt


def _ssd_state_sample(state, layer, xh, dth, alog_h, dsk_h):
    n = xh.shape[1]
    ps = SSD_P * SSD_S
    rep = SSD_HEADS // SSD_GROUPS
    piece = lambda f: pl.BlockSpec((None, n, SSD_S), lambda h: (f(h), 0, 0))
    lanes = lambda: pl.BlockSpec((None, 1, LANES), lambda h: (layer * SSD_HEADS + h, 0, 0))
    return pl.pallas_call(
        _ssd_state_sample_body,
        grid=(SSD_HEADS,),
        in_specs=[pl.BlockSpec((None, n, ps), lambda h: (layer, 0, h)),
                  piece(lambda h: h), piece(lambda h: SSD_HEADS + h // rep),
                  piece(lambda h: SSD_HEADS + SSD_GROUPS + h // rep),
                  pl.BlockSpec((None, n, LANES), lambda h: (h, 0, 0)), lanes(), lanes()],
        out_specs=[pl.BlockSpec((None, n, SSD_P), lambda h: (h, 0, 0)),
                   pl.BlockSpec((n, ps), lambda h: (0, h))],
        out_shape=[jax.ShapeDtypeStruct((SSD_HEADS, n, SSD_P), F32),
                   jax.ShapeDtypeStruct((n, SSD_HEADS * ps), F32)],
        compiler_params=_params("arbitrary"),
        name="ssd_state_sample",
    )(state, xh, xh, xh, dth, alog_h, dsk_h)


def _ssd_gate_sample_body(y_ref, z_ref, ng_ref, o_ref):
    zf = z_ref[...].astype(F32)
    o_ref[...] = _rms(y_ref[...] * (zf * _sigmoid(zf)), ng_ref[...]).astype(BF16)


def _ssd_gate_sample(y, z, layer, ng):
    n = y.shape[0]
    return pl.pallas_call(
        _ssd_gate_sample_body,
        grid=(1,),
        in_specs=[_full_spec(y), _full_spec(z), _layer_spec(ng, layer)],
        out_specs=pl.BlockSpec((n, SSD_INNER), lambda i: (0, 0)),
        out_shape=jax.ShapeDtypeStruct((n, SSD_INNER), BF16),
        compiler_params=_params("arbitrary"),
        name="ssd_gate_sample",
    )(y, z, ng)


N_POW = S5_TC + 1
N_SCAN = 12


def _cmul(a, b):
    return a[0] * b[0] - a[1] * b[1], a[0] * b[1] + a[1] * b[0]


def _s5_prep_body(lr_ref, li_ref, ldt_ref, btr_ref, bti_ref, cr_ref, ci_ref,
                  kt_ref, wt_ref, cp_ref, ap_ref):
    lr, li = lr_ref[...], li_ref[...]
    delta = jnp.exp(ldt_ref[...])
    mag = jnp.exp(lr * delta)
    ab = (mag * jnp.cos(li * delta), mag * jnp.sin(li * delta))
    nr, ni = ab[0] - 1.0, ab[1]
    den = lr * lr + li * li
    f = ((nr * lr + ni * li) / den, (ni * lr - nr * li) / den)
    bb = _cmul(f, (btr_ref[...], bti_ref[...]))
    cc = (cr_ref[...], ci_ref[...])
    pw = [(jnp.ones_like(lr), jnp.zeros_like(lr))]
    for _ in range(S5_TC):
        pw.append(_cmul(pw[-1], ab))
    cps = [_cmul(cc, p) for p in pw]
    for tau in range(N_POW):
        cp_ref[0, tau] = cps[tau][0]
        cp_ref[1, tau] = -cps[tau][1]
    for j in range(S5_TC):
        w = _cmul(bb, pw[S5_TC - 1 - j])
        wt_ref[0, j] = w[0]
        wt_ref[1, j] = w[1]
    cpr = jnp.concatenate([c[0] for c in cps], axis=1).astype(BF16)
    cpi = jnp.concatenate([c[1] for c in cps], axis=1).astype(BF16)
    bbr, bbi = bb[0].astype(BF16), bb[1].astype(BF16)
    nt = (((2,), (2,)), ((0,), (0,)))
    kt_ref[...] = (lax.dot_general(cpr, bbr, nt, preferred_element_type=F32)
                   - lax.dot_general(cpi, bbi, nt, preferred_element_type=F32))
    a8 = pw[S5_TC]
    a16 = _cmul(a8, a8)
    a32 = _cmul(a16, a16)
    rows = [ab, a16, a32, a8]
    for _ in range(S5_TC - 1):
        rows.append(_cmul(rows[-1], a8))
    rows.append((jnp.zeros_like(lr), jnp.zeros_like(lr)))
    for k, rw in enumerate(rows):
        ap_ref[0, k] = rw[0]
        ap_ref[1, k] = rw[1]


def _s5_prep(a_re, a_im, log_dt, b_re, b_im, c_re, c_im):
    g, s, c = S5_GROUPS, S5_STATE, S5_CH
    lr = a_re.reshape(DEPTH, g, 1, s)
    li = a_im.reshape(DEPTH, g, 1, s)
    ldt = jnp.broadcast_to(log_dt[:, :, None, None], (DEPTH, g, 1, s))
    btr = jnp.swapaxes(b_re, 2, 3)
    bti = jnp.swapaxes(b_im, 2, 3)
    args = (lr, li, ldt, btr, bti, c_re, c_im)
    spec = lambda a: pl.BlockSpec((None,) + tuple(a.shape[1:]), lambda l: (l,) + (0,) * (a.ndim - 1))
    out_shapes = [jax.ShapeDtypeStruct((DEPTH, g, N_POW * c, c), F32),
                  jax.ShapeDtypeStruct((DEPTH, 2, S5_TC, g, c, s), F32),
                  jax.ShapeDtypeStruct((DEPTH, 2, N_POW, g, c, s), F32),
                  jax.ShapeDtypeStruct((DEPTH, 2, N_SCAN, g, 1, s), F32)]
    return pl.pallas_call(
        _s5_prep_body,
        grid=(DEPTH,),
        in_specs=[spec(a) for a in args],
        out_specs=[spec(o) for o in out_shapes],
        out_shape=out_shapes,
        compiler_params=_params("arbitrary"),
        name="s5_prep",
    )(*args)


def _s5_tables(kt, wt, cp, ap, d_skip):
    c, s, tc, gb = S5_CH, S5_STATE, S5_TC, S5_GB
    eye = jnp.eye(8, dtype=bool)
    k5 = kt.reshape(DEPTH, gb, 8, N_POW, c, c)[:, :, :, :tc]
    toe = jnp.stack([jnp.pad(k5[:, :, :, :tc - j], ((0, 0), (0, 0), (0, 0), (j, 0), (0, 0), (0, 0)))
                     for j in range(tc)], axis=3)
    toe = jnp.transpose(toe, (0, 1, 3, 2, 6, 4, 5))
    t_blk = jnp.where(eye[None, None, None, :, None, None, :, None], toe[:, :, :, :, :, :, None, :], 0.0)
    t_blk = t_blk.reshape(DEPTH, gb, tc * 128, tc * 128)
    w6 = wt.reshape(DEPTH, 2, tc, gb, 8, c, s)
    w6 = jnp.transpose(w6, (0, 3, 2, 4, 5, 1, 6))
    w_blk = jnp.where(eye[None, None, None, :, None, None, :, None], w6[:, :, :, :, :, :, None, :], 0.0)
    w_blk = w_blk.reshape(DEPTH, gb, tc * 128, 2 * S5_BLK)
    tw = jnp.concatenate([t_blk, w_blk], axis=-1).astype(BF16)
    v6 = cp.reshape(DEPTH, 2, N_POW, gb, 8, c, s)
    v6t = jnp.transpose(v6, (0, 3, 1, 4, 6, 2, 5))
    v_all = jnp.where(eye[None, None, None, :, None, None, :, None], v6t[:, :, :, :, :, :, None, :], 0.0)
    v_blk = v_all[:, :, :, :, :, 1:].reshape(DEPTH, gb, 2 * S5_BLK, tc * 128).astype(BF16)
    v0_blk = v_all[:, :, :, :, :, 0].reshape(DEPTH, gb, 2 * S5_BLK, 128).astype(BF16)
    scan = jnp.transpose(ap.reshape(DEPTH, 2, N_SCAN, gb, S5_BLK), (0, 3, 1, 2, 4))
    pad = jnp.zeros((DEPTH, gb, 2, 16 - N_SCAN, S5_BLK), F32)
    scan = jnp.concatenate([scan, pad], axis=3)
    d_blk = d_skip.reshape(DEPTH, gb, 1, 128)
    return tw, v_blk, v0_blk, scan, d_blk


def _s5_prompt_body(*refs):
    u_refs = refs[0:S5_TC]
    tw_ref, v_ref, sc_ref, d_ref, y_ref, hout_ref, sre_ref, sim_ref, hin_ref, cr_ref, ci_ref = refs[S5_TC:]
    r = pl.program_id(2)
    tr = sre_ref.shape[0]

    @pl.when(r == 0)
    def _():
        cr_ref[...] = jnp.zeros_like(cr_ref)
        ci_ref[...] = jnp.zeros_like(ci_ref)

    u = jnp.concatenate([ur[...] for ur in u_refs], axis=1)
    tw = _dot(u, tw_ref[...])
    yd = tw[:, 0:S5_TC * 128]
    sre_ref[...] = tw[:, S5_TC * 128:S5_TC * 128 + S5_BLK]
    sim_ref[...] = tw[:, S5_TC * 128 + S5_BLK:]

    a8 = (sc_ref[0, 3:4, :], sc_ref[1, 3:4, :])
    a16 = (sc_ref[0, 1:2, :], sc_ref[1, 1:2, :])
    a32 = (sc_ref[0, 2:3, :], sc_ref[1, 2:3, :])
    apw = (sc_ref[0, 3:11, :], sc_ref[1, 3:11, :])
    row = lax.broadcasted_iota(jnp.int32, (8, S5_BLK), 0)

    def shift(x, k):
        return jnp.where(row >= k, pltpu.roll(x, k, axis=0), 0.0)

    def tile(t, carry):
        c_re, c_im = carry
        rows = pl.ds(pl.multiple_of(t * 8, 8), 8)
        xr, xi = sre_ref[rows, :], sim_ref[rows, :]
        for k, (ar, ai) in ((1, a8), (2, a16), (4, a32)):
            pr, pi = shift(xr, k), shift(xi, k)
            xr, xi = xr + ar * pr - ai * pi, xi + ar * pi + ai * pr
        xr = xr + apw[0] * c_re - apw[1] * c_im
        xi = xi + apw[0] * c_im + apw[1] * c_re
        hin_ref[rows, 0:S5_BLK] = jnp.where(row >= 1, pltpu.roll(xr, 1, axis=0), c_re)
        hin_ref[rows, S5_BLK:] = jnp.where(row >= 1, pltpu.roll(xi, 1, axis=0), c_im)
        return xr[7:8, :], xi[7:8, :]

    c_re, c_im = lax.fori_loop(0, tr // 8, tile, (cr_ref[...], ci_ref[...]))
    cr_ref[...] = c_re
    ci_ref[...] = c_im
    hout_ref[:, 0:S5_BLK] = c_re
    hout_ref[:, S5_BLK:] = c_im

    d_row = jnp.concatenate([d_ref[...]] * S5_TC, axis=1)
    y = yd + _dot(hin_ref[...].astype(BF16), v_ref[...]) + d_row * u.astype(F32)
    for j in range(S5_TC):
        y_ref[j] = y[:, j * 128:(j + 1) * 128].astype(BF16)


def _s5_prompt(u, layer, n, length, tw, v_blk, scan, d_blk, tr=512):
    rows = length // S5_TC
    tr = min(tr, rows)
    nr = rows // tr
    u8 = u.reshape(n * rows, S5_TC * S5_W)
    u_spec = lambda j: pl.BlockSpec((tr, 128), lambda gb, b, r: (b * nr + r, j * S5_GB + gb))
    lay = lambda a: pl.BlockSpec((None, None) + tuple(a.shape[2:]),
                                 lambda gb, b, r: (layer, gb) + (0,) * (a.ndim - 2))
    y8, hout = pl.pallas_call(
        _s5_prompt_body,
        grid=(S5_GB, n, nr),
        in_specs=[u_spec(j) for j in range(S5_TC)] + [lay(tw), lay(v_blk), lay(scan), lay(d_blk)],
        out_specs=[pl.BlockSpec((S5_TC, tr, 128), lambda gb, b, r: (0, b * nr + r, gb)),
                   pl.BlockSpec((None, None, 1, 2 * S5_BLK), lambda gb, b, r: (b, gb, 0, 0))],
        out_shape=[jax.ShapeDtypeStruct((S5_TC, n * rows, S5_W), BF16),
                   jax.ShapeDtypeStruct((n, S5_GB, 1, 2 * S5_BLK), F32)],
        scratch_shapes=[pltpu.VMEM((tr, S5_BLK), F32), pltpu.VMEM((tr, S5_BLK), F32),
                        pltpu.VMEM((tr, 2 * S5_BLK), F32),
                        pltpu.VMEM((1, S5_BLK), F32), pltpu.VMEM((1, S5_BLK), F32)],
        compiler_params=_params("arbitrary", "arbitrary", "arbitrary"),
        name="s5_prompt",
    )(*([u8] * S5_TC), tw, v_blk, scan, d_blk)
    y = jnp.transpose(y8, (1, 0, 2)).reshape(n * length, S5_W)
    state = jnp.transpose(hout.reshape(n, S5_GB, 2, 8, S5_STATE), (0, 2, 1, 3, 4))
    return y, state.reshape(n, 2, S5_GROUPS, S5_STATE)


def _s5_sample_body(u_ref, hr_ref, hi_ref, w_ref, v0_ref, sc_ref, d_ref, y_ref, nr_ref, ni_ref):
    u = u_ref[...]
    bu = _dot(u, w_ref[...])
    ar, ai = sc_ref[0, 0:1, :], sc_ref[1, 0:1, :]
    hr, hi = hr_ref[...], hi_ref[...]
    nr = ar * hr - ai * hi + bu[:, 0:S5_BLK]
    ni = ar * hi + ai * hr + bu[:, S5_BLK:]
    nr_ref[...] = nr
    ni_ref[...] = ni
    hcat = jnp.concatenate([nr, ni], axis=1).astype(BF16)
    y_ref[...] = (_dot(hcat, v0_ref[...]) + d_ref[...] * u.astype(F32)).astype(BF16)


def _s5_sample(u, state, layer, tw, v0_blk, scan, d_blk):
    n = u.shape[0]
    lay = lambda a: pl.BlockSpec((None, None) + tuple(a.shape[2:]),
                                 lambda gb: (layer, gb) + (0,) * (a.ndim - 2))
    st = lambda off: pl.BlockSpec((None, n, S5_BLK), lambda gb: (layer, 0, off + gb))
    half = S5_GROUPS * S5_STATE
    y, nr, ni = pl.pallas_call(
        _s5_sample_body,
        grid=(S5_GB,),
        in_specs=[pl.BlockSpec((n, 128), lambda gb: (0, gb)), st(0), st(S5_GB),
                  pl.BlockSpec((None, None, 128, 2 * S5_BLK), lambda gb: (layer, gb, S5_TC - 1, 1)),
                  lay(v0_blk), lay(scan), lay(d_blk)],
        out_specs=[pl.BlockSpec((n, 128), lambda gb: (0, gb)),
                   pl.BlockSpec((n, S5_BLK), lambda gb: (0, gb)),
                   pl.BlockSpec((n, S5_BLK), lambda gb: (0, gb))],
        out_shape=[jax.ShapeDtypeStruct((n, S5_W), BF16), jax.ShapeDtypeStruct((n, half), F32),
                   jax.ShapeDtypeStruct((n, half), F32)],
        compiler_params=_params("arbitrary"),
        name="s5_sample",
    )(u, state, state, tw, v0_blk, scan, d_blk)
    return y, jnp.stack([nr, ni], axis=1).reshape(n, 2, S5_GROUPS, S5_STATE)


def kernel(x_prompt, x_sample, cache_kv_w128, cache_kv_w512, cache_kv_w2048, state_ssd, state_conv, state_s5, p_prompt, p_sample, attn_rel_bias, ffn1_norm, ffn1_w_gate, ffn1_w_up, ffn1_w_down, mix_norm, w_in, ssd_conv_w, ssd_conv_b, ssd_dt_bias, ssd_a_log, ssd_d, ssd_norm, s5_a_re, s5_a_im, s5_b_re, s5_b_im, s5_c_re, s5_c_im, s5_d, s5_log_dt, w_s5_glu, w_branch_attn, w_branch_ssd, w_out, ffn2_norm, ffn2_w_gate, ffn2_w_up, ffn2_w_down, pe_norm, w_pe_gate, w_pe_proj, final_norm):
    n_p, len_p, _ = x_prompt.shape
    n_s = x_sample.shape[0]
    m_p = n_p * len_p
    assert x_sample.shape[1] == 1 and len_p % (QBLK * ATTN_PATTERNS[-1][1]) == 0
    caches = (cache_kv_w128, cache_kv_w512, cache_kv_w2048)
    for cache, (w, d) in zip(caches, ATTN_PATTERNS):
        assert cache.shape[3] == w == QBLK * d

    bf = lambda a: a.astype(BF16)
    vec = lambda a: a.reshape(DEPTH, 1, a.shape[-1])
    w1g, w1u, w1d = bf(ffn1_w_gate), bf(ffn1_w_up), bf(ffn1_w_down)
    w2g, w2u, w2d = bf(ffn2_w_gate), bf(ffn2_w_up), bf(ffn2_w_down)
    dt_lo = OFF_XBC + SSD_CONV_CH
    n_dt = SSD_HEADS
    w_in_p = bf(jnp.concatenate(
        [w_in[..., :dt_lo + n_dt], jnp.zeros((DEPTH, D_MODEL, DT_PAD - n_dt), w_in.dtype),
         w_in[..., dt_lo + n_dt:]], axis=-1))
    wba, wbs, wglu, wout = bf(w_branch_attn), bf(w_branch_ssd), bf(w_s5_glu), bf(w_out)
    wpg, wpp = bf(w_pe_gate), bf(w_pe_proj)
    n1, nm, n2, npe = vec(ffn1_norm), vec(mix_norm), vec(ffn2_norm), vec(pe_norm)
    fin = final_norm.reshape(1, D_MODEL)
    cb = vec(ssd_conv_b)
    lane_pad = lambda a: jnp.pad(a, ((0, 0), (0, DT_PAD - a.shape[-1]))).reshape(DEPTH, 1, DT_PAD)
    dtb, alog = lane_pad(ssd_dt_bias), lane_pad(ssd_a_log)
    dsk = jnp.repeat(ssd_d, SSD_P, axis=-1).reshape(DEPTH, 1, SSD_INNER)
    ng = vec(ssd_norm)
    per_head = lambda a: jnp.broadcast_to(a.reshape(DEPTH * SSD_HEADS, 1, 1), (DEPTH * SSD_HEADS, 1, LANES))
    alog_h, dsk_h = per_head(ssd_a_log), per_head(ssd_d)

    kt, wt, cp, ap = _s5_prep(s5_a_re, s5_a_im, s5_log_dt, s5_b_re, s5_b_im, s5_c_re, s5_c_im)
    tw, v_blk, v0_blk, scan, d_blk = _s5_tables(kt, wt, cp, ap, s5_d)

    bias_p = [_prompt_bias(attn_rel_bias, g, d) for g, (_, d) in enumerate(ATTN_PATTERNS)]
    bias_s = [_sample_bias(attn_rel_bias, g, d) for g, (_, d) in enumerate(ATTN_PATTERNS)]

    pp = p_prompt.reshape(DEPTH, m_p, PE_DIM)
    psm = p_sample.reshape(DEPTH, n_s, PE_DIM)
    st_ssd = state_ssd.reshape(DEPTH, n_s, SSD_HEADS * SSD_P * SSD_S)
    st_s5 = state_s5.reshape(DEPTH, n_s, 2 * S5_GROUPS * S5_STATE)

    xp = x_prompt.reshape(m_p, D_MODEL)
    xs = x_sample.reshape(n_s, D_MODEL)
    tm_p, tm_s = 256, n_s
    kv_p, kv_s = [[], [], []], [[], [], []]
    ssd_p, ssd_s, conv_p, conv_s, s5_p, s5_s = [], [], [], [], [], []

    def kv_pair(qkv, g, n, length, keep):
        k = qkv[:, OFF_K + g * GROUP_W:OFF_K + (g + 1) * GROUP_W].reshape(n, length, HPG, HEAD_DIM)
        v = qkv[:, OFF_V + g * GROUP_W:OFF_V + (g + 1) * GROUP_W].reshape(n, length, HPG, HEAD_DIM)
        return jnp.stack([k[:, length - keep:], v[:, length - keep:]], axis=1)

    for i in range(DEPTH):
        last = i == DEPTH - 1
        xp = _ffn(xp, i, tm_p, n1, w1g, w1u, w1d)
        qkv, z, xbc, dt, u, gates = _inproj(xp, i, tm_p, nm, w_in_p)
        parts = []
        for g, (w, d) in enumerate(ATTN_PATTERNS):
            parts += list(_attn_prompt(qkv, bias_p[g], g, d, n_p, len_p))
            kv_p[g].append(kv_pair(qkv, g, n_p, len_p, min(w, len_p)))
        y_ssd, st = _ssd_prompt(xbc, z, dt, i, n_p, len_p, ssd_conv_w, cb, dtb, alog, dsk, ng)
        ssd_p.append(st)
        conv_p.append(xbc.reshape(n_p, len_p, SSD_CONV_CH)[:, len_p - 3:])
        y_s5, st5 = _s5_prompt(u, i, n_p, len_p, tw, v_blk, scan, d_blk)
        s5_p.append(st5)
        xp = _merge(xp, i, tm_p, parts, y_ssd.reshape(m_p, SSD_INNER), y_s5, gates, wba, wbs, wglu, wout)
        xp = _ffn(xp, i, tm_p, n2, w2g, w2u, w2d, pe=(pp, npe, wpg, wpp), final=fin if last else None)
        xs = _ffn(xs, i, tm_s, n1, w1g, w1u, w1d)
        qkv, z, xbc, dt, u, gates = _inproj(xs, i, tm_s, nm, w_in_p)
        parts = []
        for g, (w, d) in enumerate(ATTN_PATTERNS):
            parts += list(_attn_sample(qkv, caches[g], i, bias_s[g], g, d))
            kv_s[g].append(kv_pair(qkv, g, n_s, 1, 1))
        xc, dts = _ssd_conv_sample(xbc, state_conv, dt, i, ssd_conv_w, cb, dtb)
        conv_s.append(jnp.concatenate([state_conv[i][:, 1:], xbc[:, None, :]], axis=1))
        xh = jnp.transpose(xc.reshape(n_s, SSD_CONV_CH // SSD_S, SSD_S), (1, 0, 2))
        dth = jnp.broadcast_to(jnp.transpose(dts[:, :SSD_HEADS])[:, :, None], (SSD_HEADS, n_s, LANES))
        yh, hn = _ssd_state_sample(st_ssd, i, xh, dth, alog_h, dsk_h)
        ssd_s.append(hn.reshape(n_s, SSD_HEADS, SSD_P, SSD_S))
        y_ssd = _ssd_gate_sample(jnp.transpose(yh, (1, 0, 2)).reshape(n_s, SSD_INNER), z, i, ng)
        y_s5, st5 = _s5_sample(u, st_s5, i, tw, v0_blk, scan, d_blk)
        s5_s.append(st5)
        xs = _merge(xs, i, tm_s, parts, y_ssd, y_s5, gates, wba, wbs, wglu, wout)
        xs = _ffn(xs, i, tm_s, n2, w2g, w2u, w2d, pe=(psm, npe, wpg, wpp), final=fin if last else None)

    return (xp.reshape(n_p, len_p, D_MODEL), xs.reshape(n_s, 1, D_MODEL),
            jnp.stack(kv_p[0], 0), jnp.stack(kv_s[0], 0),
            jnp.stack(kv_p[1], 0), jnp.stack(kv_s[1], 0),
            jnp.stack(kv_p[2], 0), jnp.stack(kv_s[2], 0),
            jnp.stack(ssd_p, 0), jnp.stack(ssd_s, 0),
            jnp.stack(conv_p, 0), jnp.stack(conv_s, 0),
            jnp.stack(s5_p, 0), jnp.stack(s5_s, 0))
```

```python
import functools
import math

import numpy as np
import jax
import jax.numpy as jnp
from jax import lax
from jax.experimental import pallas as pl
from jax.experimental.pallas import tpu as pltpu

F32 = jnp.float32
BF16 = jnp.bfloat16

D_MODEL = 1024
DEPTH = 4
HEAD_DIM = 64
ATTN_PATTERNS = ((128, 1), (512, 4), (2048, 16))
HPG = 4
GROUP_W = HPG * HEAD_DIM
ATTN_W = 3 * GROUP_W
QBLK = 128
N_BUCKETS = 32
BUCKET_MAX_DIST = 2048
NEG_INF = -1e30
SSD_HEADS = 12
SSD_P = 64
SSD_S = 64
SSD_GROUPS = 4
SSD_INNER = SSD_HEADS * SSD_P
SSD_CONV_CH = SSD_INNER + 2 * SSD_GROUPS * SSD_S
SSD_CHUNK = 128
S5_GROUPS = 48
S5_CH = 16
S5_STATE = 64
S5_W = S5_GROUPS * S5_CH
S5_TC = 8
S5_GB = 6
S5_BLK = 8 * S5_STATE
D_FF = 2816
PE_DIM = 256
RMS_EPS = 1e-6
LANES = 128
DT_PAD = LANES
OFF_Q, OFF_K, OFF_V = 0, ATTN_W, 2 * ATTN_W
OFF_Z = 3 * ATTN_W
OFF_XBC = OFF_Z + SSD_INNER
OFF_DT = OFF_XBC + SSD_CONV_CH
OFF_U = OFF_DT + DT_PAD
OFF_G = OFF_U + S5_W
IN_COLS_PACKED = OFF_G + 3 * D_MODEL
VMEM_LIMIT = 56 * 1024 * 1024


def _params(*sem):
    return pltpu.CompilerParams(dimension_semantics=sem, vmem_limit_bytes=VMEM_LIMIT)


def _rms(x, g):
    inv = lax.rsqrt(jnp.mean(x * x, axis=-1, keepdims=True) + RMS_EPS)
    return (x * inv) * g


def _dot(a, b):
    return jnp.dot(a, b, preferred_element_type=F32)


def _dot_nt(a, b):
    return lax.dot_general(a, b, (((1,), (1,)), ((), ())), preferred_element_type=F32)


def _dot_tn(a, b):
    return lax.dot_general(a, b, (((0,), (0,)), ((), ())), preferred_element_type=F32)


def _sigmoid(x):
    return jax.nn.sigmoid(x)


def _layer_spec(arr, layer):
    nd = arr.ndim - 1
    return pl.BlockSpec((None,) + tuple(arr.shape[1:]),
                        lambda *_: (layer,) + (0,) * nd,
                        pipeline_mode=pl.Buffered(1))


def _full_spec(arr):
    nd = arr.ndim
    return pl.BlockSpec(tuple(arr.shape), lambda *_: (0,) * nd, pipeline_mode=pl.Buffered(1))


def _ffn_body(*refs, has_pe, has_final):
    it = iter(refs)
    x_ref, g_ref, wg_ref, wu_ref, wd_ref = (next(it) for _ in range(5))
    if has_pe:
        p_ref, pn_ref, wpg_ref, wpp_ref = (next(it) for _ in range(4))
    if has_final:
        fn_ref = next(it)
    o_ref = next(it)
    act_ref = next(it)
    x = x_ref[...]
    h = _rms(x, g_ref[...]).astype(BF16)
    half = D_FF // 2
    for f0 in (0, half):
        gate = _dot(h, wg_ref[:, f0:f0 + half])
        up = _dot(h, wu_ref[:, f0:f0 + half])
        act_ref[:, f0:f0 + half] = (gate * _sigmoid(gate) * up).astype(BF16)
    y = x + 0.5 * _dot(act_ref[...], wd_ref[...])
    if has_pe:
        h2 = _rms(y, pn_ref[...]).astype(BF16)
        gt = _sigmoid(_dot(h2, wpg_ref[...]))
        y = y + gt * _dot(p_ref[...].astype(BF16), wpp_ref[...])
    if has_final:
        y = _rms(y, fn_ref[...])
    o_ref[...] = y


def _ffn(x, layer, tm, norm, wg, wu, wd, pe=None, final=None):
    m = x.shape[0]
    row = lambda i: (i, 0)
    args = [x, norm, wg, wu, wd]
    specs = [pl.BlockSpec((tm, D_MODEL), row), _layer_spec(norm, layer), _layer_spec(wg, layer),
             _layer_spec(wu, layer), _layer_spec(wd, layer)]
    if pe is not None:
        p_all, pn, wpg, wpp = pe
        args += [p_all, pn, wpg, wpp]
        specs += [pl.BlockSpec((None, tm, PE_DIM), lambda i: (layer, i, 0)), _layer_spec(pn, layer),
                  _layer_spec(wpg, layer), _layer_spec(wpp, layer)]
    if final is not None:
        args.append(final)
        specs.append(_full_spec(final))
    return pl.pallas_call(
        functools.partial(_ffn_body, has_pe=pe is not None, has_final=final is not None),
        grid=(m // tm,),
        in_specs=specs,
        out_specs=pl.BlockSpec((tm, D_MODEL), row),
        out_shape=jax.ShapeDtypeStruct((m, D_MODEL), F32),
        scratch_shapes=[pltpu.VMEM((tm, D_FF), BF16)],
        compiler_params=_params("arbitrary"),
        name="ffn",
    )(*args)


def _inproj_body(x_ref, g_ref, w_ref, qa_ref, qb_ref, qc_ref, z_ref, xbc_ref, dt_ref, u_ref, gates_ref,
                 stage_ref, *, dils):
    h = _rms(x_ref[...], g_ref[...]).astype(BF16)
    tm = x_ref.shape[0]
    for g, (out_ref, d) in enumerate(zip((qa_ref, qb_ref, qc_ref), dils)):
        res = _dot(h, w_ref[:, g * ATTN_W:(g + 1) * ATTN_W])
        if d == 1:
            out_ref[0] = res
        else:
            for c in range(ATTN_W // LANES):
                stage_ref[c] = res[:, c * LANES:(c + 1) * LANES]
            for r in range(d):
                for c in range(ATTN_W // LANES):
                    out_ref[r, :, c * LANES:(c + 1) * LANES] = stage_ref[c, pl.ds(r, tm // d, stride=d), :]
    z_ref[...] = _dot(h, w_ref[:, OFF_Z:OFF_XBC]).astype(BF16)
    xbc_ref[...] = _dot(h, w_ref[:, OFF_XBC:OFF_DT])
    dt_ref[...] = _dot(h, w_ref[:, OFF_DT:OFF_U])
    u_ref[...] = _dot(h, w_ref[:, OFF_U:OFF_G]).astype(BF16)
    for k in range(3):
        c0 = OFF_G + k * D_MODEL
        gates_ref[:, k * D_MODEL:(k + 1) * D_MODEL] = _dot(h, w_ref[:, c0:c0 + D_MODEL]).astype(BF16)


def _inproj(x, layer, tm, norm, w, n, length, dils):
    m = x.shape[0]
    tps = length // tm
    row = lambda i: (i, 0)
    widths = ((SSD_INNER, BF16), (SSD_CONV_CH, F32), (DT_PAD, F32), (S5_W, BF16), (3 * D_MODEL, BF16))
    q_specs = [pl.BlockSpec((None, d, tm // d, ATTN_W), lambda i: (i // tps, 0, i % tps, 0)) for d in dils]
    q_shapes = [jax.ShapeDtypeStruct((n, d, length // d, ATTN_W), F32) for d in dils]
    return pl.pallas_call(
        functools.partial(_inproj_body, dils=dils),
        grid=(m // tm,),
        in_specs=[pl.BlockSpec((tm, D_MODEL), row), _layer_spec(norm, layer), _layer_spec(w, layer)],
        out_specs=q_specs + [pl.BlockSpec((tm, wd), row) for wd, _ in widths],
        out_shape=q_shapes + [jax.ShapeDtypeStruct((m, wd), dt) for wd, dt in widths],
        scratch_shapes=[pltpu.VMEM((ATTN_W // LANES, tm, LANES), F32)],
        compiler_params=_params("arbitrary"),
        name="inproj",
    )(x, norm, w)


def _merge_body(x_ref, o0, l0, o1, l1, o2, l2, yssd_ref, ys5_ref, gates_ref,
                wba_ref, wbs_ref, wglu_ref, wout_ref, out_ref, stage_ref, *, dils):
    tm = x_ref.shape[0]
    parts = []
    for k, (ref, d) in enumerate(zip((o0, l0, o1, l1, o2, l2), (dils[0], dils[0], dils[1], dils[1], dils[2], dils[2]))):
        if d == 1:
            parts.append(ref[0])
        else:
            nl = GROUP_W // LANES
            for r in range(d):
                for c in range(nl):
                    stage_ref[k * nl + c, pl.ds(r, tm // d, stride=d), :] = ref[r, :, c * LANES:(c + 1) * LANES]
            parts.append(jnp.concatenate([stage_ref[k * nl + c] for c in range(nl)], axis=1))
    oa, la, ob, lb, oc, lc = parts
    mx = jnp.maximum(jnp.maximum(la, lb), lc)
    ea, eb, ec = jnp.exp(la - mx), jnp.exp(lb - mx), jnp.exp(lc - mx)
    attn = (ea * oa + eb * ob + ec * oc) / (ea + eb + ec)
    acc = _sigmoid(gates_ref[:, 0:D_MODEL].astype(F32)) * _dot(attn.astype(BF16), wba_ref[...])
    acc = acc + _sigmoid(gates_ref[:, D_MODEL:2 * D_MODEL].astype(F32)) * _dot(yssd_ref[...], wbs_ref[...])
    gl = _dot(jax.nn.gelu(ys5_ref[...].astype(F32)).astype(BF16), wglu_ref[...])
    s5 = gl[:, :D_MODEL] * _sigmoid(gl[:, D_MODEL:])
    acc = acc + _sigmoid(gates_ref[:, 2 * D_MODEL:].astype(F32)) * s5
    out_ref[...] = x_ref[...] + _dot(acc.astype(BF16), wout_ref[...])


def _merge(x, layer, tm, attn_parts, dils, length, y_ssd, y_s5, gates, wba, wbs, wglu, wout):
    m = x.shape[0]
    tps = length // tm
    row = lambda i: (i, 0)
    args = [x] + list(attn_parts) + [y_ssd, y_s5, gates, wba, wbs, wglu, wout]
    part_specs = []
    for d in dils:
        part_specs += [pl.BlockSpec((None, d, tm // d, GROUP_W), lambda i: (i // tps, 0, i % tps, 0))] * 2
    specs = ([pl.BlockSpec((tm, D_MODEL), row)] + part_specs
             + [pl.BlockSpec((tm, SSD_INNER), row), pl.BlockSpec((tm, S5_W), row),
                pl.BlockSpec((tm, 3 * D_MODEL), row)]
             + [_layer_spec(w, layer) for w in (wba, wbs, wglu, wout)])
    return pl.pallas_call(
        functools.partial(_merge_body, dils=dils),
        grid=(m // tm,),
        in_specs=specs,
        out_specs=pl.BlockSpec((tm, D_MODEL), row),
        out_shape=jax.ShapeDtypeStruct((m, D_MODEL), F32),
        scratch_shapes=[pltpu.VMEM((6 * GROUP_W // LANES, tm, LANES), F32)],
        compiler_params=_params("arbitrary"),
        name="merge",
    )(*args)


def _t5_bucket(dist):
    max_exact = N_BUCKETS // 2
    d = np.asarray(dist).astype(np.int32)
    df = np.maximum(d, 1).astype(np.float32)
    large = max_exact + (np.log(df / max_exact) / math.log(BUCKET_MAX_DIST / max_exact)
                         * (N_BUCKETS - max_exact)).astype(np.int32)
    return np.where(d < max_exact, d, np.minimum(large, N_BUCKETS - 1)).astype(np.int32)


def _prompt_bias(rel_bias, g, d):
    w = 2 * QBLK
    tab = rel_bias[:, g * HPG:(g + 1) * HPG].astype(F32)
    near = jnp.transpose(tab[_t5_bucket(np.arange(QBLK, -1, -1) * d)], (1, 0))
    row0 = jnp.concatenate([near, jnp.full((HPG, w - QBLK - 1), NEG_INF, F32)], axis=1)
    x = jnp.concatenate([row0, row0, jnp.full((HPG, 1), NEG_INF, F32)], axis=1)
    flat = jnp.tile(x, (1, QBLK))[:, :QBLK * 2 * w]
    return flat.reshape(HPG, QBLK, 2 * w)[:, :, w:]


def _attn_prompt_body(q_ref, kp_ref, kc_ref, vp_ref, vc_ref, bias_ref, o_ref, l_ref):
    first = pl.program_id(2) == 0
    pen = jnp.where(first, NEG_INF, 0.0).astype(F32)
    q = (q_ref[...] * (HEAD_DIM ** -0.5)).astype(BF16)
    kp, kc = kp_ref[...].astype(BF16), kc_ref[...].astype(BF16)
    vp, vc = vp_ref[...].astype(BF16), vc_ref[...].astype(BF16)
    for h in range(HPG):
        sl = slice(h * HEAD_DIM, (h + 1) * HEAD_DIM)
        qh = q[:, sl]
        sp = _dot_nt(qh, kp[:, sl]) + bias_ref[h, :, 0:QBLK] + pen
        sc = _dot_nt(qh, kc[:, sl]) + bias_ref[h, :, QBLK:2 * QBLK]
        m = jnp.maximum(jnp.max(sp, axis=-1, keepdims=True), jnp.max(sc, axis=-1, keepdims=True))
        ep, ec = jnp.exp(sp - m), jnp.exp(sc - m)
        den = jnp.sum(ep, axis=-1, keepdims=True) + jnp.sum(ec, axis=-1, keepdims=True)
        o = (_dot(ep.astype(BF16), vp[:, sl]) + _dot(ec.astype(BF16), vc[:, sl])) / den
        o_ref[:, sl] = o
        l_ref[:, sl] = jnp.broadcast_to(m + jnp.log(den), (QBLK, HEAD_DIM))


def _attn_prompt(qkv_g, bias, g):
    n, d, rows, _ = qkv_g.shape
    nb = rows // QBLK
    blk = (None, None, QBLK, GROUP_W)
    cur = lambda col: pl.BlockSpec(blk, lambda b, r, i: (b, r, i, col))
    prev = lambda col: pl.BlockSpec(blk, lambda b, r, i: (b, r, jnp.maximum(i - 1, 0), col))
    out_spec = pl.BlockSpec(blk, lambda b, r, i: (b, r, i, 0))
    out_sds = jax.ShapeDtypeStruct((n, d, rows, GROUP_W), F32)
    return pl.pallas_call(
        _attn_prompt_body,
        grid=(n, d, nb),
        in_specs=[cur(0), prev(1), cur(1), prev(2), cur(2), _full_spec(bias)],
        out_specs=[out_spec, out_spec],
        out_shape=[out_sds, out_sds],
        compiler_params=_params("arbitrary", "arbitrary", "arbitrary"),
        name=f"attn_prompt_g{g}",
    )(qkv_g, qkv_g, qkv_g, qkv_g, qkv_g, bias)


def _sample_bias(rel_bias, g, d):
    steps = QBLK - np.arange(QBLK)
    tab = rel_bias[:, g * HPG:(g + 1) * HPG].astype(F32)
    on_grid = jnp.transpose(tab[_t5_bucket(steps * d)], (1, 0))[:, :, None]
    off_grid = jnp.full((HPG, QBLK, d - 1), NEG_INF, F32)
    cache_b = jnp.concatenate([on_grid, off_grid], axis=2).reshape(HPG, QBLK * d)
    self_b = jnp.broadcast_to(tab[0][:, None], (HPG, LANES))
    pad = lambda a: jnp.concatenate([a, jnp.zeros((8 - HPG, a.shape[1]), F32)], 0)
    return pad(cache_b), pad(self_b)


def _attn_sample_body(q_ref, kv_ref, cb_ref, sb_ref, o_ref, l_ref, *, bs):
    base = pl.program_id(0) * bs
    rowi = lax.broadcasted_iota(jnp.int32, (8, GROUP_W), 0)
    lane = lax.broadcasted_iota(jnp.int32, (8, GROUP_W), 1)
    own = (lane // HEAD_DIM) == rowi
    p = kv_ref.shape[-1]
    for j in range(bs):
        r = pl.ds(base + j, 1)
        q = q_ref[r, 0:GROUP_W] * (HEAD_DIM ** -0.5)
        kn = q_ref[r, GROUP_W:2 * GROUP_W]
        vn = q_ref[r, 2 * GROUP_W:3 * GROUP_W]
        qblk = jnp.where(own, jnp.broadcast_to(q, (8, GROUP_W)), 0.0)
        kt = kv_ref[j, 0].reshape(GROUP_W, p).astype(BF16)
        vt = kv_ref[j, 1].reshape(GROUP_W, p).astype(BF16)
        s = _dot(qblk.astype(BF16), kt) + cb_ref[...]
        s_self = jnp.sum(qblk * kn, axis=-1, keepdims=True) + sb_ref[:, 0:1]
        m = jnp.maximum(jnp.max(s, axis=-1, keepdims=True), s_self)
        e = jnp.exp(s - m)
        e_self = jnp.exp(s_self - m)
        den = jnp.sum(e, axis=-1, keepdims=True) + e_self
        o = (_dot_nt(e.astype(BF16), vt) + e_self * vn) / den
        o_ref[r, :] = jnp.sum(jnp.where(own, o, 0.0), axis=0, keepdims=True)
        lse = m + jnp.log(den)
        l_ref[r, :] = jnp.sum(jnp.where(own, lse, 0.0), axis=0, keepdims=True)


def _attn_sample(qkv_g, cache_t, layer, biases, g, bs):
    n = qkv_g.shape[0]
    p = cache_t.shape[-1]
    cb, sb = biases
    out_sds = jax.ShapeDtypeStruct((n, GROUP_W), F32)
    return pl.pallas_call(
        functools.partial(_attn_sample_body, bs=bs),
        grid=(n // bs,),
        in_specs=[_full_spec(qkv_g),
                  pl.BlockSpec((None, bs, 2, HPG, HEAD_DIM, p), lambda s: (layer, s, 0, 0, 0, 0)),
                  _full_spec(cb), _full_spec(sb)],
        out_specs=[pl.BlockSpec((n, GROUP_W), lambda s: (0, 0))] * 2,
        out_shape=[out_sds, out_sds],
        compiler_params=_params("arbitrary"),
        name=f"attn_sample_g{g}",
    )(qkv_g, cache_t, cb, sb)


def _softplus(x):
    return jnp.maximum(x, 0.0) + jnp.log(1.0 + jnp.exp(-jnp.abs(x)))


def _split3(x):
    hi = x.astype(BF16)
    r1 = x - hi.astype(F32)
    mid = r1.astype(BF16)
    lo = (r1 - mid.astype(F32)).astype(BF16)
    return hi, mid, lo


def _ssd_prompt_body(xbc_ref, z_ref, dt_ref, cw_ref, cb_ref, dtb_ref, alog_ref, dsk_ref, ng_ref,
                     y_ref, st_ref, xs_ref, state_ref, ybuf_ref):
    c = pl.program_id(1)
    q = SSD_CHUNK

    @pl.when(c == 0)
    def _():
        state_ref[...] = jnp.zeros_like(state_ref)
        xs_ref[0:8, :] = jnp.zeros((8, SSD_CONV_CH), F32)

    xs_ref[8:8 + q, :] = xbc_ref[...]
    conv = cb_ref[...]
    for k in range(4):
        conv = conv + cw_ref[k:k + 1, :] * xs_ref[5 + k:5 + k + q, :]
    xs_ref[0:8, :] = xs_ref[q:q + 8, :]
    xc = conv * _sigmoid(conv)

    dt = _softplus(dt_ref[...] + dtb_ref[...])
    a = -jnp.exp(alog_ref[...])
    da = dt * a
    ri = lax.broadcasted_iota(jnp.int32, (q, q), 0)
    ci = lax.broadcasted_iota(jnp.int32, (q, q), 1)
    causal = ri >= ci
    tril = jnp.where(causal, 1.0, 0.0).astype(BF16)
    cum = sum(_dot(tril, part) for part in _split3(da))
    cum_t = cum.T
    dt_t = dt.T
    cum_last = cum[q - 1:q, :]
    w_end = jnp.exp(cum_last - cum) * dt
    e_cum = jnp.exp(cum)
    e_last = jnp.exp(cum_last)

    for g in range(SSD_GROUPS):
        bg = xc[:, SSD_INNER + g * SSD_S:SSD_INNER + (g + 1) * SSD_S].astype(BF16)
        cg = xc[:, SSD_INNER + (SSD_GROUPS + g) * SSD_S:SSD_INNER + (SSD_GROUPS + g + 1) * SSD_S].astype(BF16)
        gmat = _dot_nt(cg, bg)
        for hh in range(SSD_HEADS // SSD_GROUPS):
            h = g * (SSD_HEADS // SSD_GROUPS) + hh
            sl = slice(h * SSD_P, (h + 1) * SSD_P)
            diff = cum[:, h:h + 1] - cum_t[h:h + 1, :]
            decay = jnp.where(causal, jnp.exp(jnp.where(causal, diff, 0.0)), 0.0)
            scores = gmat * decay * dt_t[h:h + 1, :]
            xh = xc[:, sl]
            hin = state_ref[h]
            y = _dot(scores.astype(BF16), xh.astype(BF16))
            y = y + _dot_nt(cg, hin.astype(BF16)) * e_cum[:, h:h + 1]
            y = y + dsk_ref[:, sl] * xh
            st = _dot_tn((xh * w_end[:, h:h + 1]).astype(BF16), bg)
            state_ref[h] = e_last[:, h:h + 1] * hin + st
            ybuf_ref[:, sl] = y

    zf = z_ref[...].astype(F32)
    yg = ybuf_ref[...] * (zf * _sigmoid(zf))
    y_ref[...] = _rms(yg, ng_ref[...]).astype(BF16)
    st_ref[...] = state_ref[...]


def _ssd_prompt(xbc, z, dt, layer, n, length, cw, cb, dtb, alog, dsk, ng):
    nc = length // SSD_CHUNK
    blk = lambda w: pl.BlockSpec((None, SSD_CHUNK, w), lambda b, c: (b, c, 0))
    return pl.pallas_call(
        _ssd_prompt_body,
        grid=(n, nc),
        in_specs=[blk(SSD_CONV_CH), blk(SSD_INNER), blk(DT_PAD)]
                 + [_layer_spec(w, layer) for w in (cw, cb, dtb, alog, dsk, ng)],
        out_specs=[blk(SSD_INNER),
                   pl.BlockSpec((None, SSD_HEADS, SSD_P, SSD_S), lambda b, c: (b, 0, 0, 0))],
        out_shape=[jax.ShapeDtypeStruct((n, length, SSD_INNER), BF16),
                   jax.ShapeDtypeStruct((n, SSD_HEADS, SSD_P, SSD_S), F32)],
        scratch_shapes=[pltpu.VMEM((SSD_CHUNK + 8, SSD_CONV_CH), F32),
                        pltpu.VMEM((SSD_HEADS, SSD_P, SSD_S), F32),
                        pltpu.VMEM((SSD_CHUNK, SSD_INNER), F32)],
        compiler_params=_params("arbitrary", "arbitrary"),
        name="ssd_prompt",
    )(xbc.reshape(n, length, SSD_CONV_CH), z.reshape(n, length, SSD_INNER),
      dt.reshape(n, length, DT_PAD), cw, cb, dtb, alog, dsk, ng)


def _ssd_conv_sample_body(xbc_ref, cs_ref, dt_ref, cw_ref, cb_ref, dtb_ref, xc_ref, dts_ref):
    conv = cb_ref[...] + cw_ref[3:4, :] * xbc_ref[...]
    for k in range(3):
        conv = conv + cw_ref[k:k + 1, :] * cs_ref[:, k * SSD_CONV_CH:(k + 1) * SSD_CONV_CH]
    xc_ref[...] = conv * _sigmoid(conv)
    dts_ref[...] = _softplus(dt_ref[...] + dtb_ref[...])


def _ssd_conv_sample(xbc, conv_state, dt, layer, cw, cb, dtb):
    n = xbc.shape[0]
    cs = conv_state.reshape(DEPTH, n, 3 * SSD_CONV_CH)
    return pl.pallas_call(
        _ssd_conv_sample_body,
        grid=(1,),
        in_specs=[_full_spec(xbc), _layer_spec(cs, layer), _full_spec(dt)]
                 + [_layer_spec(w, layer) for w in (cw, cb, dtb)],
        out_specs=[pl.BlockSpec((n, SSD_CONV_CH), lambda i: (0, 0)),
                   pl.BlockSpec((n, DT_PAD), lambda i: (0, 0))],
        out_shape=[jax.ShapeDtypeStruct((n, SSD_CONV_CH), F32), jax.ShapeDtypeStruct((n, DT_PAD), F32)],
        compiler_params=_params("arbitrary"),
        name="ssd_conv_sample",
    )(xbc, cs, dt, cw, cb, dtb)


def _ssd_state_sample_body(h0_ref, x_ref, b_ref, c_ref, dt_ref, alog_ref, dsk_ref, y_ref, hn_ref):
    n = x_ref.shape[0]
    ps = SSD_P * SSD_S
    x, bv, cv = x_ref[...], b_ref[...], c_ref[...]
    dt = dt_ref[...]
    dec = jnp.exp(dt * (-jnp.exp(alog_ref[...])))
    lane = lax.broadcasted_iota(jnp.int32, (SSD_P, ps), 1)
    rowp = lax.broadcasted_iota(jnp.int32, (SSD_P, ps), 0)
    rep = jnp.where(lane // SSD_S == rowp, 1.0, 0.0).astype(BF16)
    c2 = jnp.concatenate([cv, cv], axis=1)
    b2 = jnp.concatenate([bv, bv], axis=1)
    ct = jnp.concatenate([c2] * (ps // (2 * SSD_S)), axis=1)
    bt = jnp.concatenate([b2] * (ps // (2 * SSD_S)), axis=1)
    h0 = h0_ref[...]
    y_off = _dot_nt((h0 * ct).astype(BF16), rep) * dec[:, 0:SSD_P]
    cb = jnp.sum(cv * bv, axis=-1, keepdims=True)
    y_ref[...] = cb * dt[:, 0:SSD_P] * x + y_off + dsk_ref[:, 0:SSD_P] * x
    xrep = _dot(x.astype(BF16), rep)
    dtw = jnp.concatenate([dt] * (ps // LANES), axis=1)
    decw = jnp.concatenate([dec] * (ps // LANES), axis=1)
    hn_ref[...] = decw * h0 + dtw * xrep * bt


def _ssd_state_sample(state, layer, xh, dth, alog_h, dsk_h):
    n = xh.shape[1]
    ps = SSD_P * SSD_S
    rep = SSD_HEADS // SSD_GROUPS
    piece = lambda f: pl.BlockSpec((None, n, SSD_S), lambda h: (f(h), 0, 0))
    lanes = lambda: pl.BlockSpec((None, 1, LANES), lambda h: (layer * SSD_HEADS + h, 0, 0))
    return pl.pallas_call(
        _ssd_state_sample_body,
        grid=(SSD_HEADS,),
        in_specs=[pl.BlockSpec((None, n, ps), lambda h: (layer, 0, h)),
                  piece(lambda h: h), piece(lambda h: SSD_HEADS + h // rep),
                  piece(lambda h: SSD_HEADS + SSD_GROUPS + h // rep),
                  pl.BlockSpec((None, n, LANES), lambda h: (h, 0, 0)), lanes(), lanes()],
        out_specs=[pl.BlockSpec((None, n, SSD_P), lambda h: (h, 0, 0)),
                   pl.BlockSpec((n, ps), lambda h: (0, h))],
        out_shape=[jax.ShapeDtypeStruct((SSD_HEADS, n, SSD_P), F32),
                   jax.ShapeDtypeStruct((n, SSD_HEADS * ps), F32)],
        compiler_params=_params("arbitrary"),
        name="ssd_state_sample",
    )(state, xh, xh, xh, dth, alog_h, dsk_h)


def _ssd_gate_sample_body(y_ref, z_ref, ng_ref, o_ref):
    zf = z_ref[...].astype(F32)
    o_ref[...] = _rms(y_ref[...] * (zf * _sigmoid(zf)), ng_ref[...]).astype(BF16)


def _ssd_gate_sample(y, z, layer, ng):
    n = y.shape[0]
    return pl.pallas_call(
        _ssd_gate_sample_body,
        grid=(1,),
        in_specs=[_full_spec(y), _full_spec(z), _layer_spec(ng, layer)],
        out_specs=pl.BlockSpec((n, SSD_INNER), lambda i: (0, 0)),
        out_shape=jax.ShapeDtypeStruct((n, SSD_INNER), BF16),
        compiler_params=_params("arbitrary"),
        name="ssd_gate_sample",
    )(y, z, ng)


def _cmul(a, b):
    return a[0] * b[0] - a[1] * b[1], a[0] * b[1] + a[1] * b[0]


def _s5_discretise(lr, li, dl):
    delta = jnp.exp(dl)
    mag = jnp.exp(lr * delta)
    ab = (mag * jnp.cos(li * delta), mag * jnp.sin(li * delta))
    nr, ni = ab[0] - 1.0, ab[1]
    den = lr * lr + li * li
    return ab, ((nr * lr + ni * li) / den, (ni * lr - nr * li) / den)


def _s5_powers(ab, n):
    pw = [(jnp.ones_like(ab[0]), jnp.zeros_like(ab[0]))]
    for _ in range(n):
        pw.append(_cmul(pw[-1], ab))
    return pw


def _s5_tables_body(lr_r, li_r, dl_r, btr_r, bti_r, cr_r, ci_r, lr_l, li_l, dl_l,
                    lr_t, li_t, dl_t, cr_t, ci_t, tw_ref, v_ref, v0_ref, sc_ref):
    tc, nl = S5_TC, LANES
    ab, f = _s5_discretise(lr_r[...], li_r[...], dl_r[...])
    bb = _cmul(f, (btr_r[...], bti_r[...]))
    pw = _s5_powers(ab, tc - 1)
    cc = (cr_r[...], ci_r[...])
    bbr, bbi = bb[0].astype(BF16), bb[1].astype(BF16)
    rg = lax.broadcasted_iota(jnp.int32, (nl, nl), 0) // S5_CH
    cg = lax.broadcasted_iota(jnp.int32, (nl, nl), 1) // S5_CH
    kblk = []
    for tau in range(tc):
        cp = _cmul(cc, pw[tau])
        k = _dot_nt(bbr, cp[0].astype(BF16)) - _dot_nt(bbi, cp[1].astype(BF16))
        kblk.append(jnp.where(rg == cg, k, 0.0).astype(BF16))
    zero = jnp.zeros((nl, nl), BF16)
    for j in range(tc):
        for i in range(tc):
            tw_ref[j * nl:(j + 1) * nl, i * nl:(i + 1) * nl] = kblk[i - j] if i >= j else zero
    ab_l, _ = _s5_discretise(lr_l[...], li_l[...], dl_l[...])
    pw_l = _s5_powers(ab_l, tc)
    pair = lambda x: jnp.concatenate([x, x], axis=1)
    widen = lambda x: jnp.concatenate([pair(x)] * (S5_BLK // (2 * S5_STATE)), axis=1)
    bwr, bwi = widen(bb[0]), widen(bb[1])
    wrow = lax.broadcasted_iota(jnp.int32, (nl, S5_BLK), 0) // S5_CH
    wcol = lax.broadcasted_iota(jnp.int32, (nl, S5_BLK), 1) // S5_STATE
    wmask = wrow == wcol
    for j in range(tc):
        pr, pi = pw_l[tc - 1 - j]
        tw_ref[j * nl:(j + 1) * nl, tc * nl:tc * nl + S5_BLK] = jnp.where(wmask, bwr * pr - bwi * pi, 0.0).astype(BF16)
        tw_ref[j * nl:(j + 1) * nl, tc * nl + S5_BLK:] = jnp.where(wmask, bwr * pi + bwi * pr, 0.0).astype(BF16)
    a8 = pw_l[tc]
    a16 = _cmul(a8, a8)
    a32 = _cmul(a16, a16)
    rows = [ab_l, a16, a32, a8]
    for _ in range(tc - 1):
        rows.append(_cmul(rows[-1], a8))
    sc_ref[...] = jnp.zeros_like(sc_ref)
    for k, rw in enumerate(rows):
        sc_ref[0, k:k + 1, :] = rw[0]
        sc_ref[1, k:k + 1, :] = rw[1]
    ab_t, _ = _s5_discretise(lr_t[...], li_t[...], dl_t[...])
    pw_t = _s5_powers(ab_t, tc)
    cc_t = (cr_t[...], ci_t[...])
    vrow = lax.broadcasted_iota(jnp.int32, (S5_BLK, nl), 0) // S5_STATE
    vcol = lax.broadcasted_iota(jnp.int32, (S5_BLK, nl), 1) // S5_CH
    vmask = vrow == vcol
    tall = lambda x: jnp.concatenate([x] * (S5_BLK // S5_STATE), axis=0)
    for tau in range(tc + 1):
        cp = _cmul(cc_t, pw_t[tau])
        vr = jnp.where(vmask, tall(cp[0]), 0.0).astype(BF16)
        vi = jnp.where(vmask, tall(-cp[1]), 0.0).astype(BF16)
        if tau == 0:
            v0_ref[0:S5_BLK, :] = vr
            v0_ref[S5_BLK:, :] = vi
        else:
            v_ref[0:S5_BLK, (tau - 1) * nl:tau * nl] = vr
            v_ref[S5_BLK:, (tau - 1) * nl:tau * nl] = vi


def _s5_tables(a_re, a_im, log_dt, b_re, b_im, c_re, c_im, d_skip):
    g, s, c, gb = S5_GROUPS, S5_STATE, S5_CH, S5_GB
    ldt = jnp.broadcast_to(log_dt[:, :, None], (DEPTH, g, s))
    rows = lambda a: jnp.repeat(a, c, axis=1)
    flat = lambda a: a.reshape(DEPTH, gb, 1, S5_BLK)
    cols = lambda a: jnp.repeat(jnp.swapaxes(a, 1, 2), c, axis=2)
    by_rows = [rows(a_re), rows(a_im), rows(ldt),
               jnp.swapaxes(b_re, 2, 3).reshape(DEPTH, g * c, s), jnp.swapaxes(b_im, 2, 3).reshape(DEPTH, g * c, s),
               c_re.reshape(DEPTH, g * c, s), c_im.reshape(DEPTH, g * c, s)]
    by_lane = [flat(a_re), flat(a_im), flat(ldt)]
    by_col = [cols(a_re), cols(a_im), cols(ldt),
              jnp.transpose(c_re, (0, 3, 1, 2)).reshape(DEPTH, s, g * c),
              jnp.transpose(c_im, (0, 3, 1, 2)).reshape(DEPTH, s, g * c)]
    in_specs = ([pl.BlockSpec((None, LANES, s), lambda l, b: (l, b, 0))] * len(by_rows)
                + [pl.BlockSpec((None, None, 1, S5_BLK), lambda l, b: (l, b, 0, 0))] * len(by_lane)
                + [pl.BlockSpec((None, s, LANES), lambda l, b: (l, 0, b))] * len(by_col))
    blk = lambda *shape: pl.BlockSpec((None, None) + shape, lambda l, b: (l, b) + (0,) * len(shape))
    tcw = S5_TC * LANES
    tw, v_blk, v0_blk, scan = pl.pallas_call(
        _s5_tables_body,
        grid=(DEPTH, gb),
        in_specs=in_specs,
        out_specs=[blk(tcw, tcw + 2 * S5_BLK), blk(2 * S5_BLK, tcw), blk(2 * S5_BLK, LANES), blk(2, 16, S5_BLK)],
        out_shape=[jax.ShapeDtypeStruct((DEPTH, gb, tcw, tcw + 2 * S5_BLK), BF16),
                   jax.ShapeDtypeStruct((DEPTH, gb, 2 * S5_BLK, tcw), BF16),
                   jax.ShapeDtypeStruct((DEPTH, gb, 2 * S5_BLK, LANES), BF16),
                   jax.ShapeDtypeStruct((DEPTH, gb, 2, 16, S5_BLK), F32)],
        compiler_params=_params("arbitrary", "arbitrary"),
        name="s5_tables",
    )(*by_rows, *by_lane, *by_col)
    return tw, v_blk, v0_blk, scan, d_skip.reshape(DEPTH, gb, 1, LANES)


def _s5_prompt_body(*refs):
    u_refs = refs[0:S5_TC]
    tw_ref, v_ref, sc_ref, d_ref, y_ref, hout_ref, sre_ref, sim_ref, hin_ref, cr_ref, ci_ref = refs[S5_TC:]
    r = pl.program_id(2)
    tr = sre_ref.shape[0]

    @pl.when(r == 0)
    def _():
        cr_ref[...] = jnp.zeros_like(cr_ref)
        ci_ref[...] = jnp.zeros_like(ci_ref)

    u = jnp.concatenate([ur[...] for ur in u_refs], axis=1)
    tw = _dot(u, tw_ref[...])
    yd = tw[:, 0:S5_TC * 128]
    sre_ref[...] = tw[:, S5_TC * 128:S5_TC * 128 + S5_BLK]
    sim_ref[...] = tw[:, S5_TC * 128 + S5_BLK:]

    a8 = (sc_ref[0, 3:4, :], sc_ref[1, 3:4, :])
    a16 = (sc_ref[0, 1:2, :], sc_ref[1, 1:2, :])
    a32 = (sc_ref[0, 2:3, :], sc_ref[1, 2:3, :])
    apw = (sc_ref[0, 3:11, :], sc_ref[1, 3:11, :])
    row = lax.broadcasted_iota(jnp.int32, (8, S5_BLK), 0)

    def shift(x, k):
        return jnp.where(row >= k, pltpu.roll(x, k, axis=0), 0.0)

    def tile(t, carry):
        c_re, c_im = carry
        rows = pl.ds(pl.multiple_of(t * 8, 8), 8)
        xr, xi = sre_ref[rows, :], sim_ref[rows, :]
        for k, (ar, ai) in ((1, a8), (2, a16), (4, a32)):
            pr, pi = shift(xr, k), shift(xi, k)
            xr, xi = xr + ar * pr - ai * pi, xi + ar * pi + ai * pr
        xr = xr + apw[0] * c_re - apw[1] * c_im
        xi = xi + apw[0] * c_im + apw[1] * c_re
        hin_ref[rows, 0:S5_BLK] = jnp.where(row >= 1, pltpu.roll(xr, 1, axis=0), c_re)
        hin_ref[rows, S5_BLK:] = jnp.where(row >= 1, pltpu.roll(xi, 1, axis=0), c_im)
        return xr[7:8, :], xi[7:8, :]

    c_re, c_im = lax.fori_loop(0, tr // 8, tile, (cr_ref[...], ci_ref[...]))
    cr_ref[...] = c_re
    ci_ref[...] = c_im
    hout_ref[:, 0:S5_BLK] = c_re
    hout_ref[:, S5_BLK:] = c_im

    d_row = jnp.concatenate([d_ref[...]] * S5_TC, axis=1)
    y = yd + _dot(hin_ref[...].astype(BF16), v_ref[...]) + d_row * u.astype(F32)
    for j in range(S5_TC):
        y_ref[j] = y[:, j * 128:(j + 1) * 128].astype(BF16)


def _s5_prompt(u, layer, n, length, tw, v_blk, scan, d_blk, tr=512):
    rows = length // S5_TC
    tr = min(tr, rows)
    nr = rows // tr
    u8 = u.reshape(n * rows, S5_TC * S5_W)
    u_spec = lambda j: pl.BlockSpec((tr, 128), lambda gb, b, r: (b * nr + r, j * S5_GB + gb))
    lay = lambda a: pl.BlockSpec((None, None) + tuple(a.shape[2:]),
                                 lambda gb, b, r: (layer, gb) + (0,) * (a.ndim - 2))
    y8, hout = pl.pallas_call(
        _s5_prompt_body,
        grid=(S5_GB, n, nr),
        in_specs=[u_spec(j) for j in range(S5_TC)] + [lay(tw), lay(v_blk), lay(scan), lay(d_blk)],
        out_specs=[pl.BlockSpec((S5_TC, tr, 128), lambda gb, b, r: (0, b * nr + r, gb)),
                   pl.BlockSpec((None, None, 1, 2 * S5_BLK), lambda gb, b, r: (b, gb, 0, 0))],
        out_shape=[jax.ShapeDtypeStruct((S5_TC, n * rows, S5_W), BF16),
                   jax.ShapeDtypeStruct((n, S5_GB, 1, 2 * S5_BLK), F32)],
        scratch_shapes=[pltpu.VMEM((tr, S5_BLK), F32), pltpu.VMEM((tr, S5_BLK), F32),
                        pltpu.VMEM((tr, 2 * S5_BLK), F32),
                        pltpu.VMEM((1, S5_BLK), F32), pltpu.VMEM((1, S5_BLK), F32)],
        compiler_params=_params("arbitrary", "arbitrary", "arbitrary"),
        name="s5_prompt",
    )(*([u8] * S5_TC), tw, v_blk, scan, d_blk)
    y = jnp.transpose(y8, (1, 0, 2)).reshape(n * length, S5_W)
    state = jnp.transpose(hout.reshape(n, S5_GB, 2, 8, S5_STATE), (0, 2, 1, 3, 4))
    return y, state.reshape(n, 2, S5_GROUPS, S5_STATE)


def _s5_sample_body(u_ref, hr_ref, hi_ref, w_ref, v0_ref, sc_ref, d_ref, y_ref, nr_ref, ni_ref):
    u = u_ref[...]
    bu = _dot(u, w_ref[...])
    ar, ai = sc_ref[0, 0:1, :], sc_ref[1, 0:1, :]
    hr, hi = hr_ref[...], hi_ref[...]
    nr = ar * hr - ai * hi + bu[:, 0:S5_BLK]
    ni = ar * hi + ai * hr + bu[:, S5_BLK:]
    nr_ref[...] = nr
    ni_ref[...] = ni
    hcat = jnp.concatenate([nr, ni], axis=1).astype(BF16)
    y_ref[...] = (_dot(hcat, v0_ref[...]) + d_ref[...] * u.astype(F32)).astype(BF16)


def _s5_sample(u, state, layer, tw, v0_blk, scan, d_blk):
    n = u.shape[0]
    lay = lambda a: pl.BlockSpec((None, None) + tuple(a.shape[2:]),
                                 lambda gb: (layer, gb) + (0,) * (a.ndim - 2))
    st = lambda off: pl.BlockSpec((None, n, S5_BLK), lambda gb: (layer, 0, off + gb))
    half = S5_GROUPS * S5_STATE
    y, nr, ni = pl.pallas_call(
        _s5_sample_body,
        grid=(S5_GB,),
        in_specs=[pl.BlockSpec((n, 128), lambda gb: (0, gb)), st(0), st(S5_GB),
                  pl.BlockSpec((None, None, 128, 2 * S5_BLK), lambda gb: (layer, gb, S5_TC - 1, 1)),
                  lay(v0_blk), lay(scan), lay(d_blk)],
        out_specs=[pl.BlockSpec((n, 128), lambda gb: (0, gb)),
                   pl.BlockSpec((n, S5_BLK), lambda gb: (0, gb)),
                   pl.BlockSpec((n, S5_BLK), lambda gb: (0, gb))],
        out_shape=[jax.ShapeDtypeStruct((n, S5_W), BF16), jax.ShapeDtypeStruct((n, half), F32),
                   jax.ShapeDtypeStruct((n, half), F32)],
        compiler_params=_params("arbitrary"),
        name="s5_sample",
    )(u, state, state, tw, v0_blk, scan, d_blk)
    return y, jnp.stack([nr, ni], axis=1).reshape(n, 2, S5_GROUPS, S5_STATE)


def kernel(x_prompt, x_sample, cache_kv_w128, cache_kv_w512, cache_kv_w2048, state_ssd, state_conv, state_s5, p_prompt, p_sample, attn_rel_bias, ffn1_norm, ffn1_w_gate, ffn1_w_up, ffn1_w_down, mix_norm, w_in, ssd_conv_w, ssd_conv_b, ssd_dt_bias, ssd_a_log, ssd_d, ssd_norm, s5_a_re, s5_a_im, s5_b_re, s5_b_im, s5_c_re, s5_c_im, s5_d, s5_log_dt, w_s5_glu, w_branch_attn, w_branch_ssd, w_out, ffn2_norm, ffn2_w_gate, ffn2_w_up, ffn2_w_down, pe_norm, w_pe_gate, w_pe_proj, final_norm):
    n_p, len_p, _ = x_prompt.shape
    n_s = x_sample.shape[0]
    m_p = n_p * len_p
    assert x_sample.shape[1] == 1 and len_p % (QBLK * ATTN_PATTERNS[-1][1]) == 0
    caches = (cache_kv_w128, cache_kv_w512, cache_kv_w2048)
    dils = tuple(d for _, d in ATTN_PATTERNS)
    for cache, (w, d) in zip(caches, ATTN_PATTERNS):
        assert cache.shape[3] == w == QBLK * d
    caches_t = [jnp.transpose(c, (0, 1, 2, 4, 5, 3)) for c in caches]
    sample_bs = (16, 8, 2)

    bf = lambda a: a.astype(BF16)
    vec = lambda a: a.reshape(DEPTH, 1, a.shape[-1])
    w1g, w1u, w1d = bf(ffn1_w_gate), bf(ffn1_w_up), bf(ffn1_w_down)
    w2g, w2u, w2d = bf(ffn2_w_gate), bf(ffn2_w_up), bf(ffn2_w_down)
    dt_lo = OFF_XBC + SSD_CONV_CH
    n_dt = SSD_HEADS
    w_qkv = jnp.swapaxes(w_in[..., :OFF_Z].reshape(DEPTH, D_MODEL, 3, 3, GROUP_W), 2, 3)
    w_in_p = bf(jnp.concatenate(
        [w_qkv.reshape(DEPTH, D_MODEL, OFF_Z), w_in[..., OFF_Z:dt_lo + n_dt],
         jnp.zeros((DEPTH, D_MODEL, DT_PAD - n_dt), w_in.dtype), w_in[..., dt_lo + n_dt:]], axis=-1))
    wba, wbs, wglu, wout = bf(w_branch_attn), bf(w_branch_ssd), bf(w_s5_glu), bf(w_out)
    wpg, wpp = bf(w_pe_gate), bf(w_pe_proj)
    n1, nm, n2, npe = vec(ffn1_norm), vec(mix_norm), vec(ffn2_norm), vec(pe_norm)
    fin = final_norm.reshape(1, D_MODEL)
    cb = vec(ssd_conv_b)
    lane_pad = lambda a: jnp.pad(a, ((0, 0), (0, DT_PAD - a.shape[-1]))).reshape(DEPTH, 1, DT_PAD)
    dtb, alog = lane_pad(ssd_dt_bias), lane_pad(ssd_a_log)
    dsk = jnp.repeat(ssd_d, SSD_P, axis=-1).reshape(DEPTH, 1, SSD_INNER)
    ng = vec(ssd_norm)
    per_head = lambda a: jnp.broadcast_to(a.reshape(DEPTH * SSD_HEADS, 1, 1), (DEPTH * SSD_HEADS, 1, LANES))
    alog_h, dsk_h = per_head(ssd_a_log), per_head(ssd_d)

    tw, v_blk, v0_blk, scan, d_blk = _s5_tables(s5_a_re, s5_a_im, s5_log_dt, s5_b_re, s5_b_im,
                                                s5_c_re, s5_c_im, s5_d)

    bias_p = [_prompt_bias(attn_rel_bias, g, d) for g, (_, d) in enumerate(ATTN_PATTERNS)]
    bias_s = [_sample_bias(attn_rel_bias, g, d) for g, (_, d) in enumerate(ATTN_PATTERNS)]

    pp = p_prompt.reshape(DEPTH, m_p, PE_DIM)
    psm = p_sample.reshape(DEPTH, n_s, PE_DIM)
    st_ssd = state_ssd.reshape(DEPTH, n_s, SSD_HEADS * SSD_P * SSD_S)
    st_s5 = state_s5.reshape(DEPTH, n_s, 2 * S5_GROUPS * S5_STATE)

    xp = x_prompt.reshape(m_p, D_MODEL)
    xs = x_sample.reshape(n_s, D_MODEL)
    tm_p, tm_s = 256, n_s
    kv_p, kv_s = [[], [], []], [[], [], []]
    ssd_p, ssd_s, conv_p, conv_s, s5_p, s5_s = [], [], [], [], [], []

    def kv_prompt(qg, d):
        n, _, rows, _ = qg.shape
        t = qg[:, :, rows - QBLK:, GROUP_W:].reshape(n, d, QBLK, 2, HPG, HEAD_DIM)
        return jnp.transpose(t, (0, 3, 2, 1, 4, 5)).reshape(n, 2, QBLK * d, HPG, HEAD_DIM)

    for i in range(DEPTH):
        last = i == DEPTH - 1
        xp = _ffn(xp, i, tm_p, n1, w1g, w1u, w1d)
        *qgs, z, xbc, dt, u, gates = _inproj(xp, i, tm_p, nm, w_in_p, n_p, len_p, dils)
        parts = []
        for g, d in enumerate(dils):
            parts += list(_attn_prompt(qgs[g], bias_p[g], g))
            kv_p[g].append(kv_prompt(qgs[g], d))
        y_ssd, st = _ssd_prompt(xbc, z, dt, i, n_p, len_p, ssd_conv_w, cb, dtb, alog, dsk, ng)
        ssd_p.append(st)
        conv_p.append(xbc.reshape(n_p, len_p, SSD_CONV_CH)[:, len_p - 3:])
        y_s5, st5 = _s5_prompt(u, i, n_p, len_p, tw, v_blk, scan, d_blk)
        s5_p.append(st5)
        xp = _merge(xp, i, tm_p, parts, dils, len_p, y_ssd.reshape(m_p, SSD_INNER), y_s5, gates,
                    wba, wbs, wglu, wout)
        xp = _ffn(xp, i, tm_p, n2, w2g, w2u, w2d, pe=(pp, npe, wpg, wpp), final=fin if last else None)
        xs = _ffn(xs, i, tm_s, n1, w1g, w1u, w1d)
        *qgs, z, xbc, dt, u, gates = _inproj(xs, i, tm_s, nm, w_in_p, 1, n_s, (1, 1, 1))
        parts = []
        for g in range(len(dils)):
            qg = qgs[g].reshape(n_s, ATTN_W)
            o_s, l_s = _attn_sample(qg, caches_t[g], i, bias_s[g], g, sample_bs[g])
            parts += [o_s.reshape(1, 1, n_s, GROUP_W), l_s.reshape(1, 1, n_s, GROUP_W)]
            kv_s[g].append(qg[:, GROUP_W:].reshape(n_s, 2, 1, HPG, HEAD_DIM))
        xc, dts = _ssd_conv_sample(xbc, state_conv, dt, i, ssd_conv_w, cb, dtb)
        conv_s.append(jnp.concatenate([state_conv[i][:, 1:], xbc[:, None, :]], axis=1))
        xh = jnp.transpose(xc.reshape(n_s, SSD_CONV_CH // SSD_S, SSD_S), (1, 0, 2))
        dth = jnp.broadcast_to(jnp.transpose(dts[:, :SSD_HEADS])[:, :, None], (SSD_HEADS, n_s, LANES))
        yh, hn = _ssd_state_sample(st_ssd, i, xh, dth, alog_h, dsk_h)
        ssd_s.append(hn.reshape(n_s, SSD_HEADS, SSD_P, SSD_S))
        y_ssd = _ssd_gate_sample(jnp.transpose(yh, (1, 0, 2)).reshape(n_s, SSD_INNER), z, i, ng)
        y_s5, st5 = _s5_sample(u, st_s5, i, tw, v0_blk, scan, d_blk)
        s5_s.append(st5)
        xs = _merge(xs, i, tm_s, parts, (1, 1, 1), n_s, y_ssd, y_s5, gates, wba, wbs, wglu, wout)
        xs = _ffn(xs, i, tm_s, n2, w2g, w2u, w2d, pe=(psm, npe, wpg, wpp), final=fin if last else None)

    return (xp.reshape(n_p, len_p, D_MODEL), xs.reshape(n_s, 1, D_MODEL),
            jnp.stack(kv_p[0], 0), jnp.stack(kv_s[0], 0),
            jnp.stack(kv_p[1], 0), jnp.stack(kv_s[1], 0),
            jnp.stack(kv_p[2], 0), jnp.stack(kv_s[2], 0),
            jnp.stack(ssd_p, 0), jnp.stack(ssd_s, 0),
            jnp.stack(conv_p, 0), jnp.stack(conv_s, 0),
            jnp.stack(s5_p, 0), jnp.stack(s5_s, 0))
```

```python
import functools
import math

import numpy as np
import jax
import jax.numpy as jnp
from jax import lax
from jax.experimental import pallas as pl
from jax.experimental.pallas import tpu as pltpu

F32 = jnp.float32
BF16 = jnp.bfloat16

D_MODEL = 1024
DEPTH = 4
HEAD_DIM = 64
ATTN_PATTERNS = ((128, 1), (512, 4), (2048, 16))
HPG = 4
GROUP_W = HPG * HEAD_DIM
ATTN_W = 3 * GROUP_W
QBLK = 128
N_BUCKETS = 32
BUCKET_MAX_DIST = 2048
NEG_INF = -1e30
SSD_HEADS = 12
SSD_P = 64
SSD_S = 64
SSD_GROUPS = 4
SSD_INNER = SSD_HEADS * SSD_P
SSD_CONV_CH = SSD_INNER + 2 * SSD_GROUPS * SSD_S
SSD_CHUNK = 128
S5_GROUPS = 48
S5_CH = 16
S5_STATE = 64
S5_W = S5_GROUPS * S5_CH
S5_TC = 8
S5_GB = 6
S5_BLK = 8 * S5_STATE
D_FF = 2816
PE_DIM = 256
RMS_EPS = 1e-6
LANES = 128
DT_PAD = LANES
OFF_Q, OFF_K, OFF_V = 0, ATTN_W, 2 * ATTN_W
OFF_Z = 3 * ATTN_W
OFF_XBC = OFF_Z + SSD_INNER
OFF_DT = OFF_XBC + SSD_CONV_CH
OFF_U = OFF_DT + DT_PAD
OFF_G = OFF_U + S5_W
IN_COLS_PACKED = OFF_G + 3 * D_MODEL
VMEM_LIMIT = 56 * 1024 * 1024


def _params(*sem):
    return pltpu.CompilerParams(dimension_semantics=sem, vmem_limit_bytes=VMEM_LIMIT)


def _rms(x, g):
    inv = lax.rsqrt(jnp.mean(x * x, axis=-1, keepdims=True) + RMS_EPS)
    return (x * inv) * g


def _dot(a, b):
    return jnp.dot(a, b, preferred_element_type=F32)


def _dot_nt(a, b):
    return lax.dot_general(a, b, (((1,), (1,)), ((), ())), preferred_element_type=F32)


def _dot_tn(a, b):
    return lax.dot_general(a, b, (((0,), (0,)), ((), ())), preferred_element_type=F32)


def _sigmoid(x):
    return jax.nn.sigmoid(x)


def _layer_spec(arr, layer):
    nd = arr.ndim - 1
    return pl.BlockSpec((None,) + tuple(arr.shape[1:]),
                        lambda *_: (layer,) + (0,) * nd,
                        pipeline_mode=pl.Buffered(1))


def _full_spec(arr):
    nd = arr.ndim
    return pl.BlockSpec(tuple(arr.shape), lambda *_: (0,) * nd, pipeline_mode=pl.Buffered(1))


def _ffn_body(*refs, has_pe, has_final):
    it = iter(refs)
    x_ref, g_ref, wg_ref, wu_ref, wd_ref = (next(it) for _ in range(5))
    if has_pe:
        p_ref, pn_ref, wpg_ref, wpp_ref = (next(it) for _ in range(4))
    if has_final:
        fn_ref = next(it)
    o_ref = next(it)
    act_ref = next(it)
    x = x_ref[...]
    h = _rms(x, g_ref[...]).astype(BF16)
    half = D_FF // 2
    for f0 in (0, half):
        gate = _dot(h, wg_ref[:, f0:f0 + half])
        up = _dot(h, wu_ref[:, f0:f0 + half])
        act_ref[:, f0:f0 + half] = (gate * _sigmoid(gate) * up).astype(BF16)
    y = x + 0.5 * _dot(act_ref[...], wd_ref[...])
    if has_pe:
        h2 = _rms(y, pn_ref[...]).astype(BF16)
        gt = _sigmoid(_dot(h2, wpg_ref[...]))
        y = y + gt * _dot(p_ref[...].astype(BF16), wpp_ref[...])
    if has_final:
        y = _rms(y, fn_ref[...])
    o_ref[...] = y


def _ffn(x, layer, tm, norm, wg, wu, wd, pe=None, final=None):
    m = x.shape[0]
    row = lambda i: (i, 0)
    args = [x, norm, wg, wu, wd]
    specs = [pl.BlockSpec((tm, D_MODEL), row), _layer_spec(norm, layer), _layer_spec(wg, layer),
             _layer_spec(wu, layer), _layer_spec(wd, layer)]
    if pe is not None:
        p_all, pn, wpg, wpp = pe
        args += [p_all, pn, wpg, wpp]
        specs += [pl.BlockSpec((None, tm, PE_DIM), lambda i: (layer, i, 0)), _layer_spec(pn, layer),
                  _layer_spec(wpg, layer), _layer_spec(wpp, layer)]
    if final is not None:
        args.append(final)
        specs.append(_full_spec(final))
    return pl.pallas_call(
        functools.partial(_ffn_body, has_pe=pe is not None, has_final=final is not None),
        grid=(m // tm,),
        in_specs=specs,
        out_specs=pl.BlockSpec((tm, D_MODEL), row),
        out_shape=jax.ShapeDtypeStruct((m, D_MODEL), F32),
        scratch_shapes=[pltpu.VMEM((tm, D_FF), BF16)],
        compiler_params=_params("arbitrary"),
        name="ffn",
    )(*args)


def _inproj_body(x_ref, g_ref, w_ref, qa_ref, qb_ref, qc_ref, z_ref, xbc_ref, dt_ref, u_ref, gates_ref,
                 stage_ref, *, dils, prompt):
    h = _rms(x_ref[...], g_ref[...]).astype(BF16)
    tm = x_ref.shape[0]
    for g, (out_ref, d) in enumerate(zip((qa_ref, qb_ref, qc_ref), dils)):
        res = _dot(h, w_ref[:, g * ATTN_W:(g + 1) * ATTN_W])
        if d == 1:
            out_ref[0] = res
        else:
            for c in range(ATTN_W // LANES):
                stage_ref[c] = res[:, c * LANES:(c + 1) * LANES]
            for r in range(d):
                for c in range(ATTN_W // LANES):
                    out_ref[r, :, c * LANES:(c + 1) * LANES] = stage_ref[c, pl.ds(r, tm // d, stride=d), :]
    z_ref[...] = _dot(h, w_ref[:, OFF_Z:OFF_XBC]).astype(BF16)
    xbc_ref[...] = _dot(h, w_ref[:, OFF_XBC:OFF_DT])
    dt_ref[...] = _dot(h, w_ref[:, OFF_DT:OFF_U])
    u = _dot(h, w_ref[:, OFF_U:OFF_G])
    if prompt:
        for c in range(S5_W // LANES):
            stage_ref[c] = u[:, c * LANES:(c + 1) * LANES]
        for j in range(S5_TC):
            for c in range(S5_W // LANES):
                u_ref[j, :, c * LANES:(c + 1) * LANES] = (
                    stage_ref[c, pl.ds(j, tm // S5_TC, stride=S5_TC), :].astype(BF16))
    else:
        u_ref[...] = u.astype(BF16)
    for k in range(3):
        c0 = OFF_G + k * D_MODEL
        gates_ref[:, k * D_MODEL:(k + 1) * D_MODEL] = _dot(h, w_ref[:, c0:c0 + D_MODEL]).astype(BF16)


def _inproj(x, layer, tm, norm, w, n, length, dils, prompt):
    m = x.shape[0]
    tps = length // tm
    row = lambda i: (i, 0)
    q_specs = [pl.BlockSpec((None, d, tm // d, ATTN_W), lambda i: (i // tps, 0, i % tps, 0)) for d in dils]
    q_shapes = [jax.ShapeDtypeStruct((n, d, length // d, ATTN_W), F32) for d in dils]
    flat = lambda wd, dt: (pl.BlockSpec((tm, wd), row), jax.ShapeDtypeStruct((m, wd), dt))
    if prompt:
        u_out = (pl.BlockSpec((S5_TC, tm // S5_TC, S5_W), lambda i: (0, i, 0)),
                 jax.ShapeDtypeStruct((S5_TC, m // S5_TC, S5_W), BF16))
    else:
        u_out = flat(S5_W, BF16)
    rest = [flat(SSD_INNER, BF16), flat(SSD_CONV_CH, F32), flat(DT_PAD, F32), u_out, flat(3 * D_MODEL, BF16)]
    return pl.pallas_call(
        functools.partial(_inproj_body, dils=dils, prompt=prompt),
        grid=(m // tm,),
        in_specs=[pl.BlockSpec((tm, D_MODEL), row), _layer_spec(norm, layer), _layer_spec(w, layer)],
        out_specs=q_specs + [s for s, _ in rest],
        out_shape=q_shapes + [s for _, s in rest],
        scratch_shapes=[pltpu.VMEM((ATTN_W // LANES, tm, LANES), F32)],
        compiler_params=_params("arbitrary"),
        name="inproj",
    )(x, norm, w)


def _merge_body(x_ref, o0, l0, o1, l1, o2, l2, yssd_ref, ys5_ref, gates_ref,
                wba_ref, wbs_ref, wglu_ref, wout_ref, out_ref, stage_ref, s5_stage_ref, *, dils, s5_chunked):
    tm = x_ref.shape[0]
    parts = []
    for k, (ref, d) in enumerate(zip((o0, l0, o1, l1, o2, l2), (dils[0], dils[0], dils[1], dils[1], dils[2], dils[2]))):
        if d == 1:
            parts.append(ref[0])
        else:
            nl = GROUP_W // LANES
            for r in range(d):
                for c in range(nl):
                    stage_ref[k * nl + c, pl.ds(r, tm // d, stride=d), :] = ref[r, :, c * LANES:(c + 1) * LANES]
            parts.append(jnp.concatenate([stage_ref[k * nl + c] for c in range(nl)], axis=1))
    oa, la, ob, lb, oc, lc = parts
    mx = jnp.maximum(jnp.maximum(la, lb), lc)
    ea, eb, ec = jnp.exp(la - mx), jnp.exp(lb - mx), jnp.exp(lc - mx)
    attn = (ea * oa + eb * ob + ec * oc) / (ea + eb + ec)
    acc = _sigmoid(gates_ref[:, 0:D_MODEL].astype(F32)) * _dot(attn.astype(BF16), wba_ref[...])
    acc = acc + _sigmoid(gates_ref[:, D_MODEL:2 * D_MODEL].astype(F32)) * _dot(yssd_ref[...], wbs_ref[...])
    if s5_chunked:
        nl6 = S5_W // LANES
        for j in range(S5_TC):
            for c in range(nl6):
                s5_stage_ref[c, pl.ds(j, tm // S5_TC, stride=S5_TC), :] = (
                    ys5_ref[j, :, c * LANES:(c + 1) * LANES].astype(F32))
        ys5 = jnp.concatenate([s5_stage_ref[c] for c in range(nl6)], axis=1)
    else:
        ys5 = ys5_ref[...].astype(F32)
    gl = _dot(jax.nn.gelu(ys5).astype(BF16), wglu_ref[...])
    s5 = gl[:, :D_MODEL] * _sigmoid(gl[:, D_MODEL:])
    acc = acc + _sigmoid(gates_ref[:, 2 * D_MODEL:].astype(F32)) * s5
    out_ref[...] = x_ref[...] + _dot(acc.astype(BF16), wout_ref[...])


def _merge(x, layer, tm, attn_parts, dils, length, y_ssd, y_s5, gates, wba, wbs, wglu, wout):
    m = x.shape[0]
    tps = length // tm
    row = lambda i: (i, 0)
    s5_chunked = y_s5.ndim == 3
    args = [x] + list(attn_parts) + [y_ssd, y_s5, gates, wba, wbs, wglu, wout]
    part_specs = []
    for d in dils:
        part_specs += [pl.BlockSpec((None, d, tm // d, GROUP_W), lambda i: (i // tps, 0, i % tps, 0))] * 2
    s5_spec = (pl.BlockSpec((S5_TC, tm // S5_TC, S5_W), lambda i: (0, i, 0)) if s5_chunked
               else pl.BlockSpec((tm, S5_W), row))
    specs = ([pl.BlockSpec((tm, D_MODEL), row)] + part_specs
             + [pl.BlockSpec((tm, SSD_INNER), row), s5_spec, pl.BlockSpec((tm, 3 * D_MODEL), row)]
             + [_layer_spec(w, layer) for w in (wba, wbs, wglu, wout)])
    return pl.pallas_call(
        functools.partial(_merge_body, dils=dils, s5_chunked=s5_chunked),
        grid=(m // tm,),
        in_specs=specs,
        out_specs=pl.BlockSpec((tm, D_MODEL), row),
        out_shape=jax.ShapeDtypeStruct((m, D_MODEL), F32),
        scratch_shapes=[pltpu.VMEM((6 * GROUP_W // LANES, tm, LANES), F32),
                        pltpu.VMEM((S5_W // LANES, tm, LANES), F32)],
        compiler_params=_params("arbitrary"),
        name="merge",
    )(*args)


def _t5_bucket(dist):
    max_exact = N_BUCKETS // 2
    d = np.asarray(dist).astype(np.int32)
    df = np.maximum(d, 1).astype(np.float32)
    large = max_exact + (np.log(df / max_exact) / math.log(BUCKET_MAX_DIST / max_exact)
                         * (N_BUCKETS - max_exact)).astype(np.int32)
    return np.where(d < max_exact, d, np.minimum(large, N_BUCKETS - 1)).astype(np.int32)


def _prompt_bias(rel_bias, g, d):
    w = 2 * QBLK
    tab = rel_bias[:, g * HPG:(g + 1) * HPG].astype(F32)
    near = jnp.transpose(tab[_t5_bucket(np.arange(QBLK, -1, -1) * d)], (1, 0))
    row0 = jnp.concatenate([near, jnp.full((HPG, w - QBLK - 1), NEG_INF, F32)], axis=1)
    x = jnp.concatenate([row0, row0, jnp.full((HPG, 1), NEG_INF, F32)], axis=1)
    flat = jnp.tile(x, (1, QBLK))[:, :QBLK * 2 * w]
    return flat.reshape(HPG, QBLK, 2 * w)[:, :, w:]


def _attn_prompt_body(q_ref, kp_ref, kc_ref, vp_ref, vc_ref, bias_ref, o_ref, l_ref):
    nsub = q_ref.shape[0] // QBLK
    first = pl.program_id(2) == 0
    lane = lax.broadcasted_iota(jnp.int32, (1, 2 * QBLK), 1)
    pen = jnp.where(jnp.logical_and(first, lane < QBLK), NEG_INF, 0.0).astype(F32)
    q = (q_ref[...] * (HEAD_DIM ** -0.5)).astype(BF16)
    k = jnp.concatenate([kp_ref[...], kc_ref[...]], axis=0).astype(BF16)
    v = jnp.concatenate([vp_ref[...], vc_ref[...]], axis=0).astype(BF16)
    for s in range(nsub):
        rq = slice(s * QBLK, (s + 1) * QBLK)
        rk = slice(s * QBLK, (s + 2) * QBLK)
        for h in range(HPG):
            sl = slice(h * HEAD_DIM, (h + 1) * HEAD_DIM)
            sc = _dot_nt(q[rq, sl], k[rk, sl]) + bias_ref[h]
            if s == 0:
                sc = sc + pen
            m = jnp.max(sc, axis=-1, keepdims=True)
            e = jnp.exp(sc - m)
            den = jnp.sum(e, axis=-1, keepdims=True)
            o_ref[rq, sl] = _dot(e.astype(BF16), v[rk, sl]) / den
            l_ref[rq, sl] = jnp.broadcast_to(m + jnp.log(den), (QBLK, HEAD_DIM))


def _attn_prompt(qkv_g, bias, g, tq=512):
    n, d, rows, _ = qkv_g.shape
    tq = min(tq, rows)
    nsub = tq // QBLK
    nb = rows // tq
    blk = (None, None, tq, GROUP_W)
    cur = lambda col: pl.BlockSpec(blk, lambda b, r, i: (b, r, i, col))
    prev = lambda col: pl.BlockSpec((None, None, QBLK, GROUP_W),
                                    lambda b, r, i: (b, r, jnp.maximum(i * nsub - 1, 0), col))
    out_spec = pl.BlockSpec(blk, lambda b, r, i: (b, r, i, 0))
    out_sds = jax.ShapeDtypeStruct((n, d, rows, GROUP_W), F32)
    return pl.pallas_call(
        _attn_prompt_body,
        grid=(n, d, nb),
        in_specs=[cur(0), prev(1), cur(1), prev(2), cur(2), _full_spec(bias)],
        out_specs=[out_spec, out_spec],
        out_shape=[out_sds, out_sds],
        compiler_params=_params("arbitrary", "arbitrary", "arbitrary"),
        name=f"attn_prompt_g{g}",
    )(qkv_g, qkv_g, qkv_g, qkv_g, qkv_g, bias)


def _sample_bias(rel_bias, g, d):
    steps = QBLK - np.arange(QBLK)
    tab = rel_bias[:, g * HPG:(g + 1) * HPG].astype(F32)
    on_grid = jnp.transpose(tab[_t5_bucket(steps * d)], (1, 0))[:, :, None]
    off_grid = jnp.full((HPG, QBLK, d - 1), NEG_INF, F32)
    cache_b = jnp.concatenate([on_grid, off_grid], axis=2).reshape(HPG, QBLK * d)
    self_b = jnp.broadcast_to(tab[0][:, None], (HPG, LANES))
    pad = lambda a: jnp.concatenate([a, jnp.zeros((8 - HPG, a.shape[1]), F32)], 0)
    return pad(cache_b), pad(self_b)


def _attn_sample_body(q_ref, kv_ref, cb_ref, sb_ref, o_ref, l_ref, *, bs):
    base = pl.program_id(0) * bs
    rowi = lax.broadcasted_iota(jnp.int32, (8, GROUP_W), 0)
    lane = lax.broadcasted_iota(jnp.int32, (8, GROUP_W), 1)
    own = (lane // HEAD_DIM) == rowi
    p = kv_ref.shape[-1]
    for j in range(bs):
        r = pl.ds(base + j, 1)
        q = q_ref[r, 0:GROUP_W] * (HEAD_DIM ** -0.5)
        kn = q_ref[r, GROUP_W:2 * GROUP_W]
        vn = q_ref[r, 2 * GROUP_W:3 * GROUP_W]
        qblk = jnp.where(own, jnp.broadcast_to(q, (8, GROUP_W)), 0.0)
        kt = kv_ref[j, 0].reshape(GROUP_W, p).astype(BF16)
        vt = kv_ref[j, 1].reshape(GROUP_W, p).astype(BF16)
        s = _dot(qblk.astype(BF16), kt) + cb_ref[...]
        s_self = jnp.sum(qblk * kn, axis=-1, keepdims=True) + sb_ref[:, 0:1]
        m = jnp.maximum(jnp.max(s, axis=-1, keepdims=True), s_self)
        e = jnp.exp(s - m)
        e_self = jnp.exp(s_self - m)
        den = jnp.sum(e, axis=-1, keepdims=True) + e_self
        o = (_dot_nt(e.astype(BF16), vt) + e_self * vn) / den
        o_ref[r, :] = jnp.sum(jnp.where(own, o, 0.0), axis=0, keepdims=True)
        lse = m + jnp.log(den)
        l_ref[r, :] = jnp.sum(jnp.where(own, lse, 0.0), axis=0, keepdims=True)


def _attn_sample(qkv_g, cache_t, layer, biases, g, bs):
    n = qkv_g.shape[0]
    p = cache_t.shape[-1]
    cb, sb = biases
    out_sds = jax.ShapeDtypeStruct((n, GROUP_W), F32)
    return pl.pallas_call(
        functools.partial(_attn_sample_body, bs=bs),
        grid=(n // bs,),
        in_specs=[_full_spec(qkv_g),
                  pl.BlockSpec((None, bs, 2, HPG, HEAD_DIM, p), lambda s: (layer, s, 0, 0, 0, 0)),
                  _full_spec(cb), _full_spec(sb)],
        out_specs=[pl.BlockSpec((n, GROUP_W), lambda s: (0, 0))] * 2,
        out_shape=[out_sds, out_sds],
        compiler_params=_params("arbitrary"),
        name=f"attn_sample_g{g}",
    )(qkv_g, cache_t, cb, sb)


def _softplus(x):
    return jnp.maximum(x, 0.0) + jnp.log(1.0 + jnp.exp(-jnp.abs(x)))


def _split3(x):
    hi = x.astype(BF16)
    r1 = x - hi.astype(F32)
    mid = r1.astype(BF16)
    lo = (r1 - mid.astype(F32)).astype(BF16)
    return hi, mid, lo


def _ssd_prompt_body(xbc_ref, z_ref, dt_ref, cw_ref, cb_ref, dtb_ref, alog_ref, dsk_ref, ng_ref,
                     y_ref, st_ref, xs_ref, state_ref, ybuf_ref):
    @pl.when(pl.program_id(1) == 0)
    def _():
        state_ref[...] = jnp.zeros_like(state_ref)
        xs_ref[:, 0:8, :] = jnp.zeros((xs_ref.shape[0], 8, SSD_CONV_CH), F32)

    for b in range(xbc_ref.shape[0]):
        _ssd_chunk(xbc_ref.at[b], z_ref.at[b], dt_ref.at[b], cw_ref, cb_ref, dtb_ref, alog_ref, dsk_ref, ng_ref,
                   y_ref.at[b], xs_ref.at[b], state_ref.at[b], ybuf_ref.at[b])
    st_ref[...] = state_ref[...]


def _ssd_chunk(xbc_ref, z_ref, dt_ref, cw_ref, cb_ref, dtb_ref, alog_ref, dsk_ref, ng_ref,
               y_ref, xs_ref, state_ref, ybuf_ref):
    q = SSD_CHUNK
    xs_ref[8:8 + q, :] = xbc_ref[...]
    conv = cb_ref[...]
    for k in range(4):
        conv = conv + cw_ref[k:k + 1, :] * xs_ref[5 + k:5 + k + q, :]
    xs_ref[0:8, :] = xs_ref[q:q + 8, :]
    xc = conv * _sigmoid(conv)

    dt = _softplus(dt_ref[...] + dtb_ref[...])
    a = -jnp.exp(alog_ref[...])
    da = dt * a
    ri = lax.broadcasted_iota(jnp.int32, (q, q), 0)
    ci = lax.broadcasted_iota(jnp.int32, (q, q), 1)
    causal = ri >= ci
    tril = jnp.where(causal, 1.0, 0.0).astype(BF16)
    cum = sum(_dot(tril, part) for part in _split3(da))
    cum_t = cum.T
    dt_t = dt.T
    cum_last = cum[q - 1:q, :]
    w_end = jnp.exp(cum_last - cum) * dt
    e_cum = jnp.exp(cum)
    e_last = jnp.exp(cum_last)

    for g in range(SSD_GROUPS):
        bg = xc[:, SSD_INNER + g * SSD_S:SSD_INNER + (g + 1) * SSD_S].astype(BF16)
        cg = xc[:, SSD_INNER + (SSD_GROUPS + g) * SSD_S:SSD_INNER + (SSD_GROUPS + g + 1) * SSD_S].astype(BF16)
        gmat = _dot_nt(cg, bg)
        for hh in range(SSD_HEADS // SSD_GROUPS):
            h = g * (SSD_HEADS // SSD_GROUPS) + hh
            sl = slice(h * SSD_P, (h + 1) * SSD_P)
            diff = cum[:, h:h + 1] - cum_t[h:h + 1, :]
            decay = jnp.where(causal, jnp.exp(jnp.where(causal, diff, 0.0)), 0.0)
            scores = gmat * decay * dt_t[h:h + 1, :]
            xh = xc[:, sl]
            hin = state_ref[h]
            y = _dot(scores.astype(BF16), xh.astype(BF16))
            y = y + _dot_nt(cg, hin.astype(BF16)) * e_cum[:, h:h + 1]
            y = y + dsk_ref[:, sl] * xh
            st = _dot_tn((xh * w_end[:, h:h + 1]).astype(BF16), bg)
            state_ref[h] = e_last[:, h:h + 1] * hin + st
            ybuf_ref[:, sl] = y

    zf = z_ref[...].astype(F32)
    yg = ybuf_ref[...] * (zf * _sigmoid(zf))
    y_ref[...] = _rms(yg, ng_ref[...]).astype(BF16)


def _ssd_prompt(xbc, z, dt, layer, n, length, cw, cb, dtb, alog, dsk, ng):
    nc = length // SSD_CHUNK
    nb = 1
    blk = lambda w: pl.BlockSpec((nb, SSD_CHUNK, w), lambda b, c: (b, c, 0))
    return pl.pallas_call(
        _ssd_prompt_body,
        grid=(n // nb, nc),
        in_specs=[blk(SSD_CONV_CH), blk(SSD_INNER), blk(DT_PAD)]
                 + [_layer_spec(w, layer) for w in (cw, cb, dtb, alog, dsk, ng)],
        out_specs=[blk(SSD_INNER),
                   pl.BlockSpec((nb, SSD_HEADS, SSD_P, SSD_S), lambda b, c: (b, 0, 0, 0))],
        out_shape=[jax.ShapeDtypeStruct((n, length, SSD_INNER), BF16),
                   jax.ShapeDtypeStruct((n, SSD_HEADS, SSD_P, SSD_S), F32)],
        scratch_shapes=[pltpu.VMEM((nb, SSD_CHUNK + 8, SSD_CONV_CH), F32),
                        pltpu.VMEM((nb, SSD_HEADS, SSD_P, SSD_S), F32),
                        pltpu.VMEM((nb, SSD_CHUNK, SSD_INNER), F32)],
        compiler_params=_params("arbitrary", "arbitrary"),
        name="ssd_prompt",
    )(xbc.reshape(n, length, SSD_CONV_CH), z.reshape(n, length, SSD_INNER),
      dt.reshape(n, length, DT_PAD), cw, cb, dtb, alog, dsk, ng)


def _ssd_conv_sample_body(xbc_ref, cs_ref, dt_ref, cw_ref, cb_ref, dtb_ref, xc_ref, dts_ref):
    conv = cb_ref[...] + cw_ref[3:4, :] * xbc_ref[...]
    for k in range(3):
        conv = conv + cw_ref[k:k + 1, :] * cs_ref[:, k * SSD_CONV_CH:(k + 1) * SSD_CONV_CH]
    xc_ref[...] = conv * _sigmoid(conv)
    dts_ref[...] = _softplus(dt_ref[...] + dtb_ref[...])


def _ssd_conv_sample(xbc, conv_state, dt, layer, cw, cb, dtb):
    n = xbc.shape[0]
    cs = conv_state.reshape(DEPTH, n, 3 * SSD_CONV_CH)
    return pl.pallas_call(
        _ssd_conv_sample_body,
        grid=(1,),
        in_specs=[_full_spec(xbc), _layer_spec(cs, layer), _full_spec(dt)]
                 + [_layer_spec(w, layer) for w in (cw, cb, dtb)],
        out_specs=[pl.BlockSpec((n, SSD_CONV_CH), lambda i: (0, 0)),
                   pl.BlockSpec((n, DT_PAD), lambda i: (0, 0))],
        out_shape=[jax.ShapeDtypeStruct((n, SSD_CONV_CH), F32), jax.ShapeDtypeStruct((n, DT_PAD), F32)],
        compiler_params=_params("arbitrary"),
        name="ssd_conv_sample",
    )(xbc, cs, dt, cw, cb, dtb)


def _ssd_state_sample_body(h0_ref, x_ref, b_ref, c_ref, dt_ref, alog_ref, dsk_ref, y_ref, hn_ref):
    n = x_ref.shape[0]
    ps = SSD_P * SSD_S
    x, bv, cv = x_ref[...], b_ref[...], c_ref[...]
    dt = dt_ref[...]
    dec = jnp.exp(dt * (-jnp.exp(alog_ref[...])))
    lane = lax.broadcasted_iota(jnp.int32, (SSD_P, ps), 1)
    rowp = lax.broadcasted_iota(jnp.int32, (SSD_P, ps), 0)
    rep = jnp.where(lane // SSD_S == rowp, 1.0, 0.0).astype(BF16)
    c2 = jnp.concatenate([cv, cv], axis=1)
    b2 = jnp.concatenate([bv, bv], axis=1)
    ct = jnp.concatenate([c2] * (ps // (2 * SSD_S)), axis=1)
    bt = jnp.concatenate([b2] * (ps // (2 * SSD_S)), axis=1)
    h0 = h0_ref[...]
    y_off = _dot_nt((h0 * ct).astype(BF16), rep) * dec[:, 0:SSD_P]
    cb = jnp.sum(cv * bv, axis=-1, keepdims=True)
    y_ref[...] = cb * dt[:, 0:SSD_P] * x + y_off + dsk_ref[:, 0:SSD_P] * x
    xrep = _dot(x.astype(BF16), rep)
    dtw = jnp.concatenate([dt] * (ps // LANES), axis=1)
    decw = jnp.concatenate([dec] * (ps // LANES), axis=1)
    hn_ref[...] = decw * h0 + dtw * xrep * bt


def _ssd_state_sample(state, layer, xh, dth, alog_h, dsk_h):
    n = xh.shape[1]
    ps = SSD_P * SSD_S
    rep = SSD_HEADS // SSD_GROUPS
    piece = lambda f: pl.BlockSpec((None, n, SSD_S), lambda h: (f(h), 0, 0))
    lanes = lambda: pl.BlockSpec((None, 1, LANES), lambda h: (layer * SSD_HEADS + h, 0, 0))
    return pl.pallas_call(
        _ssd_state_sample_body,
        grid=(SSD_HEADS,),
        in_specs=[pl.BlockSpec((None, n, ps), lambda h: (layer, 0, h)),
                  piece(lambda h: h), piece(lambda h: SSD_HEADS + h // rep),
                  piece(lambda h: SSD_HEADS + SSD_GROUPS + h // rep),
                  pl.BlockSpec((None, n, LANES), lambda h: (h, 0, 0)), lanes(), lanes()],
        out_specs=[pl.BlockSpec((None, n, SSD_P), lambda h: (h, 0, 0)),
                   pl.BlockSpec((n, ps), lambda h: (0, h))],
        out_shape=[jax.ShapeDtypeStruct((SSD_HEADS, n, SSD_P), F32),
                   jax.ShapeDtypeStruct((n, SSD_HEADS * ps), F32)],
        compiler_params=_params("arbitrary"),
        name="ssd_state_sample",
    )(state, xh, xh, xh, dth, alog_h, dsk_h)


def _ssd_gate_sample_body(y_ref, z_ref, ng_ref, o_ref):
    zf = z_ref[...].astype(F32)
    o_ref[...] = _rms(y_ref[...] * (zf * _sigmoid(zf)), ng_ref[...]).astype(BF16)


def _ssd_gate_sample(y, z, layer, ng):
    n = y.shape[0]
    return pl.pallas_call(
        _ssd_gate_sample_body,
        grid=(1,),
        in_specs=[_full_spec(y), _full_spec(z), _layer_spec(ng, layer)],
        out_specs=pl.BlockSpec((n, SSD_INNER), lambda i: (0, 0)),
        out_shape=jax.ShapeDtypeStruct((n, SSD_INNER), BF16),
        compiler_params=_params("arbitrary"),
        name="ssd_gate_sample",
    )(y, z, ng)


def _cmul(a, b):
    return a[0] * b[0] - a[1] * b[1], a[0] * b[1] + a[1] * b[0]


def _s5_discretise(lr, li, dl):
    delta = jnp.exp(dl)
    mag = jnp.exp(lr * delta)
    ab = (mag * jnp.cos(li * delta), mag * jnp.sin(li * delta))
    nr, ni = ab[0] - 1.0, ab[1]
    den = lr * lr + li * li
    return ab, ((nr * lr + ni * li) / den, (ni * lr - nr * li) / den)


def _s5_powers(ab, n):
    pw = [(jnp.ones_like(ab[0]), jnp.zeros_like(ab[0]))]
    for _ in range(n):
        pw.append(_cmul(pw[-1], ab))
    return pw


def _s5_tables_body(lr_r, li_r, dl_r, btr_r, bti_r, cr_r, ci_r, lr_l, li_l, dl_l,
                    lr_t, li_t, dl_t, cr_t, ci_t, tw_ref, v_ref, v0_ref, sc_ref):
    tc, nl = S5_TC, LANES
    ab, f = _s5_discretise(lr_r[...], li_r[...], dl_r[...])
    bb = _cmul(f, (btr_r[...], bti_r[...]))
    pw = _s5_powers(ab, tc - 1)
    cc = (cr_r[...], ci_r[...])
    bbr, bbi = bb[0].astype(BF16), bb[1].astype(BF16)
    rg = lax.broadcasted_iota(jnp.int32, (nl, nl), 0) // S5_CH
    cg = lax.broadcasted_iota(jnp.int32, (nl, nl), 1) // S5_CH
    kblk = []
    for tau in range(tc):
        cp = _cmul(cc, pw[tau])
        k = _dot_nt(bbr, cp[0].astype(BF16)) - _dot_nt(bbi, cp[1].astype(BF16))
        kblk.append(jnp.where(rg == cg, k, 0.0).astype(BF16))
    zero = jnp.zeros((nl, nl), BF16)
    for j in range(tc):
        for i in range(tc):
            tw_ref[j * nl:(j + 1) * nl, i * nl:(i + 1) * nl] = kblk[i - j] if i >= j else zero
    ab_l, _ = _s5_discretise(lr_l[...], li_l[...], dl_l[...])
    pw_l = _s5_powers(ab_l, tc)
    pair = lambda x: jnp.concatenate([x, x], axis=1)
    widen = lambda x: jnp.concatenate([pair(x)] * (S5_BLK // (2 * S5_STATE)), axis=1)
    bwr, bwi = widen(bb[0]), widen(bb[1])
    wrow = lax.broadcasted_iota(jnp.int32, (nl, S5_BLK), 0) // S5_CH
    wcol = lax.broadcasted_iota(jnp.int32, (nl, S5_BLK), 1) // S5_STATE
    wmask = wrow == wcol
    for j in range(tc):
        pr, pi = pw_l[tc - 1 - j]
        tw_ref[j * nl:(j + 1) * nl, tc * nl:tc * nl + S5_BLK] = jnp.where(wmask, bwr * pr - bwi * pi, 0.0).astype(BF16)
        tw_ref[j * nl:(j + 1) * nl, tc * nl + S5_BLK:] = jnp.where(wmask, bwr * pi + bwi * pr, 0.0).astype(BF16)
    a8 = pw_l[tc]
    a16 = _cmul(a8, a8)
    a32 = _cmul(a16, a16)
    rows = [ab_l, a16, a32, a8]
    for _ in range(tc - 1):
        rows.append(_cmul(rows[-1], a8))
    sc_ref[...] = jnp.zeros_like(sc_ref)
    for k, rw in enumerate(rows):
        sc_ref[0, k:k + 1, :] = rw[0]
        sc_ref[1, k:k + 1, :] = rw[1]
    ab_t, _ = _s5_discretise(lr_t[...], li_t[...], dl_t[...])
    pw_t = _s5_powers(ab_t, tc)
    cc_t = (cr_t[...], ci_t[...])
    vrow = lax.broadcasted_iota(jnp.int32, (S5_BLK, nl), 0) // S5_STATE
    vcol = lax.broadcasted_iota(jnp.int32, (S5_BLK, nl), 1) // S5_CH
    vmask = vrow == vcol
    tall = lambda x: jnp.concatenate([x] * (S5_BLK // S5_STATE), axis=0)
    for tau in range(tc + 1):
        cp = _cmul(cc_t, pw_t[tau])
        vr = jnp.where(vmask, tall(cp[0]), 0.0).astype(BF16)
        vi = jnp.where(vmask, tall(-cp[1]), 0.0).astype(BF16)
        if tau == 0:
            v0_ref[0:S5_BLK, :] = vr
            v0_ref[S5_BLK:, :] = vi
        else:
            v_ref[0:S5_BLK, (tau - 1) * nl:tau * nl] = vr
            v_ref[S5_BLK:, (tau - 1) * nl:tau * nl] = vi


def _s5_tables(a_re, a_im, log_dt, b_re, b_im, c_re, c_im, d_skip):
    g, s, c, gb = S5_GROUPS, S5_STATE, S5_CH, S5_GB
    ldt = jnp.broadcast_to(log_dt[:, :, None], (DEPTH, g, s))
    rows = lambda a: jnp.repeat(a, c, axis=1)
    flat = lambda a: a.reshape(DEPTH, gb, 1, S5_BLK)
    cols = lambda a: jnp.repeat(jnp.swapaxes(a, 1, 2), c, axis=2)
    by_rows = [rows(a_re), rows(a_im), rows(ldt),
               jnp.swapaxes(b_re, 2, 3).reshape(DEPTH, g * c, s), jnp.swapaxes(b_im, 2, 3).reshape(DEPTH, g * c, s),
               c_re.reshape(DEPTH, g * c, s), c_im.reshape(DEPTH, g * c, s)]
    by_lane = [flat(a_re), flat(a_im), flat(ldt)]
    by_col = [cols(a_re), cols(a_im), cols(ldt),
              jnp.transpose(c_re, (0, 3, 1, 2)).reshape(DEPTH, s, g * c),
              jnp.transpose(c_im, (0, 3, 1, 2)).reshape(DEPTH, s, g * c)]
    in_specs = ([pl.BlockSpec((None, LANES, s), lambda l, b: (l, b, 0))] * len(by_rows)
                + [pl.BlockSpec((None, None, 1, S5_BLK), lambda l, b: (l, b, 0, 0))] * len(by_lane)
                + [pl.BlockSpec((None, s, LANES), lambda l, b: (l, 0, b))] * len(by_col))
    blk = lambda *shape: pl.BlockSpec((None, None) + shape, lambda l, b: (l, b) + (0,) * len(shape))
    tcw = S5_TC * LANES
    tw, v_blk, v0_blk, scan = pl.pallas_call(
        _s5_tables_body,
        grid=(DEPTH, gb),
        in_specs=in_specs,
        out_specs=[blk(tcw, tcw + 2 * S5_BLK), blk(2 * S5_BLK, tcw), blk(2 * S5_BLK, LANES), blk(2, 16, S5_BLK)],
        out_shape=[jax.ShapeDtypeStruct((DEPTH, gb, tcw, tcw + 2 * S5_BLK), BF16),
                   jax.ShapeDtypeStruct((DEPTH, gb, 2 * S5_BLK, tcw), BF16),
                   jax.ShapeDtypeStruct((DEPTH, gb, 2 * S5_BLK, LANES), BF16),
                   jax.ShapeDtypeStruct((DEPTH, gb, 2, 16, S5_BLK), F32)],
        compiler_params=_params("arbitrary", "arbitrary"),
        name="s5_tables",
    )(*by_rows, *by_lane, *by_col)
    return tw, v_blk, v0_blk, scan, d_skip.reshape(DEPTH, gb, 1, LANES)


def _s5_prompt_body(u_ref, tw_ref, v_ref, sc_ref, d_ref, y_ref, hout_ref,
                    sre_ref, sim_ref, hin_ref, cr_ref, ci_ref):
    r = pl.program_id(2)
    tr = sre_ref.shape[0]

    @pl.when(r == 0)
    def _():
        cr_ref[...] = jnp.zeros_like(cr_ref)
        ci_ref[...] = jnp.zeros_like(ci_ref)

    u = jnp.concatenate([u_ref[j] for j in range(S5_TC)], axis=1)
    tw = _dot(u, tw_ref[...])
    yd = tw[:, 0:S5_TC * 128]
    sre_ref[...] = tw[:, S5_TC * 128:S5_TC * 128 + S5_BLK]
    sim_ref[...] = tw[:, S5_TC * 128 + S5_BLK:]

    a8 = (sc_ref[0, 3:4, :], sc_ref[1, 3:4, :])
    a16 = (sc_ref[0, 1:2, :], sc_ref[1, 1:2, :])
    a32 = (sc_ref[0, 2:3, :], sc_ref[1, 2:3, :])
    apw = (sc_ref[0, 3:11, :], sc_ref[1, 3:11, :])
    row = lax.broadcasted_iota(jnp.int32, (8, S5_BLK), 0)

    sub = lax.broadcasted_iota(jnp.int32, (tr, S5_BLK), 0) % 8
    xr, xi = sre_ref[...], sim_ref[...]
    for k, (ar, ai) in ((1, a8), (2, a16), (4, a32)):
        pr = jnp.where(sub >= k, pltpu.roll(xr, k, axis=0), 0.0)
        pi = jnp.where(sub >= k, pltpu.roll(xi, k, axis=0), 0.0)
        xr, xi = xr + ar * pr - ai * pi, xi + ar * pi + ai * pr
    sre_ref[...] = xr
    sim_ref[...] = xi

    def tile(t, carry):
        c_re, c_im = carry
        rows = pl.ds(pl.multiple_of(t * 8, 8), 8)
        xr, xi = sre_ref[rows, :], sim_ref[rows, :]
        xr = xr + apw[0] * c_re - apw[1] * c_im
        xi = xi + apw[0] * c_im + apw[1] * c_re
        hin_ref[rows, 0:S5_BLK] = jnp.where(row >= 1, pltpu.roll(xr, 1, axis=0), c_re)
        hin_ref[rows, S5_BLK:] = jnp.where(row >= 1, pltpu.roll(xi, 1, axis=0), c_im)
        return xr[7:8, :], xi[7:8, :]

    c_re, c_im = lax.fori_loop(0, tr // 8, tile, (cr_ref[...], ci_ref[...]))
    cr_ref[...] = c_re
    ci_ref[...] = c_im
    hout_ref[:, 0:S5_BLK] = c_re
    hout_ref[:, S5_BLK:] = c_im

    d_row = jnp.concatenate([d_ref[...]] * S5_TC, axis=1)
    y = yd + _dot(hin_ref[...].astype(BF16), v_ref[...]) + d_row * u.astype(F32)
    for j in range(S5_TC):
        y_ref[j] = y[:, j * 128:(j + 1) * 128].astype(BF16)


def _s5_prompt(u8, layer, n, length, tw, v_blk, scan, d_blk, tr=512):
    rows = length // S5_TC
    tr = min(tr, rows)
    nr = rows // tr
    chunk_blk = pl.BlockSpec((S5_TC, tr, 128), lambda gb, b, r: (0, b * nr + r, gb))
    lay = lambda a: pl.BlockSpec((None, None) + tuple(a.shape[2:]),
                                 lambda gb, b, r: (layer, gb) + (0,) * (a.ndim - 2))
    y8, hout = pl.pallas_call(
        _s5_prompt_body,
        grid=(S5_GB, n, nr),
        in_specs=[chunk_blk, lay(tw), lay(v_blk), lay(scan), lay(d_blk)],
        out_specs=[chunk_blk,
                   pl.BlockSpec((None, None, 1, 2 * S5_BLK), lambda gb, b, r: (b, gb, 0, 0))],
        out_shape=[jax.ShapeDtypeStruct((S5_TC, n * rows, S5_W), BF16),
                   jax.ShapeDtypeStruct((n, S5_GB, 1, 2 * S5_BLK), F32)],
        scratch_shapes=[pltpu.VMEM((tr, S5_BLK), F32), pltpu.VMEM((tr, S5_BLK), F32),
                        pltpu.VMEM((tr, 2 * S5_BLK), F32),
                        pltpu.VMEM((1, S5_BLK), F32), pltpu.VMEM((1, S5_BLK), F32)],
        compiler_params=_params("arbitrary", "arbitrary", "arbitrary"),
        name="s5_prompt",
    )(u8, tw, v_blk, scan, d_blk)
    state = jnp.transpose(hout.reshape(n, S5_GB, 2, 8, S5_STATE), (0, 2, 1, 3, 4))
    return y8, state.reshape(n, 2, S5_GROUPS, S5_STATE)


def _s5_sample_body(u_ref, hr_ref, hi_ref, w_ref, v0_ref, sc_ref, d_ref, y_ref, nr_ref, ni_ref):
    u = u_ref[...]
    bu = _dot(u, w_ref[...])
    ar, ai = sc_ref[0, 0:1, :], sc_ref[1, 0:1, :]
    hr, hi = hr_ref[...], hi_ref[...]
    nr = ar * hr - ai * hi + bu[:, 0:S5_BLK]
    ni = ar * hi + ai * hr + bu[:, S5_BLK:]
    nr_ref[...] = nr
    ni_ref[...] = ni
    hcat = jnp.concatenate([nr, ni], axis=1).astype(BF16)
    y_ref[...] = (_dot(hcat, v0_ref[...]) + d_ref[...] * u.astype(F32)).astype(BF16)


def _s5_sample(u, state, layer, tw, v0_blk, scan, d_blk):
    n = u.shape[0]
    lay = lambda a: pl.BlockSpec((None, None) + tuple(a.shape[2:]),
                                 lambda gb: (layer, gb) + (0,) * (a.ndim - 2))
    st = lambda off: pl.BlockSpec((None, n, S5_BLK), lambda gb: (layer, 0, off + gb))
    half = S5_GROUPS * S5_STATE
    y, nr, ni = pl.pallas_call(
        _s5_sample_body,
        grid=(S5_GB,),
        in_specs=[pl.BlockSpec((n, 128), lambda gb: (0, gb)), st(0), st(S5_GB),
                  pl.BlockSpec((None, None, 128, 2 * S5_BLK), lambda gb: (layer, gb, S5_TC - 1, 1)),
                  lay(v0_blk), lay(scan), lay(d_blk)],
        out_specs=[pl.BlockSpec((n, 128), lambda gb: (0, gb)),
                   pl.BlockSpec((n, S5_BLK), lambda gb: (0, gb)),
                   pl.BlockSpec((n, S5_BLK), lambda gb: (0, gb))],
        out_shape=[jax.ShapeDtypeStruct((n, S5_W), BF16), jax.ShapeDtypeStruct((n, half), F32),
                   jax.ShapeDtypeStruct((n, half), F32)],
        compiler_params=_params("arbitrary"),
        name="s5_sample",
    )(u, state, state, tw, v0_blk, scan, d_blk)
    return y, jnp.stack([nr, ni], axis=1).reshape(n, 2, S5_GROUPS, S5_STATE)


def kernel(x_prompt, x_sample, cache_kv_w128, cache_kv_w512, cache_kv_w2048, state_ssd, state_conv, state_s5, p_prompt, p_sample, attn_rel_bias, ffn1_norm, ffn1_w_gate, ffn1_w_up, ffn1_w_down, mix_norm, w_in, ssd_conv_w, ssd_conv_b, ssd_dt_bias, ssd_a_log, ssd_d, ssd_norm, s5_a_re, s5_a_im, s5_b_re, s5_b_im, s5_c_re, s5_c_im, s5_d, s5_log_dt, w_s5_glu, w_branch_attn, w_branch_ssd, w_out, ffn2_norm, ffn2_w_gate, ffn2_w_up, ffn2_w_down, pe_norm, w_pe_gate, w_pe_proj, final_norm):
    n_p, len_p, _ = x_prompt.shape
    n_s = x_sample.shape[0]
    m_p = n_p * len_p
    assert x_sample.shape[1] == 1 and len_p % (QBLK * ATTN_PATTERNS[-1][1]) == 0
    caches = (cache_kv_w128, cache_kv_w512, cache_kv_w2048)
    dils = tuple(d for _, d in ATTN_PATTERNS)
    for cache, (w, d) in zip(caches, ATTN_PATTERNS):
        assert cache.shape[3] == w == QBLK * d
    caches_t = [jnp.transpose(c, (0, 1, 2, 4, 5, 3)) for c in caches]
    sample_bs = (16, 8, 2)

    bf = lambda a: a.astype(BF16)
    vec = lambda a: a.reshape(DEPTH, 1, a.shape[-1])
    w1g, w1u, w1d = bf(ffn1_w_gate), bf(ffn1_w_up), bf(ffn1_w_down)
    w2g, w2u, w2d = bf(ffn2_w_gate), bf(ffn2_w_up), bf(ffn2_w_down)
    dt_lo = OFF_XBC + SSD_CONV_CH
    n_dt = SSD_HEADS
    w_qkv = jnp.swapaxes(w_in[..., :OFF_Z].reshape(DEPTH, D_MODEL, 3, 3, GROUP_W), 2, 3)
    w_in_p = bf(jnp.concatenate(
        [w_qkv.reshape(DEPTH, D_MODEL, OFF_Z), w_in[..., OFF_Z:dt_lo + n_dt],
         jnp.zeros((DEPTH, D_MODEL, DT_PAD - n_dt), w_in.dtype), w_in[..., dt_lo + n_dt:]], axis=-1))
    wba, wbs, wglu, wout = bf(w_branch_attn), bf(w_branch_ssd), bf(w_s5_glu), bf(w_out)
    wpg, wpp = bf(w_pe_gate), bf(w_pe_proj)
    n1, nm, n2, npe = vec(ffn1_norm), vec(mix_norm), vec(ffn2_norm), vec(pe_norm)
    fin = final_norm.reshape(1, D_MODEL)
    cb = vec(ssd_conv_b)
    lane_pad = lambda a: jnp.pad(a, ((0, 0), (0, DT_PAD - a.shape[-1]))).reshape(DEPTH, 1, DT_PAD)
    dtb, alog = lane_pad(ssd_dt_bias), lane_pad(ssd_a_log)
    dsk = jnp.repeat(ssd_d, SSD_P, axis=-1).reshape(DEPTH, 1, SSD_INNER)
    ng = vec(ssd_norm)
    per_head = lambda a: jnp.broadcast_to(a.reshape(DEPTH * SSD_HEADS, 1, 1), (DEPTH * SSD_HEADS, 1, LANES))
    alog_h, dsk_h = per_head(ssd_a_log), per_head(ssd_d)

    tw, v_blk, v0_blk, scan, d_blk = _s5_tables(s5_a_re, s5_a_im, s5_log_dt, s5_b_re, s5_b_im,
                                                s5_c_re, s5_c_im, s5_d)

    bias_p = [_prompt_bias(attn_rel_bias, g, d) for g, (_, d) in enumerate(ATTN_PATTERNS)]
    bias_s = [_sample_bias(attn_rel_bias, g, d) for g, (_, d) in enumerate(ATTN_PATTERNS)]

    pp = p_prompt.reshape(DEPTH, m_p, PE_DIM)
    psm = p_sample.reshape(DEPTH, n_s, PE_DIM)
    st_ssd = state_ssd.reshape(DEPTH, n_s, SSD_HEADS * SSD_P * SSD_S)
    st_s5 = state_s5.reshape(DEPTH, n_s, 2 * S5_GROUPS * S5_STATE)

    xp = x_prompt.reshape(m_p, D_MODEL)
    xs = x_sample.reshape(n_s, D_MODEL)
    tm_p, tm_f, tm_s = 256, 512, n_s
    kv_p, kv_s = [[], [], []], [[], [], []]
    ssd_p, ssd_s, conv_p, conv_s, s5_p, s5_s = [], [], [], [], [], []

    def kv_prompt(qg, d):
        n, _, rows, _ = qg.shape
        t = qg[:, :, rows - QBLK:, GROUP_W:].reshape(n, d, QBLK, 2, HPG, HEAD_DIM)
        return jnp.transpose(t, (0, 3, 2, 1, 4, 5)).reshape(n, 2, QBLK * d, HPG, HEAD_DIM)

    for i in range(DEPTH):
        last = i == DEPTH - 1
        xp = _ffn(xp, i, tm_f, n1, w1g, w1u, w1d)
        *qgs, z, xbc, dt, u, gates = _inproj(xp, i, tm_p, nm, w_in_p, n_p, len_p, dils, True)
        parts = []
        for g, d in enumerate(dils):
            parts += list(_attn_prompt(qgs[g], bias_p[g], g))
            kv_p[g].append(kv_prompt(qgs[g], d))
        y_ssd, st = _ssd_prompt(xbc, z, dt, i, n_p, len_p, ssd_conv_w, cb, dtb, alog, dsk, ng)
        ssd_p.append(st)
        conv_p.append(xbc.reshape(n_p, len_p, SSD_CONV_CH)[:, len_p - 3:])
        y_s5, st5 = _s5_prompt(u, i, n_p, len_p, tw, v_blk, scan, d_blk)
        s5_p.append(st5)
        xp = _merge(xp, i, tm_p, parts, dils, len_p, y_ssd.reshape(m_p, SSD_INNER), y_s5, gates,
                    wba, wbs, wglu, wout)
        xp = _ffn(xp, i, tm_f, n2, w2g, w2u, w2d, pe=(pp, npe, wpg, wpp), final=fin if last else None)
        xs = _ffn(xs, i, tm_s, n1, w1g, w1u, w1d)
        *qgs, z, xbc, dt, u, gates = _inproj(xs, i, tm_s, nm, w_in_p, 1, n_s, (1, 1, 1), False)
        parts = []
        for g in range(len(dils)):
            qg = qgs[g].reshape(n_s, ATTN_W)
            o_s, l_s = _attn_sample(qg, caches_t[g], i, bias_s[g], g, sample_bs[g])
            parts += [o_s.reshape(1, 1, n_s, GROUP_W), l_s.reshape(1, 1, n_s, GROUP_W)]
            kv_s[g].append(qg[:, GROUP_W:].reshape(n_s, 2, 1, HPG, HEAD_DIM))
        xc, dts = _ssd_conv_sample(xbc, state_conv, dt, i, ssd_conv_w, cb, dtb)
        conv_s.append(jnp.concatenate([state_conv[i][:, 1:], xbc[:, None, :]], axis=1))
        xh = jnp.transpose(xc.reshape(n_s, SSD_CONV_CH // SSD_S, SSD_S), (1, 0, 2))
        dth = jnp.broadcast_to(jnp.transpose(dts[:, :SSD_HEADS])[:, :, None], (SSD_HEADS, n_s, LANES))
        yh, hn = _ssd_state_sample(st_ssd, i, xh, dth, alog_h, dsk_h)
        ssd_s.append(hn.reshape(n_s, SSD_HEADS, SSD_P, SSD_S))
        y_ssd = _ssd_gate_sample(jnp.transpose(yh, (1, 0, 2)).reshape(n_s, SSD_INNER), z, i, ng)
        y_s5, st5 = _s5_sample(u, st_s5, i, tw, v0_blk, scan, d_blk)
        s5_s.append(st5)
        xs = _merge(xs, i, tm_s, parts, (1, 1, 1), n_s, y_ssd, y_s5, gates, wba, wbs, wglu, wout)
        xs = _ffn(xs, i, tm_s, n2, w2g, w2u, w2d, pe=(psm, npe, wpg, wpp), final=fin if last else None)

    return (xp.reshape(n_p, len_p, D_MODEL), xs.reshape(n_s, 1, D_MODEL),
            jnp.stack(kv_p[0], 0), jnp.stack(kv_s[0], 0),
            jnp.stack(kv_p[1], 0), jnp.stack(kv_s[1], 0),
            jnp.stack(kv_p[2], 0), jnp.stack(kv_s[2], 0),
            jnp.stack(ssd_p, 0), jnp.stack(ssd_s, 0),
            jnp.stack(conv_p, 0), jnp.stack(conv_s, 0),
            jnp.stack(s5_p, 0), jnp.stack(s5_s, 0))
```

```python
import functools
import math

import numpy as np
import jax
import jax.numpy as jnp
from jax import lax
from jax.experimental import pallas as pl
from jax.experimental.pallas import tpu as pltpu

F32 = jnp.float32
BF16 = jnp.bfloat16

D_MODEL = 1024
DEPTH = 4
HEAD_DIM = 64
ATTN_PATTERNS = ((128, 1), (512, 4), (2048, 16))
HPG = 4
GROUP_W = HPG * HEAD_DIM
ATTN_W = 3 * GROUP_W
QBLK = 128
N_BUCKETS = 32
BUCKET_MAX_DIST = 2048
NEG_INF = -1e30
SSD_HEADS = 12
SSD_P = 64
SSD_S = 64
SSD_GROUPS = 4
SSD_INNER = SSD_HEADS * SSD_P
SSD_CONV_CH = SSD_INNER + 2 * SSD_GROUPS * SSD_S
SSD_CHUNK = 256
S5_GROUPS = 48
S5_CH = 16
S5_STATE = 64
S5_W = S5_GROUPS * S5_CH
S5_TC = 8
S5_GB = 6
S5_BLK = 8 * S5_STATE
D_FF = 2816
PE_DIM = 256
RMS_EPS = 1e-6
LANES = 128
DT_PAD = LANES
QKV_W = 3 * ATTN_W
MID_XBC = SSD_INNER
MID_DT = MID_XBC + SSD_CONV_CH
MID_W = MID_DT + DT_PAD
REST_G = S5_W
REST_W = REST_G + 3 * D_MODEL
VMEM_LIMIT = 56 * 1024 * 1024


def _params(*sem):
    return pltpu.CompilerParams(dimension_semantics=sem, vmem_limit_bytes=VMEM_LIMIT)


def _rms(x, g):
    inv = lax.rsqrt(jnp.mean(x * x, axis=-1, keepdims=True) + RMS_EPS)
    return (x * inv) * g


def _dot(a, b):
    return jnp.dot(a, b, preferred_element_type=F32)


def _dot_nt(a, b):
    return lax.dot_general(a, b, (((1,), (1,)), ((), ())), preferred_element_type=F32)


def _dot_tn(a, b):
    return lax.dot_general(a, b, (((0,), (0,)), ((), ())), preferred_element_type=F32)


def _sigmoid(x):
    return jax.nn.sigmoid(x)


def _layer_spec(arr, layer):
    nd = arr.ndim - 1
    return pl.BlockSpec((None,) + tuple(arr.shape[1:]),
                        lambda *_: (layer,) + (0,) * nd,
                        pipeline_mode=pl.Buffered(1))


def _full_spec(arr):
    nd = arr.ndim
    return pl.BlockSpec(tuple(arr.shape), lambda *_: (0,) * nd, pipeline_mode=pl.Buffered(1))


def _ffn_body(*refs, has_pe, has_final):
    it = iter(refs)
    x_ref, g_ref, wg_ref, wu_ref, wd_ref = (next(it) for _ in range(5))
    if has_pe:
        p_ref, pn_ref, wpg_ref, wpp_ref = (next(it) for _ in range(4))
    if has_final:
        fn_ref = next(it)
    o_ref = next(it)
    act_ref = next(it)
    x = x_ref[...]
    h = _rms(x, g_ref[...]).astype(BF16)
    half = D_FF // 2
    for f0 in (0, half):
        gate = _dot(h, wg_ref[:, f0:f0 + half])
        up = _dot(h, wu_ref[:, f0:f0 + half])
        act_ref[:, f0:f0 + half] = (gate * _sigmoid(gate) * up).astype(BF16)
    y = x + 0.5 * _dot(act_ref[...], wd_ref[...])
    if has_pe:
        h2 = _rms(y, pn_ref[...]).astype(BF16)
        gt = _sigmoid(_dot(h2, wpg_ref[...]))
        y = y + gt * _dot(p_ref[...].astype(BF16), wpp_ref[...])
    if has_final:
        y = _rms(y, fn_ref[...])
    o_ref[...] = y


def _ffn(x, layer, tm, norm, wg, wu, wd, pe=None, final=None):
    m = x.shape[0]
    row = lambda i: (i, 0)
    args = [x, norm, wg, wu, wd]
    specs = [pl.BlockSpec((tm, D_MODEL), row), _layer_spec(norm, layer), _layer_spec(wg, layer),
             _layer_spec(wu, layer), _layer_spec(wd, layer)]
    if pe is not None:
        p_all, pn, wpg, wpp = pe
        args += [p_all, pn, wpg, wpp]
        specs += [pl.BlockSpec((None, tm, PE_DIM), lambda i: (layer, i, 0)), _layer_spec(pn, layer),
                  _layer_spec(wpg, layer), _layer_spec(wpp, layer)]
    if final is not None:
        args.append(final)
        specs.append(_full_spec(final))
    return pl.pallas_call(
        functools.partial(_ffn_body, has_pe=pe is not None, has_final=final is not None),
        grid=(m // tm,),
        in_specs=specs,
        out_specs=pl.BlockSpec((tm, D_MODEL), row),
        out_shape=jax.ShapeDtypeStruct((m, D_MODEL), F32),
        scratch_shapes=[pltpu.VMEM((tm, D_FF), BF16)],
        compiler_params=_params("arbitrary"),
        name="ffn",
    )(*args)


def _inproj_body(x_ref, g_ref, wq_ref, wm_ref, wr_ref, qa_ref, qb_ref, qc_ref, z_ref, xbc_ref, dt_ref, u_ref,
                 gates_ref, stage_ref, *, dils, prompt):
    h = _rms(x_ref[...], g_ref[...]).astype(BF16)
    tm = x_ref.shape[0]
    for g, (out_ref, d) in enumerate(zip((qa_ref, qb_ref, qc_ref), dils)):
        res = _dot(h, wq_ref[:, g * ATTN_W:(g + 1) * ATTN_W])
        if d == 1:
            out_ref[0] = res
        else:
            for c in range(ATTN_W // LANES):
                stage_ref[c] = res[:, c * LANES:(c + 1) * LANES]
            for r in range(d):
                for c in range(ATTN_W // LANES):
                    out_ref[r, :, c * LANES:(c + 1) * LANES] = stage_ref[c, pl.ds(r, tm // d, stride=d), :]
    z_ref[...] = _dot(h, wm_ref[:, 0:MID_XBC]).astype(BF16)
    xbc_ref[...] = _dot(h, wm_ref[:, MID_XBC:MID_DT])
    dt_ref[...] = _dot(h, wm_ref[:, MID_DT:MID_W])
    u = _dot(h, wr_ref[:, 0:REST_G])
    if prompt:
        for c in range(S5_W // LANES):
            stage_ref[c] = u[:, c * LANES:(c + 1) * LANES]
        for j in range(S5_TC):
            for c in range(S5_W // LANES):
                u_ref[j, :, c * LANES:(c + 1) * LANES] = (
                    stage_ref[c, pl.ds(j, tm // S5_TC, stride=S5_TC), :].astype(BF16))
    else:
        u_ref[...] = u.astype(BF16)
    for k in range(3):
        c0 = REST_G + k * D_MODEL
        gates_ref[:, k * D_MODEL:(k + 1) * D_MODEL] = _sigmoid(_dot(h, wr_ref[:, c0:c0 + D_MODEL])).astype(BF16)


def _inproj(x, layer, tm, norm, w, n, length, dils, prompt):
    m = x.shape[0]
    tps = length // tm
    row = lambda i: (i, 0)
    q_specs = [pl.BlockSpec((None, d, tm // d, ATTN_W), lambda i: (i // tps, 0, i % tps, 0)) for d in dils]
    q_shapes = [jax.ShapeDtypeStruct((n, d, length // d, ATTN_W), F32) for d in dils]
    flat = lambda wd, dt: (pl.BlockSpec((tm, wd), row), jax.ShapeDtypeStruct((m, wd), dt))
    if prompt:
        u_out = (pl.BlockSpec((S5_TC, tm // S5_TC, S5_W), lambda i: (0, i, 0)),
                 jax.ShapeDtypeStruct((S5_TC, m // S5_TC, S5_W), BF16))
    else:
        u_out = flat(S5_W, BF16)
    rest = [flat(SSD_INNER, BF16), flat(SSD_CONV_CH, F32), flat(DT_PAD, F32), u_out, flat(3 * D_MODEL, BF16)]
    return pl.pallas_call(
        functools.partial(_inproj_body, dils=dils, prompt=prompt),
        grid=(m // tm,),
        in_specs=[pl.BlockSpec((tm, D_MODEL), row), _layer_spec(norm, layer)] + [_layer_spec(a, layer) for a in w],
        out_specs=q_specs + [s for s, _ in rest],
        out_shape=q_shapes + [s for _, s in rest],
        scratch_shapes=[pltpu.VMEM((ATTN_W // LANES, tm, LANES), F32)],
        compiler_params=_params("arbitrary"),
        name="inproj",
    )(x, norm, *w)


def _merge_body(x_ref, o0, l0, o1, l1, o2, l2, yssd_ref, ys5_ref, gates_ref,
                wba_ref, wbs_ref, wglu_ref, wout_ref, out_ref, stage_ref, s5_stage_ref, *, dils, s5_chunked):
    tm = x_ref.shape[0]
    parts = []
    for k, (ref, d) in enumerate(zip((o0, l0, o1, l1, o2, l2), (dils[0], dils[0], dils[1], dils[1], dils[2], dils[2]))):
        if d == 1:
            parts.append(ref[0])
        else:
            nl = GROUP_W // LANES
            for r in range(d):
                for c in range(nl):
                    stage_ref[k * nl + c, pl.ds(r, tm // d, stride=d), :] = ref[r, :, c * LANES:(c + 1) * LANES]
            parts.append(jnp.concatenate([stage_ref[k * nl + c] for c in range(nl)], axis=1))
    oa, la, ob, lb, oc, lc = parts
    mx = jnp.maximum(jnp.maximum(la, lb), lc)
    ea, eb, ec = jnp.exp(la - mx), jnp.exp(lb - mx), jnp.exp(lc - mx)
    attn = (ea * oa + eb * ob + ec * oc) / (ea + eb + ec)
    acc = gates_ref[:, 0:D_MODEL].astype(F32) * _dot(attn.astype(BF16), wba_ref[...])
    acc = acc + gates_ref[:, D_MODEL:2 * D_MODEL].astype(F32) * _dot(yssd_ref[...], wbs_ref[...])
    if s5_chunked:
        nl6 = S5_W // LANES
        for j in range(S5_TC):
            for c in range(nl6):
                s5_stage_ref[c, pl.ds(j, tm // S5_TC, stride=S5_TC), :] = (
                    ys5_ref[j, :, c * LANES:(c + 1) * LANES].astype(F32))
        ys5 = jnp.concatenate([s5_stage_ref[c] for c in range(nl6)], axis=1)
    else:
        ys5 = ys5_ref[...].astype(F32)
    gl = _dot(jax.nn.gelu(ys5).astype(BF16), wglu_ref[...])
    s5 = gl[:, :D_MODEL] * _sigmoid(gl[:, D_MODEL:])
    acc = acc + gates_ref[:, 2 * D_MODEL:].astype(F32) * s5
    out_ref[...] = x_ref[...] + _dot(acc.astype(BF16), wout_ref[...])


def _merge(x, layer, tm, attn_parts, dils, length, y_ssd, y_s5, gates, wba, wbs, wglu, wout):
    m = x.shape[0]
    tps = length // tm
    row = lambda i: (i, 0)
    s5_chunked = y_s5.ndim == 3
    args = [x] + list(attn_parts) + [y_ssd, y_s5, gates, wba, wbs, wglu, wout]
    part_specs = []
    for d in dils:
        part_specs += [pl.BlockSpec((None, d, tm // d, GROUP_W), lambda i: (i // tps, 0, i % tps, 0))] * 2
    s5_spec = (pl.BlockSpec((S5_TC, tm // S5_TC, S5_W), lambda i: (0, i, 0)) if s5_chunked
               else pl.BlockSpec((tm, S5_W), row))
    specs = ([pl.BlockSpec((tm, D_MODEL), row)] + part_specs
             + [pl.BlockSpec((tm, SSD_INNER), row), s5_spec, pl.BlockSpec((tm, 3 * D_MODEL), row)]
             + [_layer_spec(w, layer) for w in (wba, wbs, wglu, wout)])
    return pl.pallas_call(
        functools.partial(_merge_body, dils=dils, s5_chunked=s5_chunked),
        grid=(m // tm,),
        in_specs=specs,
        out_specs=pl.BlockSpec((tm, D_MODEL), row),
        out_shape=jax.ShapeDtypeStruct((m, D_MODEL), F32),
        scratch_shapes=[pltpu.VMEM((6 * GROUP_W // LANES, tm, LANES), F32),
                        pltpu.VMEM((S5_W // LANES, tm, LANES), F32)],
        compiler_params=_params("arbitrary"),
        name="merge",
    )(*args)


def _t5_bucket(dist):
    max_exact = N_BUCKETS // 2
    d = np.asarray(dist).astype(np.int32)
    df = np.maximum(d, 1).astype(np.float32)
    large = max_exact + (np.log(df / max_exact) / math.log(BUCKET_MAX_DIST / max_exact)
                         * (N_BUCKETS - max_exact)).astype(np.int32)
    return np.where(d < max_exact, d, np.minimum(large, N_BUCKETS - 1)).astype(np.int32)


def _prompt_bias(rel_bias, g, d):
    w = 2 * QBLK
    tab = rel_bias[:, g * HPG:(g + 1) * HPG].astype(F32)
    near = jnp.transpose(tab[_t5_bucket(np.arange(QBLK, -1, -1) * d)], (1, 0))
    row0 = jnp.concatenate([near, jnp.full((HPG, w - QBLK - 1), NEG_INF, F32)], axis=1)
    x = jnp.concatenate([row0, row0, jnp.full((HPG, 1), NEG_INF, F32)], axis=1)
    flat = jnp.tile(x, (1, QBLK))[:, :QBLK * 2 * w]
    return flat.reshape(HPG, QBLK, 2 * w)[:, :, w:]


def _attn_prompt_body(q_ref, kp_ref, kc_ref, vp_ref, vc_ref, bias_ref, o_ref, l_ref):
    nsub = q_ref.shape[0] // QBLK
    first = pl.program_id(2) == 0
    lane = lax.broadcasted_iota(jnp.int32, (1, 2 * QBLK), 1)
    pen = jnp.where(jnp.logical_and(first, lane < QBLK), NEG_INF, 0.0).astype(F32)
    q = (q_ref[...] * (HEAD_DIM ** -0.5)).astype(BF16)
    k = jnp.concatenate([kp_ref[...], kc_ref[...]], axis=0).astype(BF16)
    v = jnp.concatenate([vp_ref[...], vc_ref[...]], axis=0).astype(BF16)
    for s in range(nsub):
        rq = slice(s * QBLK, (s + 1) * QBLK)
        rk = slice(s * QBLK, (s + 2) * QBLK)
        for h in range(HPG):
            sl = slice(h * HEAD_DIM, (h + 1) * HEAD_DIM)
            sc = _dot_nt(q[rq, sl], k[rk, sl]) + bias_ref[h]
            if s == 0:
                sc = sc + pen
            m = jnp.max(sc, axis=-1, keepdims=True)
            e = jnp.exp(sc - m)
            den = jnp.sum(e, axis=-1, keepdims=True)
            o_ref[rq, sl] = _dot(e.astype(BF16), v[rk, sl]) / den
            l_ref[rq, sl] = jnp.broadcast_to(m + jnp.log(den), (QBLK, HEAD_DIM))


def _attn_prompt(qkv_g, bias, g, tq=512):
    n, d, rows, _ = qkv_g.shape
    tq = min(tq, rows)
    nsub = tq // QBLK
    nb = rows // tq
    blk = (None, None, tq, GROUP_W)
    cur = lambda col: pl.BlockSpec(blk, lambda b, r, i: (b, r, i, col))
    prev = lambda col: pl.BlockSpec((None, None, QBLK, GROUP_W),
                                    lambda b, r, i: (b, r, jnp.maximum(i * nsub - 1, 0), col))
    out_spec = pl.BlockSpec(blk, lambda b, r, i: (b, r, i, 0))
    out_sds = jax.ShapeDtypeStruct((n, d, rows, GROUP_W), F32)
    return pl.pallas_call(
        _attn_prompt_body,
        grid=(n, d, nb),
        in_specs=[cur(0), prev(1), cur(1), prev(2), cur(2), _full_spec(bias)],
        out_specs=[out_spec, out_spec],
        out_shape=[out_sds, out_sds],
        compiler_params=_params("arbitrary", "arbitrary", "arbitrary"),
        name=f"attn_prompt_g{g}",
    )(qkv_g, qkv_g, qkv_g, qkv_g, qkv_g, bias)


def _sample_bias(rel_bias, g, d):
    steps = QBLK - np.arange(QBLK)
    tab = rel_bias[:, g * HPG:(g + 1) * HPG].astype(F32)
    on_grid = jnp.transpose(tab[_t5_bucket(steps * d)], (1, 0))[:, :, None]
    off_grid = jnp.full((HPG, QBLK, d - 1), NEG_INF, F32)
    cache_b = jnp.concatenate([on_grid, off_grid], axis=2).reshape(HPG, QBLK * d)
    self_b = jnp.broadcast_to(tab[0][:, None], (HPG, LANES))
    pad = lambda a: jnp.concatenate([a, jnp.zeros((8 - HPG, a.shape[1]), F32)], 0)
    return pad(cache_b), pad(self_b)


def _attn_sample_body(q_ref, kv_ref, cb_ref, sb_ref, o_ref, l_ref, *, bs):
    base = pl.program_id(0) * bs
    rowi = lax.broadcasted_iota(jnp.int32, (8, GROUP_W), 0)
    lane = lax.broadcasted_iota(jnp.int32, (8, GROUP_W), 1)
    own = (lane // HEAD_DIM) == rowi
    p = kv_ref.shape[-1]
    for j in range(bs):
        r = pl.ds(base + j, 1)
        q = q_ref[r, 0:GROUP_W] * (HEAD_DIM ** -0.5)
        kn = q_ref[r, GROUP_W:2 * GROUP_W]
        vn = q_ref[r, 2 * GROUP_W:3 * GROUP_W]
        qblk = jnp.where(own, jnp.broadcast_to(q, (8, GROUP_W)), 0.0)
        kt = kv_ref[j, 0].reshape(GROUP_W, p).astype(BF16)
        vt = kv_ref[j, 1].reshape(GROUP_W, p).astype(BF16)
        s = _dot(qblk.astype(BF16), kt) + cb_ref[...]
        s_self = jnp.sum(qblk * kn, axis=-1, keepdims=True) + sb_ref[:, 0:1]
        m = jnp.maximum(jnp.max(s, axis=-1, keepdims=True), s_self)
        e = jnp.exp(s - m)
        e_self = jnp.exp(s_self - m)
        den = jnp.sum(e, axis=-1, keepdims=True) + e_self
        o = (_dot_nt(e.astype(BF16), vt) + e_self * vn) / den
        o_ref[r, :] = jnp.sum(jnp.where(own, o, 0.0), axis=0, keepdims=True)
        lse = m + jnp.log(den)
        l_ref[r, :] = jnp.sum(jnp.where(own, lse, 0.0), axis=0, keepdims=True)


def _attn_sample(qkv_g, cache_t, layer, biases, g, bs):
    n = qkv_g.shape[0]
    p = cache_t.shape[-1]
    cb, sb = biases
    out_sds = jax.ShapeDtypeStruct((n, GROUP_W), F32)
    return pl.pallas_call(
        functools.partial(_attn_sample_body, bs=bs),
        grid=(n // bs,),
        in_specs=[_full_spec(qkv_g),
                  pl.BlockSpec((None, bs, 2, HPG, HEAD_DIM, p), lambda s: (layer, s, 0, 0, 0, 0)),
                  _full_spec(cb), _full_spec(sb)],
        out_specs=[pl.BlockSpec((n, GROUP_W), lambda s: (0, 0))] * 2,
        out_shape=[out_sds, out_sds],
        compiler_params=_params("arbitrary"),
        name=f"attn_sample_g{g}",
    )(qkv_g, cache_t, cb, sb)


def _softplus(x):
    return jnp.maximum(x, 0.0) + jnp.log(1.0 + jnp.exp(-jnp.abs(x)))


def _split3(x):
    hi = x.astype(BF16)
    r1 = x - hi.astype(F32)
    mid = r1.astype(BF16)
    lo = (r1 - mid.astype(F32)).astype(BF16)
    return hi, mid, lo


def _ssd_prompt_body(xbc_ref, z_ref, dt_ref, cw_ref, cb_ref, dtb_ref, alog_ref, dsk_ref, ng_ref,
                     y_ref, st_ref, xs_ref, state_ref, ybuf_ref):
    @pl.when(pl.program_id(1) == 0)
    def _():
        state_ref[...] = jnp.zeros_like(state_ref)
        xs_ref[:, 0:8, :] = jnp.zeros((xs_ref.shape[0], 8, SSD_CONV_CH), F32)

    for b in range(xbc_ref.shape[0]):
        _ssd_chunk(xbc_ref.at[b], z_ref.at[b], dt_ref.at[b], cw_ref, cb_ref, dtb_ref, alog_ref, dsk_ref, ng_ref,
                   y_ref.at[b], xs_ref.at[b], state_ref.at[b], ybuf_ref.at[b])
    st_ref[...] = state_ref[...]


def _ssd_chunk(xbc_ref, z_ref, dt_ref, cw_ref, cb_ref, dtb_ref, alog_ref, dsk_ref, ng_ref,
               y_ref, xs_ref, state_ref, ybuf_ref):
    q = SSD_CHUNK
    xs_ref[8:8 + q, :] = xbc_ref[...]
    full = xs_ref[...]
    conv = cb_ref[...] + cw_ref[3:4, :] * full[8:8 + q, :]
    for s in range(1, 4):
        conv = conv + cw_ref[3 - s:4 - s, :] * pltpu.roll(full, s, axis=0)[8:8 + q, :]
    xs_ref[0:8, :] = full[q:q + 8, :]
    xc = conv * _sigmoid(conv)

    dt = _softplus(dt_ref[...] + dtb_ref[...])
    a = -jnp.exp(alog_ref[...])
    da = dt * a
    ri = lax.broadcasted_iota(jnp.int32, (q, q), 0)
    ci = lax.broadcasted_iota(jnp.int32, (q, q), 1)
    causal = ri >= ci
    tril = jnp.where(causal, 1.0, 0.0).astype(BF16)
    cum = sum(_dot(tril, part) for part in _split3(da))
    cum_t = cum.T
    dt_t = dt.T
    cum_last = cum[q - 1:q, :]
    w_end = jnp.exp(cum_last - cum) * dt
    e_cum = jnp.exp(cum)
    e_last = jnp.exp(cum_last)

    for g in range(SSD_GROUPS):
        bgt = xc[:, SSD_INNER + g * SSD_S:SSD_INNER + (g + 1) * SSD_S].T.astype(BF16)
        cg = xc[:, SSD_INNER + (SSD_GROUPS + g) * SSD_S:SSD_INNER + (SSD_GROUPS + g + 1) * SSD_S].astype(BF16)
        gmat = _dot(cg, bgt)
        for hh in range(SSD_HEADS // SSD_GROUPS):
            h = g * (SSD_HEADS // SSD_GROUPS) + hh
            sl = slice(h * SSD_P, (h + 1) * SSD_P)
            diff = cum[:, h:h + 1] - cum_t[h:h + 1, :]
            decay = jnp.exp(jnp.where(causal, diff, NEG_INF))
            scores = gmat * decay * dt_t[h:h + 1, :]
            xh = xc[:, sl]
            hin = state_ref[h]
            y = _dot(scores.astype(BF16), xh.astype(BF16))
            y = y + _dot(cg, hin.astype(BF16)) * e_cum[:, h:h + 1]
            y = y + dsk_ref[:, sl] * xh
            st = _dot(bgt, (xh * w_end[:, h:h + 1]).astype(BF16))
            state_ref[h] = e_last[:, h:h + 1] * hin + st
            ybuf_ref[:, sl] = y

    zf = z_ref[...].astype(F32)
    yg = ybuf_ref[...] * (zf * _sigmoid(zf))
    y_ref[...] = _rms(yg, ng_ref[...]).astype(BF16)


def _ssd_prompt(xbc, z, dt, layer, n, length, cw, cb, dtb, alog, dsk, ng):
    nc = length // SSD_CHUNK
    nb = 1
    blk = lambda w: pl.BlockSpec((nb, SSD_CHUNK, w), lambda b, c: (b, c, 0))
    return pl.pallas_call(
        _ssd_prompt_body,
        grid=(n // nb, nc),
        in_specs=[blk(SSD_CONV_CH), blk(SSD_INNER), blk(DT_PAD)]
                 + [_layer_spec(w, layer) for w in (cw, cb, dtb, alog, dsk, ng)],
        out_specs=[blk(SSD_INNER),
                   pl.BlockSpec((nb, SSD_HEADS, SSD_P, SSD_S), lambda b, c: (b, 0, 0, 0))],
        out_shape=[jax.ShapeDtypeStruct((n, length, SSD_INNER), BF16),
                   jax.ShapeDtypeStruct((n, SSD_HEADS, SSD_P, SSD_S), F32)],
        scratch_shapes=[pltpu.VMEM((nb, SSD_CHUNK + 8, SSD_CONV_CH), F32),
                        pltpu.VMEM((nb, SSD_HEADS, SSD_P, SSD_S), F32),
                        pltpu.VMEM((nb, SSD_CHUNK, SSD_INNER), F32)],
        compiler_params=_params("arbitrary", "arbitrary"),
        name="ssd_prompt",
    )(xbc.reshape(n, length, SSD_CONV_CH), z.reshape(n, length, SSD_INNER),
      dt.reshape(n, length, DT_PAD), cw, cb, dtb, alog, dsk, ng)


def _ssd_conv_sample_body(xbc_ref, cs_ref, dt_ref, cw_ref, cb_ref, dtb_ref, xc_ref, dts_ref):
    conv = cb_ref[...] + cw_ref[3:4, :] * xbc_ref[...]
    for k in range(3):
        conv = conv + cw_ref[k:k + 1, :] * cs_ref[:, k * SSD_CONV_CH:(k + 1) * SSD_CONV_CH]
    xc_ref[...] = conv * _sigmoid(conv)
    dts_ref[...] = _softplus(dt_ref[...] + dtb_ref[...])


def _ssd_conv_sample(xbc, conv_state, dt, layer, cw, cb, dtb):
    n = xbc.shape[0]
    cs = conv_state.reshape(DEPTH, n, 3 * SSD_CONV_CH)
    return pl.pallas_call(
        _ssd_conv_sample_body,
        grid=(1,),
        in_specs=[_full_spec(xbc), _layer_spec(cs, layer), _full_spec(dt)]
                 + [_layer_spec(w, layer) for w in (cw, cb, dtb)],
        out_specs=[pl.BlockSpec((n, SSD_CONV_CH), lambda i: (0, 0)),
                   pl.BlockSpec((n, DT_PAD), lambda i: (0, 0))],
        out_shape=[jax.ShapeDtypeStruct((n, SSD_CONV_CH), F32), jax.ShapeDtypeStruct((n, DT_PAD), F32)],
        compiler_params=_params("arbitrary"),
        name="ssd_conv_sample",
    )(xbc, cs, dt, cw, cb, dtb)


def _ssd_state_sample_body(h0_ref, x_ref, b_ref, c_ref, dt_ref, alog_ref, dsk_ref, y_ref, hn_ref):
    n = x_ref.shape[0]
    ps = SSD_P * SSD_S
    x, bv, cv = x_ref[...], b_ref[...], c_ref[...]
    dt = dt_ref[...]
    dec = jnp.exp(dt * (-jnp.exp(alog_ref[...])))
    lane = lax.broadcasted_iota(jnp.int32, (SSD_P, ps), 1)
    rowp = lax.broadcasted_iota(jnp.int32, (SSD_P, ps), 0)
    rep = jnp.where(lane // SSD_S == rowp, 1.0, 0.0).astype(BF16)
    c2 = jnp.concatenate([cv, cv], axis=1)
    b2 = jnp.concatenate([bv, bv], axis=1)
    ct = jnp.concatenate([c2] * (ps // (2 * SSD_S)), axis=1)
    bt = jnp.concatenate([b2] * (ps // (2 * SSD_S)), axis=1)
    h0 = h0_ref[...]
    y_off = _dot_nt((h0 * ct).astype(BF16), rep) * dec[:, 0:SSD_P]
    cb = jnp.sum(cv * bv, axis=-1, keepdims=True)
    y_ref[...] = cb * dt[:, 0:SSD_P] * x + y_off + dsk_ref[:, 0:SSD_P] * x
    xrep = _dot(x.astype(BF16), rep)
    dtw = jnp.concatenate([dt] * (ps // LANES), axis=1)
    decw = jnp.concatenate([dec] * (ps // LANES), axis=1)
    hn_ref[...] = decw * h0 + dtw * xrep * bt


def _ssd_state_sample(state, layer, xh, dth, alog_h, dsk_h):
    n = xh.shape[1]
    ps = SSD_P * SSD_S
    rep = SSD_HEADS // SSD_GROUPS
    piece = lambda f: pl.BlockSpec((None, n, SSD_S), lambda h: (f(h), 0, 0))
    lanes = lambda: pl.BlockSpec((None, 1, LANES), lambda h: (layer * SSD_HEADS + h, 0, 0))
    return pl.pallas_call(
        _ssd_state_sample_body,
        grid=(SSD_HEADS,),
        in_specs=[pl.BlockSpec((None, n, ps), lambda h: (layer, 0, h)),
                  piece(lambda h: h), piece(lambda h: SSD_HEADS + h // rep),
                  piece(lambda h: SSD_HEADS + SSD_GROUPS + h // rep),
                  pl.BlockSpec((None, n, LANES), lambda h: (h, 0, 0)), lanes(), lanes()],
        out_specs=[pl.BlockSpec((None, n, SSD_P), lambda h: (h, 0, 0)),
                   pl.BlockSpec((n, ps), lambda h: (0, h))],
        out_shape=[jax.ShapeDtypeStruct((SSD_HEADS, n, SSD_P), F32),
                   jax.ShapeDtypeStruct((n, SSD_HEADS * ps), F32)],
        compiler_params=_params("arbitrary"),
        name="ssd_state_sample",
    )(state, xh, xh, xh, dth, alog_h, dsk_h)


def _ssd_gate_sample_body(y_ref, z_ref, ng_ref, o_ref):
    zf = z_ref[...].astype(F32)
    o_ref[...] = _rms(y_ref[...] * (zf * _sigmoid(zf)), ng_ref[...]).astype(BF16)


def _ssd_gate_sample(y, z, layer, ng):
    n = y.shape[0]
    return pl.pallas_call(
        _ssd_gate_sample_body,
        grid=(1,),
        in_specs=[_full_spec(y), _full_spec(z), _layer_spec(ng, layer)],
        out_specs=pl.BlockSpec((n, SSD_INNER), lambda i: (0, 0)),
        out_shape=jax.ShapeDtypeStruct((n, SSD_INNER), BF16),
        compiler_params=_params("arbitrary"),
        name="ssd_gate_sample",
    )(y, z, ng)


def _cmul(a, b):
    return a[0] * b[0] - a[1] * b[1], a[0] * b[1] + a[1] * b[0]


def _s5_discretise(lr, li, dl):
    delta = jnp.exp(dl)
    mag = jnp.exp(lr * delta)
    ab = (mag * jnp.cos(li * delta), mag * jnp.sin(li * delta))
    nr, ni = ab[0] - 1.0, ab[1]
    den = lr * lr + li * li
    return ab, ((nr * lr + ni * li) / den, (ni * lr - nr * li) / den)


def _s5_powers(ab, n):
    pw = [(jnp.ones_like(ab[0]), jnp.zeros_like(ab[0]))]
    for _ in range(n):
        pw.append(_cmul(pw[-1], ab))
    return pw


def _s5_tables_body(lr_r, li_r, dl_r, btr_r, bti_r, cr_r, ci_r, lr_l, li_l, dl_l,
                    lr_t, li_t, dl_t, cr_t, ci_t, tw_ref, v_ref, v0_ref, sc_ref):
    tc, nl = S5_TC, LANES
    ab, f = _s5_discretise(lr_r[...], li_r[...], dl_r[...])
    bb = _cmul(f, (btr_r[...], bti_r[...]))
    pw = _s5_powers(ab, tc - 1)
    cc = (cr_r[...], ci_r[...])
    bbr, bbi = bb[0].astype(BF16), bb[1].astype(BF16)
    rg = lax.broadcasted_iota(jnp.int32, (nl, nl), 0) // S5_CH
    cg = lax.broadcasted_iota(jnp.int32, (nl, nl), 1) // S5_CH
    kblk = []
    for tau in range(tc):
        cp = _cmul(cc, pw[tau])
        k = _dot_nt(bbr, cp[0].astype(BF16)) - _dot_nt(bbi, cp[1].astype(BF16))
        kblk.append(jnp.where(rg == cg, k, 0.0).astype(BF16))
    zero = jnp.zeros((nl, nl), BF16)
    for j in range(tc):
        for i in range(tc):
            tw_ref[j * nl:(j + 1) * nl, i * nl:(i + 1) * nl] = kblk[i - j] if i >= j else zero
    ab_l, _ = _s5_discretise(lr_l[...], li_l[...], dl_l[...])
    pw_l = _s5_powers(ab_l, tc)
    pair = lambda x: jnp.concatenate([x, x], axis=1)
    widen = lambda x: jnp.concatenate([pair(x)] * (S5_BLK // (2 * S5_STATE)), axis=1)
    bwr, bwi = widen(bb[0]), widen(bb[1])
    wrow = lax.broadcasted_iota(jnp.int32, (nl, S5_BLK), 0) // S5_CH
    wcol = lax.broadcasted_iota(jnp.int32, (nl, S5_BLK), 1) // S5_STATE
    wmask = wrow == wcol
    for j in range(tc):
        pr, pi = pw_l[tc - 1 - j]
        tw_ref[j * nl:(j + 1) * nl, tc * nl:tc * nl + S5_BLK] = jnp.where(wmask, bwr * pr - bwi * pi, 0.0).astype(BF16)
        tw_ref[j * nl:(j + 1) * nl, tc * nl + S5_BLK:] = jnp.where(wmask, bwr * pi + bwi * pr, 0.0).astype(BF16)
    a8 = pw_l[tc]
    a16 = _cmul(a8, a8)
    a32 = _cmul(a16, a16)
    rows = [ab_l, a16, a32, a8]
    for _ in range(tc - 1):
        rows.append(_cmul(rows[-1], a8))
    sc_ref[...] = jnp.zeros_like(sc_ref)
    for k, rw in enumerate(rows):
        sc_ref[0, k:k + 1, :] = rw[0]
        sc_ref[1, k:k + 1, :] = rw[1]
    ab_t, _ = _s5_discretise(lr_t[...], li_t[...], dl_t[...])
    pw_t = _s5_powers(ab_t, tc)
    cc_t = (cr_t[...], ci_t[...])
    vrow = lax.broadcasted_iota(jnp.int32, (S5_BLK, nl), 0) // S5_STATE
    vcol = lax.broadcasted_iota(jnp.int32, (S5_BLK, nl), 1) // S5_CH
    vmask = vrow == vcol
    tall = lambda x: jnp.concatenate([x] * (S5_BLK // S5_STATE), axis=0)
    for tau in range(tc + 1):
        cp = _cmul(cc_t, pw_t[tau])
        vr = jnp.where(vmask, tall(cp[0]), 0.0).astype(BF16)
        vi = jnp.where(vmask, tall(-cp[1]), 0.0).astype(BF16)
        if tau == 0:
            v0_ref[0:S5_BLK, :] = vr
            v0_ref[S5_BLK:, :] = vi
        else:
            v_ref[0:S5_BLK, (tau - 1) * nl:tau * nl] = vr
            v_ref[S5_BLK:, (tau - 1) * nl:tau * nl] = vi


def _s5_tables(a_re, a_im, log_dt, b_re, b_im, c_re, c_im, d_skip):
    g, s, c, gb = S5_GROUPS, S5_STATE, S5_CH, S5_GB
    ldt = jnp.broadcast_to(log_dt[:, :, None], (DEPTH, g, s))
    rows = lambda a: jnp.repeat(a, c, axis=1)
    flat = lambda a: a.reshape(DEPTH, gb, 1, S5_BLK)
    cols = lambda a: jnp.repeat(jnp.swapaxes(a, 1, 2), c, axis=2)
    by_rows = [rows(a_re), rows(a_im), rows(ldt),
               jnp.swapaxes(b_re, 2, 3).reshape(DEPTH, g * c, s), jnp.swapaxes(b_im, 2, 3).reshape(DEPTH, g * c, s),
               c_re.reshape(DEPTH, g * c, s), c_im.reshape(DEPTH, g * c, s)]
    by_lane = [flat(a_re), flat(a_im), flat(ldt)]
    by_col = [cols(a_re), cols(a_im), cols(ldt),
              jnp.transpose(c_re, (0, 3, 1, 2)).reshape(DEPTH, s, g * c),
              jnp.transpose(c_im, (0, 3, 1, 2)).reshape(DEPTH, s, g * c)]
    in_specs = ([pl.BlockSpec((None, LANES, s), lambda l, b: (l, b, 0))] * len(by_rows)
                + [pl.BlockSpec((None, None, 1, S5_BLK), lambda l, b: (l, b, 0, 0))] * len(by_lane)
                + [pl.BlockSpec((None, s, LANES), lambda l, b: (l, 0, b))] * len(by_col))
    blk = lambda *shape: pl.BlockSpec((None, None) + shape, lambda l, b: (l, b) + (0,) * len(shape))
    tcw = S5_TC * LANES
    tw, v_blk, v0_blk, scan = pl.pallas_call(
        _s5_tables_body,
        grid=(DEPTH, gb),
        in_specs=in_specs,
        out_specs=[blk(tcw, tcw + 2 * S5_BLK), blk(2 * S5_BLK, tcw), blk(2 * S5_BLK, LANES), blk(2, 16, S5_BLK)],
        out_shape=[jax.ShapeDtypeStruct((DEPTH, gb, tcw, tcw + 2 * S5_BLK), BF16),
                   jax.ShapeDtypeStruct((DEPTH, gb, 2 * S5_BLK, tcw), BF16),
                   jax.ShapeDtypeStruct((DEPTH, gb, 2 * S5_BLK, LANES), BF16),
                   jax.ShapeDtypeStruct((DEPTH, gb, 2, 16, S5_BLK), F32)],
        compiler_params=_params("arbitrary", "arbitrary"),
        name="s5_tables",
    )(*by_rows, *by_lane, *by_col)
    return tw, v_blk, v0_blk, scan, d_skip.reshape(DEPTH, gb, 1, LANES)


def _s5_prompt_body(u_ref, tw_ref, v_ref, sc_ref, d_ref, y_ref, hout_ref,
                    sre_ref, sim_ref, hin_ref, cr_ref, ci_ref):
    r = pl.program_id(2)
    tr = sre_ref.shape[0]

    @pl.when(r == 0)
    def _():
        cr_ref[...] = jnp.zeros_like(cr_ref)
        ci_ref[...] = jnp.zeros_like(ci_ref)

    u = jnp.concatenate([u_ref[j] for j in range(S5_TC)], axis=1)
    tw = _dot(u, tw_ref[...])
    yd = tw[:, 0:S5_TC * 128]
    sre_ref[...] = tw[:, S5_TC * 128:S5_TC * 128 + S5_BLK]
    sim_ref[...] = tw[:, S5_TC * 128 + S5_BLK:]

    a8 = (sc_ref[0, 3:4, :], sc_ref[1, 3:4, :])
    a16 = (sc_ref[0, 1:2, :], sc_ref[1, 1:2, :])
    a32 = (sc_ref[0, 2:3, :], sc_ref[1, 2:3, :])
    apw = (sc_ref[0, 3:11, :], sc_ref[1, 3:11, :])
    row = lax.broadcasted_iota(jnp.int32, (8, S5_BLK), 0)

    sub = lax.broadcasted_iota(jnp.int32, (tr, S5_BLK), 0) % 8
    xr, xi = sre_ref[...], sim_ref[...]
    for k, (ar, ai) in ((1, a8), (2, a16), (4, a32)):
        pr = jnp.where(sub >= k, pltpu.roll(xr, k, axis=0), 0.0)
        pi = jnp.where(sub >= k, pltpu.roll(xi, k, axis=0), 0.0)
        xr, xi = xr + ar * pr - ai * pi, xi + ar * pi + ai * pr
    sre_ref[...] = xr
    sim_ref[...] = xi

    def tile(t, carry):
        c_re, c_im = carry
        rows = pl.ds(pl.multiple_of(t * 8, 8), 8)
        xr, xi = sre_ref[rows, :], sim_ref[rows, :]
        xr = xr + apw[0] * c_re - apw[1] * c_im
        xi = xi + apw[0] * c_im + apw[1] * c_re
        hin_ref[rows, 0:S5_BLK] = jnp.where(row >= 1, pltpu.roll(xr, 1, axis=0), c_re)
        hin_ref[rows, S5_BLK:] = jnp.where(row >= 1, pltpu.roll(xi, 1, axis=0), c_im)
        return xr[7:8, :], xi[7:8, :]

    c_re, c_im = lax.fori_loop(0, tr // 8, tile, (cr_ref[...], ci_ref[...]))
    cr_ref[...] = c_re
    ci_ref[...] = c_im
    hout_ref[:, 0:S5_BLK] = c_re
    hout_ref[:, S5_BLK:] = c_im

    d_row = jnp.concatenate([d_ref[...]] * S5_TC, axis=1)
    y = yd + _dot(hin_ref[...].astype(BF16), v_ref[...]) + d_row * u.astype(F32)
    for j in range(S5_TC):
        y_ref[j] = y[:, j * 128:(j + 1) * 128].astype(BF16)


def _s5_prompt(u8, layer, n, length, tw, v_blk, scan, d_blk, tr=512):
    rows = length // S5_TC
    tr = min(tr, rows)
    nr = rows // tr
    chunk_blk = pl.BlockSpec((S5_TC, tr, 128), lambda gb, b, r: (0, b * nr + r, gb))
    lay = lambda a: pl.BlockSpec((None, None) + tuple(a.shape[2:]),
                                 lambda gb, b, r: (layer, gb) + (0,) * (a.ndim - 2))
    y8, hout = pl.pallas_call(
        _s5_prompt_body,
        grid=(S5_GB, n, nr),
        in_specs=[chunk_blk, lay(tw), lay(v_blk), lay(scan), lay(d_blk)],
        out_specs=[chunk_blk,
                   pl.BlockSpec((None, None, 1, 2 * S5_BLK), lambda gb, b, r: (b, gb, 0, 0))],
        out_shape=[jax.ShapeDtypeStruct((S5_TC, n * rows, S5_W), BF16),
                   jax.ShapeDtypeStruct((n, S5_GB, 1, 2 * S5_BLK), F32)],
        scratch_shapes=[pltpu.VMEM((tr, S5_BLK), F32), pltpu.VMEM((tr, S5_BLK), F32),
                        pltpu.VMEM((tr, 2 * S5_BLK), F32),
                        pltpu.VMEM((1, S5_BLK), F32), pltpu.VMEM((1, S5_BLK), F32)],
        compiler_params=_params("arbitrary", "arbitrary", "arbitrary"),
        name="s5_prompt",
    )(u8, tw, v_blk, scan, d_blk)
    state = jnp.transpose(hout.reshape(n, S5_GB, 2, 8, S5_STATE), (0, 2, 1, 3, 4))
    return y8, state.reshape(n, 2, S5_GROUPS, S5_STATE)


def _s5_sample_body(u_ref, hr_ref, hi_ref, w_ref, v0_ref, sc_ref, d_ref, y_ref, nr_ref, ni_ref):
    u = u_ref[...]
    bu = _dot(u, w_ref[...])
    ar, ai = sc_ref[0, 0:1, :], sc_ref[1, 0:1, :]
    hr, hi = hr_ref[...], hi_ref[...]
    nr = ar * hr - ai * hi + bu[:, 0:S5_BLK]
    ni = ar * hi + ai * hr + bu[:, S5_BLK:]
    nr_ref[...] = nr
    ni_ref[...] = ni
    hcat = jnp.concatenate([nr, ni], axis=1).astype(BF16)
    y_ref[...] = (_dot(hcat, v0_ref[...]) + d_ref[...] * u.astype(F32)).astype(BF16)


def _s5_sample(u, state, layer, tw, v0_blk, scan, d_blk):
    n = u.shape[0]
    lay = lambda a: pl.BlockSpec((None, None) + tuple(a.shape[2:]),
                                 lambda gb: (layer, gb) + (0,) * (a.ndim - 2))
    st = lambda off: pl.BlockSpec((None, n, S5_BLK), lambda gb: (layer, 0, off + gb))
    half = S5_GROUPS * S5_STATE
    y, nr, ni = pl.pallas_call(
        _s5_sample_body,
        grid=(S5_GB,),
        in_specs=[pl.BlockSpec((n, 128), lambda gb: (0, gb)), st(0), st(S5_GB),
                  pl.BlockSpec((None, None, 128, 2 * S5_BLK), lambda gb: (layer, gb, S5_TC - 1, 1)),
                  lay(v0_blk), lay(scan), lay(d_blk)],
        out_specs=[pl.BlockSpec((n, 128), lambda gb: (0, gb)),
                   pl.BlockSpec((n, S5_BLK), lambda gb: (0, gb)),
                   pl.BlockSpec((n, S5_BLK), lambda gb: (0, gb))],
        out_shape=[jax.ShapeDtypeStruct((n, S5_W), BF16), jax.ShapeDtypeStruct((n, half), F32),
                   jax.ShapeDtypeStruct((n, half), F32)],
        compiler_params=_params("arbitrary"),
        name="s5_sample",
    )(u, state, state, tw, v0_blk, scan, d_blk)
    return y, jnp.stack([nr, ni], axis=1).reshape(n, 2, S5_GROUPS, S5_STATE)


def kernel(x_prompt, x_sample, cache_kv_w128, cache_kv_w512, cache_kv_w2048, state_ssd, state_conv, state_s5, p_prompt, p_sample, attn_rel_bias, ffn1_norm, ffn1_w_gate, ffn1_w_up, ffn1_w_down, mix_norm, w_in, ssd_conv_w, ssd_conv_b, ssd_dt_bias, ssd_a_log, ssd_d, ssd_norm, s5_a_re, s5_a_im, s5_b_re, s5_b_im, s5_c_re, s5_c_im, s5_d, s5_log_dt, w_s5_glu, w_branch_attn, w_branch_ssd, w_out, ffn2_norm, ffn2_w_gate, ffn2_w_up, ffn2_w_down, pe_norm, w_pe_gate, w_pe_proj, final_norm):
    n_p, len_p, _ = x_prompt.shape
    n_s = x_sample.shape[0]
    m_p = n_p * len_p
    assert x_sample.shape[1] == 1 and len_p % (QBLK * ATTN_PATTERNS[-1][1]) == 0
    caches = (cache_kv_w128, cache_kv_w512, cache_kv_w2048)
    dils = tuple(d for _, d in ATTN_PATTERNS)
    for cache, (w, d) in zip(caches, ATTN_PATTERNS):
        assert cache.shape[3] == w == QBLK * d
    caches_t = [jnp.transpose(c, (0, 1, 2, 4, 5, 3)) for c in caches]
    sample_bs = (16, 8, 2)

    bf = lambda a: a.astype(BF16)
    vec = lambda a: a.reshape(DEPTH, 1, a.shape[-1])
    w1g, w1u, w1d = bf(ffn1_w_gate), bf(ffn1_w_up), bf(ffn1_w_down)
    w2g, w2u, w2d = bf(ffn2_w_gate), bf(ffn2_w_up), bf(ffn2_w_down)
    mid_hi = QKV_W + MID_DT + SSD_HEADS
    w_qkv = jnp.swapaxes(w_in[..., :QKV_W].reshape(DEPTH, D_MODEL, 3, 3, GROUP_W), 2, 3)
    w_in_p = (bf(w_qkv.reshape(DEPTH, D_MODEL, QKV_W)),
              bf(jnp.pad(w_in[..., QKV_W:mid_hi], ((0, 0), (0, 0), (0, DT_PAD - SSD_HEADS)))),
              bf(w_in[..., mid_hi:]))
    wba, wbs, wglu, wout = bf(w_branch_attn), bf(w_branch_ssd), bf(w_s5_glu), bf(w_out)
    wpg, wpp = bf(w_pe_gate), bf(w_pe_proj)
    n1, nm, n2, npe = vec(ffn1_norm), vec(mix_norm), vec(ffn2_norm), vec(pe_norm)
    fin = final_norm.reshape(1, D_MODEL)
    cb = vec(ssd_conv_b)
    lane_pad = lambda a: jnp.pad(a, ((0, 0), (0, DT_PAD - a.shape[-1]))).reshape(DEPTH, 1, DT_PAD)
    dtb, alog = lane_pad(ssd_dt_bias), lane_pad(ssd_a_log)
    dsk = jnp.repeat(ssd_d, SSD_P, axis=-1).reshape(DEPTH, 1, SSD_INNER)
    ng = vec(ssd_norm)
    per_head = lambda a: jnp.broadcast_to(a.reshape(DEPTH * SSD_HEADS, 1, 1), (DEPTH * SSD_HEADS, 1, LANES))
    alog_h, dsk_h = per_head(ssd_a_log), per_head(ssd_d)

    tw, v_blk, v0_blk, scan, d_blk = _s5_tables(s5_a_re, s5_a_im, s5_log_dt, s5_b_re, s5_b_im,
                                                s5_c_re, s5_c_im, s5_d)

    bias_p = [_prompt_bias(attn_rel_bias, g, d) for g, (_, d) in enumerate(ATTN_PATTERNS)]
    bias_s = [_sample_bias(attn_rel_bias, g, d) for g, (_, d) in enumerate(ATTN_PATTERNS)]

    pp = p_prompt.reshape(DEPTH, m_p, PE_DIM)
    psm = p_sample.reshape(DEPTH, n_s, PE_DIM)
    st_ssd = state_ssd.reshape(DEPTH, n_s, SSD_HEADS * SSD_P * SSD_S)
    st_s5 = state_s5.reshape(DEPTH, n_s, 2 * S5_GROUPS * S5_STATE)

    xp = x_prompt.reshape(m_p, D_MODEL)
    xs = x_sample.reshape(n_s, D_MODEL)
    tm_p, tm_f, tm_s = 256, 512, n_s
    kv_p, kv_s = [[], [], []], [[], [], []]
    ssd_p, ssd_s, conv_p, conv_s, s5_p, s5_s = [], [], [], [], [], []

    def kv_prompt(qg, d):
        n, _, rows, _ = qg.shape
        t = qg[:, :, rows - QBLK:, GROUP_W:].reshape(n, d, QBLK, 2, HPG, HEAD_DIM)
        return jnp.transpose(t, (0, 3, 2, 1, 4, 5)).reshape(n, 2, QBLK * d, HPG, HEAD_DIM)

    for i in range(DEPTH):
        last = i == DEPTH - 1
        xp = _ffn(xp, i, tm_f, n1, w1g, w1u, w1d)
        *qgs, z, xbc, dt, u, gates = _inproj(xp, i, tm_p, nm, w_in_p, n_p, len_p, dils, True)
        parts = []
        for g, d in enumerate(dils):
            parts += list(_attn_prompt(qgs[g], bias_p[g], g))
            kv_p[g].append(kv_prompt(qgs[g], d))
        y_ssd, st = _ssd_prompt(xbc, z, dt, i, n_p, len_p, ssd_conv_w, cb, dtb, alog, dsk, ng)
        ssd_p.append(jnp.swapaxes(st, 2, 3))
        conv_p.append(xbc.reshape(n_p, len_p, SSD_CONV_CH)[:, len_p - 3:])
        y_s5, st5 = _s5_prompt(u, i, n_p, len_p, tw, v_blk, scan, d_blk)
        s5_p.append(st5)
        xp = _merge(xp, i, tm_f, parts, dils, len_p, y_ssd.reshape(m_p, SSD_INNER), y_s5, gates,
                    wba, wbs, wglu, wout)
        xp = _ffn(xp, i, tm_f, n2, w2g, w2u, w2d, pe=(pp, npe, wpg, wpp), final=fin if last else None)
        xs = _ffn(xs, i, tm_s, n1, w1g, w1u, w1d)
        *qgs, z, xbc, dt, u, gates = _inproj(xs, i, tm_s, nm, w_in_p, 1, n_s, (1, 1, 1), False)
        parts = []
        for g in range(len(dils)):
            qg = qgs[g].reshape(n_s, ATTN_W)
            o_s, l_s = _attn_sample(qg, caches_t[g], i, bias_s[g], g, sample_bs[g])
            parts += [o_s.reshape(1, 1, n_s, GROUP_W), l_s.reshape(1, 1, n_s, GROUP_W)]
            kv_s[g].append(qg[:, GROUP_W:].reshape(n_s, 2, 1, HPG, HEAD_DIM))
        xc, dts = _ssd_conv_sample(xbc, state_conv, dt, i, ssd_conv_w, cb, dtb)
        conv_s.append(jnp.concatenate([state_conv[i][:, 1:], xbc[:, None, :]], axis=1))
        xh = jnp.transpose(xc.reshape(n_s, SSD_CONV_CH // SSD_S, SSD_S), (1, 0, 2))
        dth = jnp.broadcast_to(jnp.transpose(dts[:, :SSD_HEADS])[:, :, None], (SSD_HEADS, n_s, LANES))
        yh, hn = _ssd_state_sample(st_ssd, i, xh, dth, alog_h, dsk_h)
        ssd_s.append(hn.reshape(n_s, SSD_HEADS, SSD_P, SSD_S))
        y_ssd = _ssd_gate_sample(jnp.transpose(yh, (1, 0, 2)).reshape(n_s, SSD_INNER), z, i, ng)
        y_s5, st5 = _s5_sample(u, st_s5, i, tw, v0_blk, scan, d_blk)
        s5_s.append(st5)
        xs = _merge(xs, i, tm_s, parts, (1, 1, 1), n_s, y_ssd, y_s5, gates, wba, wbs, wglu, wout)
        xs = _ffn(xs, i, tm_s, n2, w2g, w2u, w2d, pe=(psm, npe, wpg, wpp), final=fin if last else None)

    return (xp.reshape(n_p, len_p, D_MODEL), xs.reshape(n_s, 1, D_MODEL),
            jnp.stack(kv_p[0], 0), jnp.stack(kv_s[0], 0),
            jnp.stack(kv_p[1], 0), jnp.stack(kv_s[1], 0),
            jnp.stack(kv_p[2], 0), jnp.stack(kv_s[2], 0),
            jnp.stack(ssd_p, 0), jnp.stack(ssd_s, 0),
            jnp.stack(conv_p, 0), jnp.stack(conv_s, 0),
            jnp.stack(s5_p, 0), jnp.stack(s5_s, 0))
```

```python
import functools
import math

import numpy as np
import jax
import jax.numpy as jnp
from jax import lax
from jax.experimental import pallas as pl
from jax.experimental.pallas import tpu as pltpu

F32 = jnp.float32
BF16 = jnp.bfloat16

D_MODEL = 1024
DEPTH = 4
HEAD_DIM = 64
ATTN_PATTERNS = ((128, 1), (512, 4), (2048, 16))
HPG = 4
GROUP_W = HPG * HEAD_DIM
ATTN_W = 3 * GROUP_W
QBLK = 128
N_BUCKETS = 32
BUCKET_MAX_DIST = 2048
NEG_INF = -1e30
SSD_HEADS = 12
SSD_P = 64
SSD_S = 64
SSD_GROUPS = 4
SSD_INNER = SSD_HEADS * SSD_P
SSD_CONV_CH = SSD_INNER + 2 * SSD_GROUPS * SSD_S
SSD_CHUNK = 256
S5_GROUPS = 48
S5_CH = 16
S5_STATE = 64
S5_W = S5_GROUPS * S5_CH
S5_TC = 8
S5_GB = 6
S5_BLK = 8 * S5_STATE
D_FF = 2816
PE_DIM = 256
RMS_EPS = 1e-6
LANES = 128
DT_PAD = LANES
QKV_W = 3 * ATTN_W
MID_XBC = SSD_INNER
MID_DT = MID_XBC + SSD_CONV_CH
MID_W = MID_DT + DT_PAD
REST_G = S5_W
REST_W = REST_G + 3 * D_MODEL
VMEM_LIMIT = 56 * 1024 * 1024


def _params(*sem):
    return pltpu.CompilerParams(dimension_semantics=sem, vmem_limit_bytes=VMEM_LIMIT)


def _rms(x, g):
    inv = lax.rsqrt(jnp.mean(x * x, axis=-1, keepdims=True) + RMS_EPS)
    return (x * inv) * g


def _dot(a, b):
    return jnp.dot(a, b, preferred_element_type=F32)


def _dot_nt(a, b):
    return lax.dot_general(a, b, (((1,), (1,)), ((), ())), preferred_element_type=F32)


def _dot_tn(a, b):
    return lax.dot_general(a, b, (((0,), (0,)), ((), ())), preferred_element_type=F32)


def _sigmoid(x):
    return jax.nn.sigmoid(x)


def _layer_spec(arr, layer):
    nd = arr.ndim - 1
    return pl.BlockSpec((None,) + tuple(arr.shape[1:]),
                        lambda *_: (layer,) + (0,) * nd,
                        pipeline_mode=pl.Buffered(1))


def _full_spec(arr):
    nd = arr.ndim
    return pl.BlockSpec(tuple(arr.shape), lambda *_: (0,) * nd, pipeline_mode=pl.Buffered(1))


def _ffn_body(*refs, has_pe, has_final):
    it = iter(refs)
    x_ref, g_ref, wg_ref, wu_ref, wd_ref = (next(it) for _ in range(5))
    if has_pe:
        p_ref, pn_ref, wpg_ref, wpp_ref = (next(it) for _ in range(4))
    if has_final:
        fn_ref = next(it)
    o_ref = next(it)
    act_ref = next(it)
    x = x_ref[...]
    h = _rms(x, g_ref[...]).astype(BF16)
    half = D_FF // 2
    for f0 in (0, half):
        gate = _dot(h, wg_ref[:, f0:f0 + half])
        up = _dot(h, wu_ref[:, f0:f0 + half])
        act_ref[:, f0:f0 + half] = (gate * _sigmoid(gate) * up).astype(BF16)
    y = x + 0.5 * _dot(act_ref[...], wd_ref[...])
    if has_pe:
        h2 = _rms(y, pn_ref[...]).astype(BF16)
        gt = _sigmoid(_dot(h2, wpg_ref[...]))
        y = y + gt * _dot(p_ref[...].astype(BF16), wpp_ref[...])
    if has_final:
        y = _rms(y, fn_ref[...])
    o_ref[...] = y


def _ffn(x, layer, tm, norm, wg, wu, wd, pe=None, final=None):
    m = x.shape[0]
    row = lambda i: (i, 0)
    args = [x, norm, wg, wu, wd]
    specs = [pl.BlockSpec((tm, D_MODEL), row), _layer_spec(norm, layer), _layer_spec(wg, layer),
             _layer_spec(wu, layer), _layer_spec(wd, layer)]
    if pe is not None:
        p_all, pn, wpg, wpp = pe
        args += [p_all, pn, wpg, wpp]
        specs += [pl.BlockSpec((None, tm, PE_DIM), lambda i: (layer, i, 0)), _layer_spec(pn, layer),
                  _layer_spec(wpg, layer), _layer_spec(wpp, layer)]
    if final is not None:
        args.append(final)
        specs.append(_full_spec(final))
    return pl.pallas_call(
        functools.partial(_ffn_body, has_pe=pe is not None, has_final=final is not None),
        grid=(m // tm,),
        in_specs=specs,
        out_specs=pl.BlockSpec((tm, D_MODEL), row),
        out_shape=jax.ShapeDtypeStruct((m, D_MODEL), F32),
        scratch_shapes=[pltpu.VMEM((tm, D_FF), BF16)],
        compiler_params=_params("arbitrary"),
        name="ffn",
    )(*args)


def _inproj_body(x_ref, g_ref, wq_ref, wm_ref, wr_ref, qa_ref, qb_ref, qc_ref, z_ref, xbc_ref, dt_ref, u_ref,
                 gates_ref, stage_ref, *, dils, prompt):
    h = _rms(x_ref[...], g_ref[...]).astype(BF16)
    tm = x_ref.shape[0]
    for g, (out_ref, d) in enumerate(zip((qa_ref, qb_ref, qc_ref), dils)):
        res = _dot(h, wq_ref[:, g * ATTN_W:(g + 1) * ATTN_W])
        if d == 1:
            out_ref[0] = res
        else:
            for c in range(ATTN_W // LANES):
                stage_ref[c] = res[:, c * LANES:(c + 1) * LANES]
            for r in range(d):
                for c in range(ATTN_W // LANES):
                    out_ref[r, :, c * LANES:(c + 1) * LANES] = stage_ref[c, pl.ds(r, tm // d, stride=d), :]
    z_ref[...] = _dot(h, wm_ref[:, 0:MID_XBC]).astype(BF16)
    xbc_ref[...] = _dot(h, wm_ref[:, MID_XBC:MID_DT])
    dt_ref[...] = _dot(h, wm_ref[:, MID_DT:MID_W])
    u = _dot(h, wr_ref[:, 0:REST_G])
    if prompt:
        for c in range(S5_W // LANES):
            stage_ref[c] = u[:, c * LANES:(c + 1) * LANES]
        for j in range(S5_TC):
            for c in range(S5_W // LANES):
                u_ref[j, :, c * LANES:(c + 1) * LANES] = (
                    stage_ref[c, pl.ds(j, tm // S5_TC, stride=S5_TC), :].astype(BF16))
    else:
        u_ref[...] = u.astype(BF16)
    for k in range(3):
        c0 = REST_G + k * D_MODEL
        gates_ref[:, k * D_MODEL:(k + 1) * D_MODEL] = _sigmoid(_dot(h, wr_ref[:, c0:c0 + D_MODEL])).astype(BF16)


def _inproj(x, layer, tm, norm, w, n, length, dils, prompt):
    m = x.shape[0]
    tps = length // tm
    row = lambda i: (i, 0)
    q_specs = [pl.BlockSpec((None, d, tm // d, ATTN_W), lambda i: (i // tps, 0, i % tps, 0)) for d in dils]
    q_shapes = [jax.ShapeDtypeStruct((n, d, length // d, ATTN_W), F32) for d in dils]
    flat = lambda wd, dt: (pl.BlockSpec((tm, wd), row), jax.ShapeDtypeStruct((m, wd), dt))
    if prompt:
        u_out = (pl.BlockSpec((S5_TC, tm // S5_TC, S5_W), lambda i: (0, i, 0)),
                 jax.ShapeDtypeStruct((S5_TC, m // S5_TC, S5_W), BF16))
    else:
        u_out = flat(S5_W, BF16)
    rest = [flat(SSD_INNER, BF16), flat(SSD_CONV_CH, F32), flat(DT_PAD, F32), u_out, flat(3 * D_MODEL, BF16)]
    return pl.pallas_call(
        functools.partial(_inproj_body, dils=dils, prompt=prompt),
        grid=(m // tm,),
        in_specs=[pl.BlockSpec((tm, D_MODEL), row), _layer_spec(norm, layer)] + [_layer_spec(a, layer) for a in w],
        out_specs=q_specs + [s for s, _ in rest],
        out_shape=q_shapes + [s for _, s in rest],
        scratch_shapes=[pltpu.VMEM((ATTN_W // LANES, tm, LANES), F32)],
        compiler_params=_params("arbitrary"),
        name="inproj",
    )(x, norm, *w)


def _merge_body(x_ref, o0, l0, o1, l1, o2, l2, yssd_ref, ys5_ref, gates_ref,
                wba_ref, wbs_ref, wglu_ref, wout_ref, out_ref, stage_ref, s5_stage_ref, *, dils, s5_chunked):
    tm = x_ref.shape[0]
    parts = []
    for k, (ref, d) in enumerate(zip((o0, l0, o1, l1, o2, l2), (dils[0], dils[0], dils[1], dils[1], dils[2], dils[2]))):
        if d == 1:
            parts.append(ref[0])
        else:
            nl = GROUP_W // LANES
            for r in range(d):
                for c in range(nl):
                    stage_ref[k * nl + c, pl.ds(r, tm // d, stride=d), :] = ref[r, :, c * LANES:(c + 1) * LANES]
            parts.append(jnp.concatenate([stage_ref[k * nl + c] for c in range(nl)], axis=1))
    oa, la, ob, lb, oc, lc = parts
    mx = jnp.maximum(jnp.maximum(la, lb), lc)
    ea, eb, ec = jnp.exp(la - mx), jnp.exp(lb - mx), jnp.exp(lc - mx)
    attn = (ea * oa + eb * ob + ec * oc) / (ea + eb + ec)
    acc = gates_ref[:, 0:D_MODEL].astype(F32) * _dot(attn.astype(BF16), wba_ref[...])
    acc = acc + gates_ref[:, D_MODEL:2 * D_MODEL].astype(F32) * _dot(yssd_ref[...], wbs_ref[...])
    if s5_chunked:
        nl6 = S5_W // LANES
        for j in range(S5_TC):
            for c in range(nl6):
                s5_stage_ref[c, pl.ds(j, tm // S5_TC, stride=S5_TC), :] = (
                    ys5_ref[j, :, c * LANES:(c + 1) * LANES].astype(F32))
        ys5 = jnp.concatenate([s5_stage_ref[c] for c in range(nl6)], axis=1)
    else:
        ys5 = ys5_ref[...].astype(F32)
    gl = _dot(jax.nn.gelu(ys5).astype(BF16), wglu_ref[...])
    s5 = gl[:, :D_MODEL] * _sigmoid(gl[:, D_MODEL:])
    acc = acc + gates_ref[:, 2 * D_MODEL:].astype(F32) * s5
    out_ref[...] = x_ref[...] + _dot(acc.astype(BF16), wout_ref[...])


def _merge(x, layer, tm, attn_parts, dils, length, y_ssd, y_s5, gates, wba, wbs, wglu, wout):
    m = x.shape[0]
    tps = length // tm
    row = lambda i: (i, 0)
    s5_chunked = y_s5.ndim == 3
    args = [x] + list(attn_parts) + [y_ssd, y_s5, gates, wba, wbs, wglu, wout]
    part_specs = []
    for d in dils:
        part_specs += [pl.BlockSpec((None, d, tm // d, GROUP_W), lambda i: (i // tps, 0, i % tps, 0))] * 2
    s5_spec = (pl.BlockSpec((S5_TC, tm // S5_TC, S5_W), lambda i: (0, i, 0)) if s5_chunked
               else pl.BlockSpec((tm, S5_W), row))
    specs = ([pl.BlockSpec((tm, D_MODEL), row)] + part_specs
             + [pl.BlockSpec((tm, SSD_INNER), row), s5_spec, pl.BlockSpec((tm, 3 * D_MODEL), row)]
             + [_layer_spec(w, layer) for w in (wba, wbs, wglu, wout)])
    return pl.pallas_call(
        functools.partial(_merge_body, dils=dils, s5_chunked=s5_chunked),
        grid=(m // tm,),
        in_specs=specs,
        out_specs=pl.BlockSpec((tm, D_MODEL), row),
        out_shape=jax.ShapeDtypeStruct((m, D_MODEL), F32),
        scratch_shapes=[pltpu.VMEM((6 * GROUP_W // LANES, tm, LANES), F32),
                        pltpu.VMEM((S5_W // LANES, tm, LANES), F32)],
        compiler_params=_params("arbitrary"),
        name="merge",
    )(*args)


def _t5_bucket(dist):
    max_exact = N_BUCKETS // 2
    d = np.asarray(dist).astype(np.int32)
    df = np.maximum(d, 1).astype(np.float32)
    large = max_exact + (np.log(df / max_exact) / math.log(BUCKET_MAX_DIST / max_exact)
                         * (N_BUCKETS - max_exact)).astype(np.int32)
    return np.where(d < max_exact, d, np.minimum(large, N_BUCKETS - 1)).astype(np.int32)


def _prompt_bias(rel_bias, g, d):
    w = 2 * QBLK
    tab = rel_bias[:, g * HPG:(g + 1) * HPG].astype(F32)
    near = jnp.transpose(tab[_t5_bucket(np.arange(QBLK, -1, -1) * d)], (1, 0))
    row0 = jnp.concatenate([near, jnp.full((HPG, w - QBLK - 1), NEG_INF, F32)], axis=1)
    x = jnp.concatenate([row0, row0, jnp.full((HPG, 1), NEG_INF, F32)], axis=1)
    flat = jnp.tile(x, (1, QBLK))[:, :QBLK * 2 * w]
    return flat.reshape(HPG, QBLK, 2 * w)[:, :, w:]


def _attn_prompt_body(q_ref, kp_ref, kc_ref, vp_ref, vc_ref, bias_ref, o_ref, l_ref):
    nsub = q_ref.shape[0] // QBLK
    first = pl.program_id(2) == 0
    lane = lax.broadcasted_iota(jnp.int32, (1, 2 * QBLK), 1)
    pen = jnp.where(jnp.logical_and(first, lane < QBLK), NEG_INF, 0.0).astype(F32)
    q = (q_ref[...] * (HEAD_DIM ** -0.5)).astype(BF16)
    k = jnp.concatenate([kp_ref[...], kc_ref[...]], axis=0).astype(BF16)
    v = jnp.concatenate([vp_ref[...], vc_ref[...]], axis=0).astype(BF16)
    for s in range(nsub):
        rq = slice(s * QBLK, (s + 1) * QBLK)
        rk = slice(s * QBLK, (s + 2) * QBLK)
        for h in range(HPG):
            sl = slice(h * HEAD_DIM, (h + 1) * HEAD_DIM)
            sc = _dot_nt(q[rq, sl], k[rk, sl]) + bias_ref[h]
            if s == 0:
                sc = sc + pen
            m = jnp.max(sc, axis=-1, keepdims=True)
            e = jnp.exp(sc - m)
            den = jnp.sum(e, axis=-1, keepdims=True)
            o_ref[rq, sl] = _dot(e.astype(BF16), v[rk, sl]) / den
            l_ref[rq, sl] = jnp.broadcast_to(m + jnp.log(den), (QBLK, HEAD_DIM))


def _attn_prompt(qkv_g, bias, g, tq=512):
    n, d, rows, _ = qkv_g.shape
    tq = min(tq, rows)
    nsub = tq // QBLK
    nb = rows // tq
    blk = (None, None, tq, GROUP_W)
    cur = lambda col: pl.BlockSpec(blk, lambda b, r, i: (b, r, i, col))
    prev = lambda col: pl.BlockSpec((None, None, QBLK, GROUP_W),
                                    lambda b, r, i: (b, r, jnp.maximum(i * nsub - 1, 0), col))
    out_spec = pl.BlockSpec(blk, lambda b, r, i: (b, r, i, 0))
    out_sds = jax.ShapeDtypeStruct((n, d, rows, GROUP_W), F32)
    return pl.pallas_call(
        _attn_prompt_body,
        grid=(n, d, nb),
        in_specs=[cur(0), prev(1), cur(1), prev(2), cur(2), _full_spec(bias)],
        out_specs=[out_spec, out_spec],
        out_shape=[out_sds, out_sds],
        compiler_params=_params("arbitrary", "arbitrary", "arbitrary"),
        name=f"attn_prompt_g{g}",
    )(qkv_g, qkv_g, qkv_g, qkv_g, qkv_g, bias)


def _sample_bias(rel_bias, g, d):
    steps = QBLK - np.arange(QBLK)
    tab = rel_bias[:, g * HPG:(g + 1) * HPG].astype(F32)
    on_grid = jnp.transpose(tab[_t5_bucket(steps * d)], (1, 0))[:, :, None]
    off_grid = jnp.full((HPG, QBLK, d - 1), NEG_INF, F32)
    cache_b = jnp.concatenate([on_grid, off_grid], axis=2).reshape(HPG, QBLK * d)
    self_b = jnp.broadcast_to(tab[0][:, None], (HPG, LANES))
    pad = lambda a: jnp.concatenate([a, jnp.zeros((8 - HPG, a.shape[1]), F32)], 0)
    return pad(cache_b), pad(self_b)


def _attn_sample_body(q_ref, kv_ref, cb_ref, sb_ref, o_ref, l_ref, *, bs):
    base = pl.program_id(0) * bs
    rowi = lax.broadcasted_iota(jnp.int32, (8, GROUP_W), 0)
    lane = lax.broadcasted_iota(jnp.int32, (8, GROUP_W), 1)
    own = (lane // HEAD_DIM) == rowi
    p = kv_ref.shape[-1]
    for j in range(bs):
        r = pl.ds(base + j, 1)
        q = q_ref[r, 0:GROUP_W] * (HEAD_DIM ** -0.5)
        kn = q_ref[r, GROUP_W:2 * GROUP_W]
        vn = q_ref[r, 2 * GROUP_W:3 * GROUP_W]
        qblk = jnp.where(own, jnp.broadcast_to(q, (8, GROUP_W)), 0.0)
        kt = kv_ref[j, 0].reshape(GROUP_W, p).astype(BF16)
        vt = kv_ref[j, 1].reshape(GROUP_W, p).astype(BF16)
        s = _dot(qblk.astype(BF16), kt) + cb_ref[...]
        s_self = jnp.sum(qblk * kn, axis=-1, keepdims=True) + sb_ref[:, 0:1]
        m = jnp.maximum(jnp.max(s, axis=-1, keepdims=True), s_self)
        e = jnp.exp(s - m)
        e_self = jnp.exp(s_self - m)
        den = jnp.sum(e, axis=-1, keepdims=True) + e_self
        o = (_dot_nt(e.astype(BF16), vt) + e_self * vn) / den
        o_ref[r, :] = jnp.sum(jnp.where(own, o, 0.0), axis=0, keepdims=True)
        lse = m + jnp.log(den)
        l_ref[r, :] = jnp.sum(jnp.where(own, lse, 0.0), axis=0, keepdims=True)


def _attn_sample(qkv_g, cache_t, layer, biases, g, bs):
    n = qkv_g.shape[0]
    p = cache_t.shape[-1]
    cb, sb = biases
    out_sds = jax.ShapeDtypeStruct((n, GROUP_W), F32)
    return pl.pallas_call(
        functools.partial(_attn_sample_body, bs=bs),
        grid=(n // bs,),
        in_specs=[_full_spec(qkv_g),
                  pl.BlockSpec((None, bs, 2, HPG, HEAD_DIM, p), lambda s: (layer, s, 0, 0, 0, 0)),
                  _full_spec(cb), _full_spec(sb)],
        out_specs=[pl.BlockSpec((n, GROUP_W), lambda s: (0, 0))] * 2,
        out_shape=[out_sds, out_sds],
        compiler_params=_params("arbitrary"),
        name=f"attn_sample_g{g}",
    )(qkv_g, cache_t, cb, sb)


def _softplus(x):
    return jnp.maximum(x, 0.0) + jnp.log(1.0 + jnp.exp(-jnp.abs(x)))


def _split3(x):
    hi = x.astype(BF16)
    r1 = x - hi.astype(F32)
    mid = r1.astype(BF16)
    lo = (r1 - mid.astype(F32)).astype(BF16)
    return hi, mid, lo


def _ssd_prompt_body(xbc_ref, z_ref, dt_ref, cw_ref, cb_ref, dtb_ref, alog_ref, dsk_ref, ng_ref,
                     y_ref, st_ref, xs_ref, state_ref, ybuf_ref):
    @pl.when(pl.program_id(1) == 0)
    def _():
        state_ref[...] = jnp.zeros_like(state_ref)
        xs_ref[:, 0:8, :] = jnp.zeros((xs_ref.shape[0], 8, SSD_CONV_CH), F32)

    for b in range(xbc_ref.shape[0]):
        _ssd_chunk(xbc_ref.at[b], z_ref.at[b], dt_ref.at[b], cw_ref, cb_ref, dtb_ref, alog_ref, dsk_ref, ng_ref,
                   y_ref.at[b], xs_ref.at[b], state_ref.at[b], ybuf_ref.at[b])
    st_ref[...] = state_ref[...]


def _ssd_chunk(xbc_ref, z_ref, dt_ref, cw_ref, cb_ref, dtb_ref, alog_ref, dsk_ref, ng_ref,
               y_ref, xs_ref, state_ref, ybuf_ref):
    q = SSD_CHUNK
    xs_ref[8:8 + q, :] = xbc_ref[...]
    full = xs_ref[...]
    conv = cb_ref[...] + cw_ref[3:4, :] * full[8:8 + q, :]
    for s in range(1, 4):
        conv = conv + cw_ref[3 - s:4 - s, :] * pltpu.roll(full, s, axis=0)[8:8 + q, :]
    xs_ref[0:8, :] = full[q:q + 8, :]
    xc = conv * _sigmoid(conv)

    dt = _softplus(dt_ref[...] + dtb_ref[...])
    a = -jnp.exp(alog_ref[...])
    da = dt * a
    ri = lax.broadcasted_iota(jnp.int32, (q, q), 0)
    ci = lax.broadcasted_iota(jnp.int32, (q, q), 1)
    causal = ri >= ci
    tril = jnp.where(causal, 1.0, 0.0).astype(BF16)
    cum = sum(_dot(tril, part) for part in _split3(da))
    cum_t = cum.T
    dt_t = dt.T
    cum_last = cum[q - 1:q, :]
    w_end = jnp.exp(cum_last - cum) * dt
    e_cum = jnp.exp(cum)
    e_last = jnp.exp(cum_last)

    for g in range(SSD_GROUPS):
        bgt = xc[:, SSD_INNER + g * SSD_S:SSD_INNER + (g + 1) * SSD_S].T.astype(BF16)
        cg = xc[:, SSD_INNER + (SSD_GROUPS + g) * SSD_S:SSD_INNER + (SSD_GROUPS + g + 1) * SSD_S].astype(BF16)
        gmat = _dot(cg, bgt)
        for hh in range(SSD_HEADS // SSD_GROUPS):
            h = g * (SSD_HEADS // SSD_GROUPS) + hh
            sl = slice(h * SSD_P, (h + 1) * SSD_P)
            diff = cum[:, h:h + 1] - cum_t[h:h + 1, :]
            decay = jnp.exp(jnp.where(causal, diff, NEG_INF))
            scores = gmat * decay * dt_t[h:h + 1, :]
            xh = xc[:, sl]
            hin = state_ref[h]
            y = _dot(scores.astype(BF16), xh.astype(BF16))
            y = y + _dot(cg, hin.astype(BF16)) * e_cum[:, h:h + 1]
            y = y + dsk_ref[:, sl] * xh
            st = _dot(bgt, (xh * w_end[:, h:h + 1]).astype(BF16))
            state_ref[h] = e_last[:, h:h + 1] * hin + st
            ybuf_ref[:, sl] = y

    zf = z_ref[...].astype(F32)
    yg = ybuf_ref[...] * (zf * _sigmoid(zf))
    y_ref[...] = _rms(yg, ng_ref[...]).astype(BF16)


def _ssd_prompt(xbc, z, dt, layer, n, length, cw, cb, dtb, alog, dsk, ng):
    nc = length // SSD_CHUNK
    nb = 1
    blk = lambda w: pl.BlockSpec((nb, SSD_CHUNK, w), lambda b, c: (b, c, 0))
    return pl.pallas_call(
        _ssd_prompt_body,
        grid=(n // nb, nc),
        in_specs=[blk(SSD_CONV_CH), blk(SSD_INNER), blk(DT_PAD)]
                 + [_layer_spec(w, layer) for w in (cw, cb, dtb, alog, dsk, ng)],
        out_specs=[blk(SSD_INNER),
                   pl.BlockSpec((nb, SSD_HEADS, SSD_P, SSD_S), lambda b, c: (b, 0, 0, 0))],
        out_shape=[jax.ShapeDtypeStruct((n, length, SSD_INNER), BF16),
                   jax.ShapeDtypeStruct((n, SSD_HEADS, SSD_P, SSD_S), F32)],
        scratch_shapes=[pltpu.VMEM((nb, SSD_CHUNK + 8, SSD_CONV_CH), F32),
                        pltpu.VMEM((nb, SSD_HEADS, SSD_P, SSD_S), F32),
                        pltpu.VMEM((nb, SSD_CHUNK, SSD_INNER), F32)],
        compiler_params=_params("arbitrary", "arbitrary"),
        name="ssd_prompt",
    )(xbc.reshape(n, length, SSD_CONV_CH), z.reshape(n, length, SSD_INNER),
      dt.reshape(n, length, DT_PAD), cw, cb, dtb, alog, dsk, ng)


def _ssd_conv_sample_body(xbc_ref, cs_ref, dt_ref, cw_ref, cb_ref, dtb_ref, xc_ref, dts_ref):
    conv = cb_ref[...] + cw_ref[3:4, :] * xbc_ref[...]
    for k in range(3):
        conv = conv + cw_ref[k:k + 1, :] * cs_ref[:, k * SSD_CONV_CH:(k + 1) * SSD_CONV_CH]
    xc_ref[...] = conv * _sigmoid(conv)
    dts_ref[...] = _softplus(dt_ref[...] + dtb_ref[...])


def _ssd_conv_sample(xbc, conv_state, dt, layer, cw, cb, dtb):
    n = xbc.shape[0]
    cs = conv_state.reshape(DEPTH, n, 3 * SSD_CONV_CH)
    return pl.pallas_call(
        _ssd_conv_sample_body,
        grid=(1,),
        in_specs=[_full_spec(xbc), _layer_spec(cs, layer), _full_spec(dt)]
                 + [_layer_spec(w, layer) for w in (cw, cb, dtb)],
        out_specs=[pl.BlockSpec((n, SSD_CONV_CH), lambda i: (0, 0)),
                   pl.BlockSpec((n, DT_PAD), lambda i: (0, 0))],
        out_shape=[jax.ShapeDtypeStruct((n, SSD_CONV_CH), F32), jax.ShapeDtypeStruct((n, DT_PAD), F32)],
        compiler_params=_params("arbitrary"),
        name="ssd_conv_sample",
    )(xbc, cs, dt, cw, cb, dtb)


def _ssd_state_sample_body(h0_ref, x_ref, b_ref, c_ref, dt_ref, alog_ref, dsk_ref, y_ref, hn_ref):
    h0 = h0_ref[...]
    x, bv, cv, dt = x_ref[...], b_ref[...], c_ref[...], dt_ref[...]
    dec = jnp.exp(dt * (-jnp.exp(alog_ref[...])))
    y_off = jnp.sum(h0 * cv[None], axis=1, keepdims=True) * dec
    cb = jnp.sum(cv * bv, axis=0, keepdims=True)
    y_ref[...] = (cb * dt) * x + y_off + dsk_ref[...] * x
    hn_ref[...] = dec * h0 + (dt * x) * bv[None]


def _ssd_state_sample(state_t, layer, x3, bc, dtt, alog_h, dsk_h):
    n = x3.shape[-1]
    rep = SSD_HEADS // SSD_GROUPS
    lanes = lambda: pl.BlockSpec((None, 1, n), lambda h: (layer * SSD_HEADS + h, 0, 0))
    st_blk = (None, None, SSD_P, SSD_S, n)
    return pl.pallas_call(
        _ssd_state_sample_body,
        grid=(SSD_HEADS,),
        in_specs=[pl.BlockSpec(st_blk, lambda h: (layer, h, 0, 0, 0)),
                  pl.BlockSpec((None, SSD_P, 1, n), lambda h: (h, 0, 0, 0)),
                  pl.BlockSpec((None, SSD_S, n), lambda h: (h // rep, 0, 0)),
                  pl.BlockSpec((None, SSD_S, n), lambda h: (SSD_GROUPS + h // rep, 0, 0)),
                  pl.BlockSpec((None, 1, n), lambda h: (h, 0, 0)), lanes(), lanes()],
        out_specs=[pl.BlockSpec((None, SSD_P, 1, n), lambda h: (h, 0, 0, 0)),
                   pl.BlockSpec((None, SSD_P, SSD_S, n), lambda h: (h, 0, 0, 0))],
        out_shape=[jax.ShapeDtypeStruct((SSD_HEADS, SSD_P, 1, n), F32),
                   jax.ShapeDtypeStruct((SSD_HEADS, SSD_P, SSD_S, n), F32)],
        compiler_params=_params("arbitrary"),
        name="ssd_state_sample",
    )(state_t, x3, bc, bc, dtt, alog_h, dsk_h)


def _ssd_gate_sample_body(y_ref, z_ref, ng_ref, o_ref):
    zf = z_ref[...].astype(F32)
    o_ref[...] = _rms(y_ref[...] * (zf * _sigmoid(zf)), ng_ref[...]).astype(BF16)


def _ssd_gate_sample(y, z, layer, ng):
    n = y.shape[0]
    return pl.pallas_call(
        _ssd_gate_sample_body,
        grid=(1,),
        in_specs=[_full_spec(y), _full_spec(z), _layer_spec(ng, layer)],
        out_specs=pl.BlockSpec((n, SSD_INNER), lambda i: (0, 0)),
        out_shape=jax.ShapeDtypeStruct((n, SSD_INNER), BF16),
        compiler_params=_params("arbitrary"),
        name="ssd_gate_sample",
    )(y, z, ng)


def _cmul(a, b):
    return a[0] * b[0] - a[1] * b[1], a[0] * b[1] + a[1] * b[0]


def _s5_discretise(lr, li, dl):
    delta = jnp.exp(dl)
    mag = jnp.exp(lr * delta)
    ab = (mag * jnp.cos(li * delta), mag * jnp.sin(li * delta))
    nr, ni = ab[0] - 1.0, ab[1]
    den = lr * lr + li * li
    return ab, ((nr * lr + ni * li) / den, (ni * lr - nr * li) / den)


def _s5_powers(ab, n):
    pw = [(jnp.ones_like(ab[0]), jnp.zeros_like(ab[0]))]
    for _ in range(n):
        pw.append(_cmul(pw[-1], ab))
    return pw


def _s5_tables_body(lr_r, li_r, dl_r, btr_r, bti_r, cr_r, ci_r, lr_l, li_l, dl_l,
                    lr_t, li_t, dl_t, cr_t, ci_t, tw_ref, v_ref, v0_ref, sc_ref):
    tc, nl = S5_TC, LANES
    ab, f = _s5_discretise(lr_r[...], li_r[...], dl_r[...])
    bb = _cmul(f, (btr_r[...], bti_r[...]))
    pw = _s5_powers(ab, tc - 1)
    cc = (cr_r[...], ci_r[...])
    bbr, bbi = bb[0].astype(BF16), bb[1].astype(BF16)
    rg = lax.broadcasted_iota(jnp.int32, (nl, nl), 0) // S5_CH
    cg = lax.broadcasted_iota(jnp.int32, (nl, nl), 1) // S5_CH
    kblk = []
    for tau in range(tc):
        cp = _cmul(cc, pw[tau])
        k = _dot_nt(bbr, cp[0].astype(BF16)) - _dot_nt(bbi, cp[1].astype(BF16))
        kblk.append(jnp.where(rg == cg, k, 0.0).astype(BF16))
    zero = jnp.zeros((nl, nl), BF16)
    for j in range(tc):
        for i in range(tc):
            tw_ref[j * nl:(j + 1) * nl, i * nl:(i + 1) * nl] = kblk[i - j] if i >= j else zero
    ab_l, _ = _s5_discretise(lr_l[...], li_l[...], dl_l[...])
    pw_l = _s5_powers(ab_l, tc)
    pair = lambda x: jnp.concatenate([x, x], axis=1)
    widen = lambda x: jnp.concatenate([pair(x)] * (S5_BLK // (2 * S5_STATE)), axis=1)
    bwr, bwi = widen(bb[0]), widen(bb[1])
    wrow = lax.broadcasted_iota(jnp.int32, (nl, S5_BLK), 0) // S5_CH
    wcol = lax.broadcasted_iota(jnp.int32, (nl, S5_BLK), 1) // S5_STATE
    wmask = wrow == wcol
    for j in range(tc):
        pr, pi = pw_l[tc - 1 - j]
        tw_ref[j * nl:(j + 1) * nl, tc * nl:tc * nl + S5_BLK] = jnp.where(wmask, bwr * pr - bwi * pi, 0.0).astype(BF16)
        tw_ref[j * nl:(j + 1) * nl, tc * nl + S5_BLK:] = jnp.where(wmask, bwr * pi + bwi * pr, 0.0).astype(BF16)
    a8 = pw_l[tc]
    a16 = _cmul(a8, a8)
    a32 = _cmul(a16, a16)
    rows = [ab_l, a16, a32, a8]
    for _ in range(tc - 1):
        rows.append(_cmul(rows[-1], a8))
    sc_ref[...] = jnp.zeros_like(sc_ref)
    for k, rw in enumerate(rows):
        sc_ref[0, k:k + 1, :] = rw[0]
        sc_ref[1, k:k + 1, :] = rw[1]
    ab_t, _ = _s5_discretise(lr_t[...], li_t[...], dl_t[...])
    pw_t = _s5_powers(ab_t, tc)
    cc_t = (cr_t[...], ci_t[...])
    vrow = lax.broadcasted_iota(jnp.int32, (S5_BLK, nl), 0) // S5_STATE
    vcol = lax.broadcasted_iota(jnp.int32, (S5_BLK, nl), 1) // S5_CH
    vmask = vrow == vcol
    tall = lambda x: jnp.concatenate([x] * (S5_BLK // S5_STATE), axis=0)
    for tau in range(tc + 1):
        cp = _cmul(cc_t, pw_t[tau])
        vr = jnp.where(vmask, tall(cp[0]), 0.0).astype(BF16)
        vi = jnp.where(vmask, tall(-cp[1]), 0.0).astype(BF16)
        if tau == 0:
            v0_ref[0:S5_BLK, :] = vr
            v0_ref[S5_BLK:, :] = vi
        else:
            v_ref[0:S5_BLK, (tau - 1) * nl:tau * nl] = vr
            v_ref[S5_BLK:, (tau - 1) * nl:tau * nl] = vi


def _s5_tables(a_re, a_im, log_dt, b_re, b_im, c_re, c_im, d_skip):
    g, s, c, gb = S5_GROUPS, S5_STATE, S5_CH, S5_GB
    ldt = jnp.broadcast_to(log_dt[:, :, None], (DEPTH, g, s))
    rows = lambda a: jnp.repeat(a, c, axis=1)
    flat = lambda a: a.reshape(DEPTH, gb, 1, S5_BLK)
    cols = lambda a: jnp.repeat(jnp.swapaxes(a, 1, 2), c, axis=2)
    by_rows = [rows(a_re), rows(a_im), rows(ldt),
               jnp.swapaxes(b_re, 2, 3).reshape(DEPTH, g * c, s), jnp.swapaxes(b_im, 2, 3).reshape(DEPTH, g * c, s),
               c_re.reshape(DEPTH, g * c, s), c_im.reshape(DEPTH, g * c, s)]
    by_lane = [flat(a_re), flat(a_im), flat(ldt)]
    by_col = [cols(a_re), cols(a_im), cols(ldt),
              jnp.transpose(c_re, (0, 3, 1, 2)).reshape(DEPTH, s, g * c),
              jnp.transpose(c_im, (0, 3, 1, 2)).reshape(DEPTH, s, g * c)]
    in_specs = ([pl.BlockSpec((None, LANES, s), lambda l, b: (l, b, 0))] * len(by_rows)
                + [pl.BlockSpec((None, None, 1, S5_BLK), lambda l, b: (l, b, 0, 0))] * len(by_lane)
                + [pl.BlockSpec((None, s, LANES), lambda l, b: (l, 0, b))] * len(by_col))
    blk = lambda *shape: pl.BlockSpec((None, None) + shape, lambda l, b: (l, b) + (0,) * len(shape))
    tcw = S5_TC * LANES
    tw, v_blk, v0_blk, scan = pl.pallas_call(
        _s5_tables_body,
        grid=(DEPTH, gb),
        in_specs=in_specs,
        out_specs=[blk(tcw, tcw + 2 * S5_BLK), blk(2 * S5_BLK, tcw), blk(2 * S5_BLK, LANES), blk(2, 16, S5_BLK)],
        out_shape=[jax.ShapeDtypeStruct((DEPTH, gb, tcw, tcw + 2 * S5_BLK), BF16),
                   jax.ShapeDtypeStruct((DEPTH, gb, 2 * S5_BLK, tcw), BF16),
                   jax.ShapeDtypeStruct((DEPTH, gb, 2 * S5_BLK, LANES), BF16),
                   jax.ShapeDtypeStruct((DEPTH, gb, 2, 16, S5_BLK), F32)],
        compiler_params=_params("arbitrary", "arbitrary"),
        name="s5_tables",
    )(*by_rows, *by_lane, *by_col)
    return tw, v_blk, v0_blk, scan, d_skip.reshape(DEPTH, gb, 1, LANES)


def _s5_prompt_body(u_ref, tw_ref, v_ref, sc_ref, d_ref, y_ref, hout_ref,
                    sre_ref, sim_ref, hin_ref, cr_ref, ci_ref):
    r = pl.program_id(2)
    tr = sre_ref.shape[0]

    @pl.when(r == 0)
    def _():
        cr_ref[...] = jnp.zeros_like(cr_ref)
        ci_ref[...] = jnp.zeros_like(ci_ref)

    u = jnp.concatenate([u_ref[j] for j in range(S5_TC)], axis=1)
    tw = _dot(u, tw_ref[...])
    yd = tw[:, 0:S5_TC * 128]
    sre_ref[...] = tw[:, S5_TC * 128:S5_TC * 128 + S5_BLK]
    sim_ref[...] = tw[:, S5_TC * 128 + S5_BLK:]

    a8 = (sc_ref[0, 3:4, :], sc_ref[1, 3:4, :])
    a16 = (sc_ref[0, 1:2, :], sc_ref[1, 1:2, :])
    a32 = (sc_ref[0, 2:3, :], sc_ref[1, 2:3, :])
    apw = (sc_ref[0, 3:11, :], sc_ref[1, 3:11, :])
    row = lax.broadcasted_iota(jnp.int32, (8, S5_BLK), 0)

    sub = lax.broadcasted_iota(jnp.int32, (tr, S5_BLK), 0) % 8
    xr, xi = sre_ref[...], sim_ref[...]
    for k, (ar, ai) in ((1, a8), (2, a16), (4, a32)):
        pr = jnp.where(sub >= k, pltpu.roll(xr, k, axis=0), 0.0)
        pi = jnp.where(sub >= k, pltpu.roll(xi, k, axis=0), 0.0)
        xr, xi = xr + ar * pr - ai * pi, xi + ar * pi + ai * pr
    sre_ref[...] = xr
    sim_ref[...] = xi

    def tile(t, carry):
        c_re, c_im = carry
        rows = pl.ds(pl.multiple_of(t * 8, 8), 8)
        xr, xi = sre_ref[rows, :], sim_ref[rows, :]
        xr = xr + apw[0] * c_re - apw[1] * c_im
        xi = xi + apw[0] * c_im + apw[1] * c_re
        hin_ref[rows, 0:S5_BLK] = jnp.where(row >= 1, pltpu.roll(xr, 1, axis=0), c_re)
        hin_ref[rows, S5_BLK:] = jnp.where(row >= 1, pltpu.roll(xi, 1, axis=0), c_im)
        return xr[7:8, :], xi[7:8, :]

    c_re, c_im = lax.fori_loop(0, tr // 8, tile, (cr_ref[...], ci_ref[...]))
    cr_ref[...] = c_re
    ci_ref[...] = c_im
    hout_ref[:, 0:S5_BLK] = c_re
    hout_ref[:, S5_BLK:] = c_im

    d_row = jnp.concatenate([d_ref[...]] * S5_TC, axis=1)
    y = yd + _dot(hin_ref[...].astype(BF16), v_ref[...]) + d_row * u.astype(F32)
    for j in range(S5_TC):
        y_ref[j] = y[:, j * 128:(j + 1) * 128].astype(BF16)


def _s5_prompt(u8, layer, n, length, tw, v_blk, scan, d_blk, tr=512):
    rows = length // S5_TC
    tr = min(tr, rows)
    nr = rows // tr
    chunk_blk = pl.BlockSpec((S5_TC, tr, 128), lambda gb, b, r: (0, b * nr + r, gb))
    lay = lambda a: pl.BlockSpec((None, None) + tuple(a.shape[2:]),
                                 lambda gb, b, r: (layer, gb) + (0,) * (a.ndim - 2))
    y8, hout = pl.pallas_call(
        _s5_prompt_body,
        grid=(S5_GB, n, nr),
        in_specs=[chunk_blk, lay(tw), lay(v_blk), lay(scan), lay(d_blk)],
        out_specs=[chunk_blk,
                   pl.BlockSpec((None, None, 1, 2 * S5_BLK), lambda gb, b, r: (b, gb, 0, 0))],
        out_shape=[jax.ShapeDtypeStruct((S5_TC, n * rows, S5_W), BF16),
                   jax.ShapeDtypeStruct((n, S5_GB, 1, 2 * S5_BLK), F32)],
        scratch_shapes=[pltpu.VMEM((tr, S5_BLK), F32), pltpu.VMEM((tr, S5_BLK), F32),
                        pltpu.VMEM((tr, 2 * S5_BLK), F32),
                        pltpu.VMEM((1, S5_BLK), F32), pltpu.VMEM((1, S5_BLK), F32)],
        compiler_params=_params("arbitrary", "arbitrary", "arbitrary"),
        name="s5_prompt",
    )(u8, tw, v_blk, scan, d_blk)
    state = jnp.transpose(hout.reshape(n, S5_GB, 2, 8, S5_STATE), (0, 2, 1, 3, 4))
    return y8, state.reshape(n, 2, S5_GROUPS, S5_STATE)


def _s5_sample_body(u_ref, hr_ref, hi_ref, w_ref, v0_ref, sc_ref, d_ref, y_ref, nr_ref, ni_ref):
    u = u_ref[...]
    bu = _dot(u, w_ref[...])
    ar, ai = sc_ref[0, 0:1, :], sc_ref[1, 0:1, :]
    hr, hi = hr_ref[...], hi_ref[...]
    nr = ar * hr - ai * hi + bu[:, 0:S5_BLK]
    ni = ar * hi + ai * hr + bu[:, S5_BLK:]
    nr_ref[...] = nr
    ni_ref[...] = ni
    hcat = jnp.concatenate([nr, ni], axis=1).astype(BF16)
    y_ref[...] = (_dot(hcat, v0_ref[...]) + d_ref[...] * u.astype(F32)).astype(BF16)


def _s5_sample(u, state, layer, tw, v0_blk, scan, d_blk):
    n = u.shape[0]
    lay = lambda a: pl.BlockSpec((None, None) + tuple(a.shape[2:]),
                                 lambda gb: (layer, gb) + (0,) * (a.ndim - 2))
    st = lambda off: pl.BlockSpec((None, n, S5_BLK), lambda gb: (layer, 0, off + gb))
    half = S5_GROUPS * S5_STATE
    y, nr, ni = pl.pallas_call(
        _s5_sample_body,
        grid=(S5_GB,),
        in_specs=[pl.BlockSpec((n, 128), lambda gb: (0, gb)), st(0), st(S5_GB),
                  pl.BlockSpec((None, None, 128, 2 * S5_BLK), lambda gb: (layer, gb, S5_TC - 1, 1)),
                  lay(v0_blk), lay(scan), lay(d_blk)],
        out_specs=[pl.BlockSpec((n, 128), lambda gb: (0, gb)),
                   pl.BlockSpec((n, S5_BLK), lambda gb: (0, gb)),
                   pl.BlockSpec((n, S5_BLK), lambda gb: (0, gb))],
        out_shape=[jax.ShapeDtypeStruct((n, S5_W), BF16), jax.ShapeDtypeStruct((n, half), F32),
                   jax.ShapeDtypeStruct((n, half), F32)],
        compiler_params=_params("arbitrary"),
        name="s5_sample",
    )(u, state, state, tw, v0_blk, scan, d_blk)
    return y, jnp.stack([nr, ni], axis=1).reshape(n, 2, S5_GROUPS, S5_STATE)


def kernel(x_prompt, x_sample, cache_kv_w128, cache_kv_w512, cache_kv_w2048, state_ssd, state_conv, state_s5, p_prompt, p_sample, attn_rel_bias, ffn1_norm, ffn1_w_gate, ffn1_w_up, ffn1_w_down, mix_norm, w_in, ssd_conv_w, ssd_conv_b, ssd_dt_bias, ssd_a_log, ssd_d, ssd_norm, s5_a_re, s5_a_im, s5_b_re, s5_b_im, s5_c_re, s5_c_im, s5_d, s5_log_dt, w_s5_glu, w_branch_attn, w_branch_ssd, w_out, ffn2_norm, ffn2_w_gate, ffn2_w_up, ffn2_w_down, pe_norm, w_pe_gate, w_pe_proj, final_norm):
    n_p, len_p, _ = x_prompt.shape
    n_s = x_sample.shape[0]
    m_p = n_p * len_p
    assert x_sample.shape[1] == 1 and len_p % (QBLK * ATTN_PATTERNS[-1][1]) == 0
    caches = (cache_kv_w128, cache_kv_w512, cache_kv_w2048)
    dils = tuple(d for _, d in ATTN_PATTERNS)
    for cache, (w, d) in zip(caches, ATTN_PATTERNS):
        assert cache.shape[3] == w == QBLK * d
    caches_t = [jnp.transpose(c, (0, 1, 2, 4, 5, 3)) for c in caches]
    sample_bs = (16, 8, 2)

    bf = lambda a: a.astype(BF16)
    vec = lambda a: a.reshape(DEPTH, 1, a.shape[-1])
    w1g, w1u, w1d = bf(ffn1_w_gate), bf(ffn1_w_up), bf(ffn1_w_down)
    w2g, w2u, w2d = bf(ffn2_w_gate), bf(ffn2_w_up), bf(ffn2_w_down)
    mid_hi = QKV_W + MID_DT + SSD_HEADS
    w_qkv = jnp.swapaxes(w_in[..., :QKV_W].reshape(DEPTH, D_MODEL, 3, 3, GROUP_W), 2, 3)
    w_in_p = (bf(w_qkv.reshape(DEPTH, D_MODEL, QKV_W)),
              bf(jnp.pad(w_in[..., QKV_W:mid_hi], ((0, 0), (0, 0), (0, DT_PAD - SSD_HEADS)))),
              bf(w_in[..., mid_hi:]))
    wba, wbs, wglu, wout = bf(w_branch_attn), bf(w_branch_ssd), bf(w_s5_glu), bf(w_out)
    wpg, wpp = bf(w_pe_gate), bf(w_pe_proj)
    n1, nm, n2, npe = vec(ffn1_norm), vec(mix_norm), vec(ffn2_norm), vec(pe_norm)
    fin = final_norm.reshape(1, D_MODEL)
    cb = vec(ssd_conv_b)
    lane_pad = lambda a: jnp.pad(a, ((0, 0), (0, DT_PAD - a.shape[-1]))).reshape(DEPTH, 1, DT_PAD)
    dtb, alog = lane_pad(ssd_dt_bias), lane_pad(ssd_a_log)
    dsk = jnp.repeat(ssd_d, SSD_P, axis=-1).reshape(DEPTH, 1, SSD_INNER)
    ng = vec(ssd_norm)
    per_head = lambda a: jnp.broadcast_to(a.reshape(DEPTH * SSD_HEADS, 1, 1), (DEPTH * SSD_HEADS, 1, n_s))
    alog_h, dsk_h = per_head(ssd_a_log), per_head(ssd_d)

    tw, v_blk, v0_blk, scan, d_blk = _s5_tables(s5_a_re, s5_a_im, s5_log_dt, s5_b_re, s5_b_im,
                                                s5_c_re, s5_c_im, s5_d)

    bias_p = [_prompt_bias(attn_rel_bias, g, d) for g, (_, d) in enumerate(ATTN_PATTERNS)]
    bias_s = [_sample_bias(attn_rel_bias, g, d) for g, (_, d) in enumerate(ATTN_PATTERNS)]

    pp = p_prompt.reshape(DEPTH, m_p, PE_DIM)
    psm = p_sample.reshape(DEPTH, n_s, PE_DIM)
    st_ssd_t = jnp.transpose(state_ssd, (0, 2, 3, 4, 1))
    st_s5 = state_s5.reshape(DEPTH, n_s, 2 * S5_GROUPS * S5_STATE)

    xp = x_prompt.reshape(m_p, D_MODEL)
    xs = x_sample.reshape(n_s, D_MODEL)
    tm_p, tm_f, tm_s = 256, 512, n_s
    kv_p, kv_s = [[], [], []], [[], [], []]
    ssd_p, ssd_s, conv_p, conv_s, s5_p, s5_s = [], [], [], [], [], []

    def kv_prompt(qg, d):
        n, _, rows, _ = qg.shape
        t = qg[:, :, rows - QBLK:, GROUP_W:].reshape(n, d, QBLK, 2, HPG, HEAD_DIM)
        return jnp.transpose(t, (0, 3, 2, 1, 4, 5)).reshape(n, 2, QBLK * d, HPG, HEAD_DIM)

    for i in range(DEPTH):
        last = i == DEPTH - 1
        xp = _ffn(xp, i, tm_f, n1, w1g, w1u, w1d)
        *qgs, z, xbc, dt, u, gates = _inproj(xp, i, tm_p, nm, w_in_p, n_p, len_p, dils, True)
        parts = []
        for g, d in enumerate(dils):
            parts += list(_attn_prompt(qgs[g], bias_p[g], g))
            kv_p[g].append(kv_prompt(qgs[g], d))
        y_ssd, st = _ssd_prompt(xbc, z, dt, i, n_p, len_p, ssd_conv_w, cb, dtb, alog, dsk, ng)
        ssd_p.append(jnp.swapaxes(st, 2, 3))
        conv_p.append(xbc.reshape(n_p, len_p, SSD_CONV_CH)[:, len_p - 3:])
        y_s5, st5 = _s5_prompt(u, i, n_p, len_p, tw, v_blk, scan, d_blk)
        s5_p.append(st5)
        xp = _merge(xp, i, tm_f, parts, dils, len_p, y_ssd.reshape(m_p, SSD_INNER), y_s5, gates,
                    wba, wbs, wglu, wout)
        xp = _ffn(xp, i, tm_f, n2, w2g, w2u, w2d, pe=(pp, npe, wpg, wpp), final=fin if last else None)
        xs = _ffn(xs, i, tm_s, n1, w1g, w1u, w1d)
        *qgs, z, xbc, dt, u, gates = _inproj(xs, i, tm_s, nm, w_in_p, 1, n_s, (1, 1, 1), False)
        parts = []
        for g in range(len(dils)):
            qg = qgs[g].reshape(n_s, ATTN_W)
            o_s, l_s = _attn_sample(qg, caches_t[g], i, bias_s[g], g, sample_bs[g])
            parts += [o_s.reshape(1, 1, n_s, GROUP_W), l_s.reshape(1, 1, n_s, GROUP_W)]
            kv_s[g].append(qg[:, GROUP_W:].reshape(n_s, 2, 1, HPG, HEAD_DIM))
        xc, dts = _ssd_conv_sample(xbc, state_conv, dt, i, ssd_conv_w, cb, dtb)
        conv_s.append(jnp.concatenate([state_conv[i][:, 1:], xbc[:, None, :]], axis=1))
        xct = xc.T
        x3 = xct[:SSD_INNER].reshape(SSD_HEADS, SSD_P, 1, n_s)
        bc = xct[SSD_INNER:].reshape(2 * SSD_GROUPS, SSD_S, n_s)
        dtt = dts.T[:SSD_HEADS].reshape(SSD_HEADS, 1, n_s)
        yh, hn = _ssd_state_sample(st_ssd_t, i, x3, bc, dtt, alog_h, dsk_h)
        ssd_s.append(jnp.transpose(hn, (3, 0, 1, 2)))
        y_ssd = _ssd_gate_sample(yh.reshape(SSD_INNER, n_s).T, z, i, ng)
        y_s5, st5 = _s5_sample(u, st_s5, i, tw, v0_blk, scan, d_blk)
        s5_s.append(st5)
        xs = _merge(xs, i, tm_s, parts, (1, 1, 1), n_s, y_ssd, y_s5, gates, wba, wbs, wglu, wout)
        xs = _ffn(xs, i, tm_s, n2, w2g, w2u, w2d, pe=(psm, npe, wpg, wpp), final=fin if last else None)

    return (xp.reshape(n_p, len_p, D_MODEL), xs.reshape(n_s, 1, D_MODEL),
            jnp.stack(kv_p[0], 0), jnp.stack(kv_s[0], 0),
            jnp.stack(kv_p[1], 0), jnp.stack(kv_s[1], 0),
            jnp.stack(kv_p[2], 0), jnp.stack(kv_s[2], 0),
            jnp.stack(ssd_p, 0), jnp.stack(ssd_s, 0),
            jnp.stack(conv_p, 0), jnp.stack(conv_s, 0),
            jnp.stack(s5_p, 0), jnp.stack(s5_s, 0))
```

```python
import functools
import math

import numpy as np
import jax
import jax.numpy as jnp
from jax import lax
from jax.experimental import pallas as pl
from jax.experimental.pallas import tpu as pltpu

F32 = jnp.float32
BF16 = jnp.bfloat16

D_MODEL = 1024
DEPTH = 4
HEAD_DIM = 64
ATTN_PATTERNS = ((128, 1), (512, 4), (2048, 16))
HPG = 4
GROUP_W = HPG * HEAD_DIM
ATTN_W = 3 * GROUP_W
QBLK = 128
N_BUCKETS = 32
BUCKET_MAX_DIST = 2048
NEG_INF = -1e30
SSD_HEADS = 12
SSD_P = 64
SSD_S = 64
SSD_GROUPS = 4
SSD_INNER = SSD_HEADS * SSD_P
SSD_CONV_CH = SSD_INNER + 2 * SSD_GROUPS * SSD_S
SSD_CHUNK = 256
S5_GROUPS = 48
S5_CH = 16
S5_STATE = 64
S5_W = S5_GROUPS * S5_CH
S5_TC = 8
S5_GB = 6
S5_BLK = 8 * S5_STATE
D_FF = 2816
PE_DIM = 256
RMS_EPS = 1e-6
LANES = 128
DT_PAD = LANES
QKV_W = 3 * ATTN_W
MID_XBC = SSD_INNER
MID_DT = MID_XBC + SSD_CONV_CH
MID_W = MID_DT + DT_PAD
REST_G = S5_W
REST_W = REST_G + 3 * D_MODEL
VMEM_LIMIT = 56 * 1024 * 1024


def _params(*sem):
    return pltpu.CompilerParams(dimension_semantics=sem, vmem_limit_bytes=VMEM_LIMIT)


def _rms(x, g):
    inv = lax.rsqrt(jnp.mean(x * x, axis=-1, keepdims=True) + RMS_EPS)
    return (x * inv) * g


def _dot(a, b):
    return jnp.dot(a, b, preferred_element_type=F32)


def _dot_nt(a, b):
    return lax.dot_general(a, b, (((1,), (1,)), ((), ())), preferred_element_type=F32)


def _dot_tn(a, b):
    return lax.dot_general(a, b, (((0,), (0,)), ((), ())), preferred_element_type=F32)


def _sigmoid(x):
    return jax.nn.sigmoid(x)


def _layer_spec(arr, layer):
    nd = arr.ndim - 1
    return pl.BlockSpec((None,) + tuple(arr.shape[1:]),
                        lambda *_: (layer,) + (0,) * nd,
                        pipeline_mode=pl.Buffered(1))


def _full_spec(arr):
    nd = arr.ndim
    return pl.BlockSpec(tuple(arr.shape), lambda *_: (0,) * nd, pipeline_mode=pl.Buffered(1))


def _ffn_body(*refs, has_pe, has_final):
    it = iter(refs)
    x_ref, g_ref, wg_ref, wu_ref, wd_ref = (next(it) for _ in range(5))
    if has_pe:
        p_ref, pn_ref, wpg_ref, wpp_ref = (next(it) for _ in range(4))
    if has_final:
        fn_ref = next(it)
    o_ref = next(it)
    act_ref = next(it)
    o_ref[...] = _ffn_math(x_ref[...], g_ref, wg_ref, wu_ref, wd_ref, act_ref,
                           (p_ref, pn_ref, wpg_ref, wpp_ref) if has_pe else None,
                           fn_ref if has_final else None)


def _ffn_math(x, g_ref, wg_ref, wu_ref, wd_ref, act_ref, pe_refs, fn_ref):
    h = _rms(x, g_ref[...]).astype(BF16)
    half = D_FF // 2
    for f0 in (0, half):
        gate = _dot(h, wg_ref[:, f0:f0 + half])
        up = _dot(h, wu_ref[:, f0:f0 + half])
        act_ref[:, f0:f0 + half] = (gate * _sigmoid(gate) * up).astype(BF16)
    y = x + 0.5 * _dot(act_ref[...], wd_ref[...])
    if pe_refs is not None:
        p_ref, pn_ref, wpg_ref, wpp_ref = pe_refs
        h2 = _rms(y, pn_ref[...]).astype(BF16)
        gt = _sigmoid(_dot(h2, wpg_ref[...]))
        y = y + gt * _dot(p_ref[...].astype(BF16), wpp_ref[...])
    if fn_ref is not None:
        y = _rms(y, fn_ref[...])
    return y


def _ffn(x, layer, tm, norm, wg, wu, wd, pe=None, final=None):
    m = x.shape[0]
    row = lambda i: (i, 0)
    args = [x, norm, wg, wu, wd]
    specs = [pl.BlockSpec((tm, D_MODEL), row), _layer_spec(norm, layer), _layer_spec(wg, layer),
             _layer_spec(wu, layer), _layer_spec(wd, layer)]
    if pe is not None:
        p_all, pn, wpg, wpp = pe
        args += [p_all, pn, wpg, wpp]
        specs += [pl.BlockSpec((None, tm, PE_DIM), lambda i: (layer, i, 0)), _layer_spec(pn, layer),
                  _layer_spec(wpg, layer), _layer_spec(wpp, layer)]
    if final is not None:
        args.append(final)
        specs.append(_full_spec(final))
    return pl.pallas_call(
        functools.partial(_ffn_body, has_pe=pe is not None, has_final=final is not None),
        grid=(m // tm,),
        in_specs=specs,
        out_specs=pl.BlockSpec((tm, D_MODEL), row),
        out_shape=jax.ShapeDtypeStruct((m, D_MODEL), F32),
        scratch_shapes=[pltpu.VMEM((tm, D_FF), BF16)],
        compiler_params=_params("arbitrary"),
        name="ffn",
    )(*args)


def _inproj_body(x_ref, g_ref, wq_ref, wm_ref, wr_ref, qa_ref, qb_ref, qc_ref, z_ref, xbc_ref, dt_ref, u_ref,
                 gates_ref, stage_ref, *, dils, prompt):
    h = _rms(x_ref[...], g_ref[...]).astype(BF16)
    tm = x_ref.shape[0]
    for g, (out_ref, d) in enumerate(zip((qa_ref, qb_ref, qc_ref), dils)):
        res = _dot(h, wq_ref[:, g * ATTN_W:(g + 1) * ATTN_W])
        if d == 1:
            out_ref[0] = res
        else:
            for c in range(ATTN_W // LANES):
                stage_ref[c] = res[:, c * LANES:(c + 1) * LANES]
            for r in range(d):
                for c in range(ATTN_W // LANES):
                    out_ref[r, :, c * LANES:(c + 1) * LANES] = stage_ref[c, pl.ds(r, tm // d, stride=d), :]
    z_ref[...] = _dot(h, wm_ref[:, 0:MID_XBC]).astype(BF16)
    xbc_ref[...] = _dot(h, wm_ref[:, MID_XBC:MID_DT])
    dt_ref[...] = _dot(h, wm_ref[:, MID_DT:MID_W])
    u = _dot(h, wr_ref[:, 0:REST_G])
    if prompt:
        for c in range(S5_W // LANES):
            stage_ref[c] = u[:, c * LANES:(c + 1) * LANES]
        for j in range(S5_TC):
            for c in range(S5_W // LANES):
                u_ref[j, :, c * LANES:(c + 1) * LANES] = (
                    stage_ref[c, pl.ds(j, tm // S5_TC, stride=S5_TC), :].astype(BF16))
    else:
        u_ref[...] = u.astype(BF16)
    for k in range(3):
        c0 = REST_G + k * D_MODEL
        gates_ref[:, k * D_MODEL:(k + 1) * D_MODEL] = _sigmoid(_dot(h, wr_ref[:, c0:c0 + D_MODEL])).astype(BF16)


def _inproj(x, layer, tm, norm, w, n, length, dils, prompt):
    m = x.shape[0]
    tps = length // tm
    row = lambda i: (i, 0)
    q_specs = [pl.BlockSpec((None, d, tm // d, ATTN_W), lambda i: (i // tps, 0, i % tps, 0)) for d in dils]
    q_shapes = [jax.ShapeDtypeStruct((n, d, length // d, ATTN_W), F32) for d in dils]
    flat = lambda wd, dt: (pl.BlockSpec((tm, wd), row), jax.ShapeDtypeStruct((m, wd), dt))
    if prompt:
        u_out = (pl.BlockSpec((S5_TC, tm // S5_TC, S5_W), lambda i: (0, i, 0)),
                 jax.ShapeDtypeStruct((S5_TC, m // S5_TC, S5_W), BF16))
    else:
        u_out = flat(S5_W, BF16)
    rest = [flat(SSD_INNER, BF16), flat(SSD_CONV_CH, F32), flat(DT_PAD, F32), u_out, flat(3 * D_MODEL, BF16)]
    return pl.pallas_call(
        functools.partial(_inproj_body, dils=dils, prompt=prompt),
        grid=(m // tm,),
        in_specs=[pl.BlockSpec((tm, D_MODEL), row), _layer_spec(norm, layer)] + [_layer_spec(a, layer) for a in w],
        out_specs=q_specs + [s for s, _ in rest],
        out_shape=q_shapes + [s for _, s in rest],
        scratch_shapes=[pltpu.VMEM((ATTN_W // LANES, tm, LANES), F32)],
        compiler_params=_params("arbitrary"),
        name="inproj",
    )(x, norm, *w)


def _merge_body(x_ref, o0, l0, o1, l1, o2, l2, yssd_ref, ys5_ref, gates_ref,
                wba_ref, wbs_ref, wglu_ref, wout_ref, g_ref, wg_ref, wu_ref, wd_ref, p_ref, pn_ref, wpg_ref, wpp_ref,
                *rest, dils, s5_chunked, has_final):
    fn_ref = rest[0] if has_final else None
    out_ref, stage_ref, s5_stage_ref, act_ref = rest[-4:]
    tm = x_ref.shape[0]
    parts = []
    for k, (ref, d) in enumerate(zip((o0, l0, o1, l1, o2, l2), (dils[0], dils[0], dils[1], dils[1], dils[2], dils[2]))):
        if d == 1:
            parts.append(ref[0])
        else:
            nl = GROUP_W // LANES
            for r in range(d):
                for c in range(nl):
                    stage_ref[k * nl + c, pl.ds(r, tm // d, stride=d), :] = ref[r, :, c * LANES:(c + 1) * LANES]
            parts.append(jnp.concatenate([stage_ref[k * nl + c] for c in range(nl)], axis=1))
    oa, la, ob, lb, oc, lc = parts
    mx = jnp.maximum(jnp.maximum(la, lb), lc)
    ea, eb, ec = jnp.exp(la - mx), jnp.exp(lb - mx), jnp.exp(lc - mx)
    attn = (ea * oa + eb * ob + ec * oc) / (ea + eb + ec)
    acc = gates_ref[:, 0:D_MODEL].astype(F32) * _dot(attn.astype(BF16), wba_ref[...])
    acc = acc + gates_ref[:, D_MODEL:2 * D_MODEL].astype(F32) * _dot(yssd_ref[...], wbs_ref[...])
    if s5_chunked:
        nl6 = S5_W // LANES
        for j in range(S5_TC):
            for c in range(nl6):
                s5_stage_ref[c, pl.ds(j, tm // S5_TC, stride=S5_TC), :] = (
                    ys5_ref[j, :, c * LANES:(c + 1) * LANES].astype(F32))
        ys5 = jnp.concatenate([s5_stage_ref[c] for c in range(nl6)], axis=1)
    else:
        ys5 = ys5_ref[...].astype(F32)
    gl = _dot(jax.nn.gelu(ys5).astype(BF16), wglu_ref[...])
    s5 = gl[:, :D_MODEL] * _sigmoid(gl[:, D_MODEL:])
    acc = acc + gates_ref[:, 2 * D_MODEL:].astype(F32) * s5
    x_mid = x_ref[...] + _dot(acc.astype(BF16), wout_ref[...])
    out_ref[...] = _ffn_math(x_mid, g_ref, wg_ref, wu_ref, wd_ref, act_ref,
                             (p_ref, pn_ref, wpg_ref, wpp_ref), fn_ref)


def _merge(x, layer, tm, attn_parts, dils, length, y_ssd, y_s5, gates, wba, wbs, wglu, wout, ffn, pe, final):
    m = x.shape[0]
    tps = length // tm
    row = lambda i: (i, 0)
    s5_chunked = y_s5.ndim == 3
    args = [x] + list(attn_parts) + [y_ssd, y_s5, gates, wba, wbs, wglu, wout] + list(ffn) + list(pe)
    part_specs = []
    for d in dils:
        part_specs += [pl.BlockSpec((None, d, tm // d, GROUP_W), lambda i: (i // tps, 0, i % tps, 0))] * 2
    s5_spec = (pl.BlockSpec((S5_TC, tm // S5_TC, S5_W), lambda i: (0, i, 0)) if s5_chunked
               else pl.BlockSpec((tm, S5_W), row))
    specs = ([pl.BlockSpec((tm, D_MODEL), row)] + part_specs
             + [pl.BlockSpec((tm, SSD_INNER), row), s5_spec, pl.BlockSpec((tm, 3 * D_MODEL), row)]
             + [_layer_spec(w, layer) for w in (wba, wbs, wglu, wout)]
             + [_layer_spec(w, layer) for w in ffn]
             + [pl.BlockSpec((None, tm, PE_DIM), lambda i: (layer, i, 0))]
             + [_layer_spec(w, layer) for w in pe[1:]])
    if final is not None:
        args.append(final)
        specs.append(_full_spec(final))
    return pl.pallas_call(
        functools.partial(_merge_body, dils=dils, s5_chunked=s5_chunked, has_final=final is not None),
        grid=(m // tm,),
        in_specs=specs,
        out_specs=pl.BlockSpec((tm, D_MODEL), row),
        out_shape=jax.ShapeDtypeStruct((m, D_MODEL), F32),
        scratch_shapes=[pltpu.VMEM((6 * GROUP_W // LANES, tm, LANES), F32),
                        pltpu.VMEM((S5_W // LANES, tm, LANES), F32),
                        pltpu.VMEM((tm, D_FF), BF16)],
        compiler_params=_params("arbitrary"),
        name="merge_ffn",
    )(*args)


def _t5_bucket(dist):
    max_exact = N_BUCKETS // 2
    d = np.asarray(dist).astype(np.int32)
    df = np.maximum(d, 1).astype(np.float32)
    large = max_exact + (np.log(df / max_exact) / math.log(BUCKET_MAX_DIST / max_exact)
                         * (N_BUCKETS - max_exact)).astype(np.int32)
    return np.where(d < max_exact, d, np.minimum(large, N_BUCKETS - 1)).astype(np.int32)


def _prompt_bias(rel_bias, g, d):
    w = 2 * QBLK
    tab = rel_bias[:, g * HPG:(g + 1) * HPG].astype(F32)
    near = jnp.transpose(tab[_t5_bucket(np.arange(QBLK, -1, -1) * d)], (1, 0))
    row0 = jnp.concatenate([near, jnp.full((HPG, w - QBLK - 1), NEG_INF, F32)], axis=1)
    x = jnp.concatenate([row0, row0, jnp.full((HPG, 1), NEG_INF, F32)], axis=1)
    flat = jnp.tile(x, (1, QBLK))[:, :QBLK * 2 * w]
    return flat.reshape(HPG, QBLK, 2 * w)[:, :, w:]


def _attn_prompt_body(q_ref, kp_ref, kc_ref, vp_ref, vc_ref, bias_ref, o_ref, l_ref):
    nsub = q_ref.shape[0] // QBLK
    first = pl.program_id(2) == 0
    lane = lax.broadcasted_iota(jnp.int32, (1, 2 * QBLK), 1)
    pen = jnp.where(jnp.logical_and(first, lane < QBLK), NEG_INF, 0.0).astype(F32)
    q = (q_ref[...] * (HEAD_DIM ** -0.5)).astype(BF16)
    k = jnp.concatenate([kp_ref[...], kc_ref[...]], axis=0).astype(BF16)
    v = jnp.concatenate([vp_ref[...], vc_ref[...]], axis=0).astype(BF16)
    for s in range(nsub):
        rq = slice(s * QBLK, (s + 1) * QBLK)
        rk = slice(s * QBLK, (s + 2) * QBLK)
        for h in range(HPG):
            sl = slice(h * HEAD_DIM, (h + 1) * HEAD_DIM)
            sc = _dot_nt(q[rq, sl], k[rk, sl]) + bias_ref[h]
            if s == 0:
                sc = sc + pen
            m = jnp.max(sc, axis=-1, keepdims=True)
            e = jnp.exp(sc - m)
            den = jnp.sum(e, axis=-1, keepdims=True)
            o_ref[rq, sl] = _dot(e.astype(BF16), v[rk, sl]) / den
            l_ref[rq, sl] = jnp.broadcast_to(m + jnp.log(den), (QBLK, HEAD_DIM))


def _attn_prompt(qkv_g, bias, g, tq=512):
    n, d, rows, _ = qkv_g.shape
    tq = min(tq, rows)
    nsub = tq // QBLK
    nb = rows // tq
    blk = (None, None, tq, GROUP_W)
    cur = lambda col: pl.BlockSpec(blk, lambda b, r, i: (b, r, i, col))
    prev = lambda col: pl.BlockSpec((None, None, QBLK, GROUP_W),
                                    lambda b, r, i: (b, r, jnp.maximum(i * nsub - 1, 0), col))
    out_spec = pl.BlockSpec(blk, lambda b, r, i: (b, r, i, 0))
    out_sds = jax.ShapeDtypeStruct((n, d, rows, GROUP_W), F32)
    return pl.pallas_call(
        _attn_prompt_body,
        grid=(n, d, nb),
        in_specs=[cur(0), prev(1), cur(1), prev(2), cur(2), _full_spec(bias)],
        out_specs=[out_spec, out_spec],
        out_shape=[out_sds, out_sds],
        compiler_params=_params("arbitrary", "arbitrary", "arbitrary"),
        name=f"attn_prompt_g{g}",
    )(qkv_g, qkv_g, qkv_g, qkv_g, qkv_g, bias)


def _sample_bias(rel_bias, g, d):
    steps = QBLK - np.arange(QBLK)
    tab = rel_bias[:, g * HPG:(g + 1) * HPG].astype(F32)
    on_grid = jnp.transpose(tab[_t5_bucket(steps * d)], (1, 0))[:, :, None]
    off_grid = jnp.full((HPG, QBLK, d - 1), NEG_INF, F32)
    cache_b = jnp.concatenate([on_grid, off_grid], axis=2).reshape(HPG, QBLK * d)
    self_b = jnp.broadcast_to(tab[0][:, None], (HPG, LANES))
    pad = lambda a: jnp.concatenate([a, jnp.zeros((8 - HPG, a.shape[1]), F32)], 0)
    return pad(cache_b), pad(self_b)


def _attn_sample_body(q_ref, kv_ref, cb_ref, sb_ref, o_ref, l_ref, *, bs):
    base = pl.program_id(0) * bs
    rowi = lax.broadcasted_iota(jnp.int32, (8, GROUP_W), 0)
    lane = lax.broadcasted_iota(jnp.int32, (8, GROUP_W), 1)
    own = (lane // HEAD_DIM) == rowi
    p = kv_ref.shape[-1]
    for j in range(bs):
        r = pl.ds(base + j, 1)
        q = q_ref[r, 0:GROUP_W] * (HEAD_DIM ** -0.5)
        kn = q_ref[r, GROUP_W:2 * GROUP_W]
        vn = q_ref[r, 2 * GROUP_W:3 * GROUP_W]
        qblk = jnp.where(own, jnp.broadcast_to(q, (8, GROUP_W)), 0.0)
        kt = kv_ref[j, 0].reshape(GROUP_W, p).astype(BF16)
        vt = kv_ref[j, 1].reshape(GROUP_W, p).astype(BF16)
        s = _dot(qblk.astype(BF16), kt) + cb_ref[...]
        s_self = jnp.sum(qblk * kn, axis=-1, keepdims=True) + sb_ref[:, 0:1]
        m = jnp.maximum(jnp.max(s, axis=-1, keepdims=True), s_self)
        e = jnp.exp(s - m)
        e_self = jnp.exp(s_self - m)
        den = jnp.sum(e, axis=-1, keepdims=True) + e_self
        o = (_dot_nt(e.astype(BF16), vt) + e_self * vn) / den
        o_ref[r, :] = jnp.sum(jnp.where(own, o, 0.0), axis=0, keepdims=True)
        lse = m + jnp.log(den)
        l_ref[r, :] = jnp.sum(jnp.where(own, lse, 0.0), axis=0, keepdims=True)


def _attn_sample(qkv_g, cache_t, layer, biases, g, bs):
    n = qkv_g.shape[0]
    p = cache_t.shape[-1]
    cb, sb = biases
    out_sds = jax.ShapeDtypeStruct((n, GROUP_W), F32)
    return pl.pallas_call(
        functools.partial(_attn_sample_body, bs=bs),
        grid=(n // bs,),
        in_specs=[_full_spec(qkv_g),
                  pl.BlockSpec((None, bs, 2, HPG, HEAD_DIM, p), lambda s: (layer, s, 0, 0, 0, 0)),
                  _full_spec(cb), _full_spec(sb)],
        out_specs=[pl.BlockSpec((n, GROUP_W), lambda s: (0, 0))] * 2,
        out_shape=[out_sds, out_sds],
        compiler_params=_params("arbitrary"),
        name=f"attn_sample_g{g}",
    )(qkv_g, cache_t, cb, sb)


def _softplus(x):
    return jnp.maximum(x, 0.0) + jnp.log(1.0 + jnp.exp(-jnp.abs(x)))


def _split3(x):
    hi = x.astype(BF16)
    r1 = x - hi.astype(F32)
    mid = r1.astype(BF16)
    lo = (r1 - mid.astype(F32)).astype(BF16)
    return hi, mid, lo


def _ssd_prompt_body(xbc_ref, z_ref, dt_ref, cw_ref, cb_ref, dtb_ref, alog_ref, dsk_ref, ng_ref,
                     y_ref, st_ref, xs_ref, state_ref, ybuf_ref):
    @pl.when(pl.program_id(1) == 0)
    def _():
        state_ref[...] = jnp.zeros_like(state_ref)
        xs_ref[:, 0:8, :] = jnp.zeros((xs_ref.shape[0], 8, SSD_CONV_CH), F32)

    for b in range(xbc_ref.shape[0]):
        _ssd_chunk(xbc_ref.at[b], z_ref.at[b], dt_ref.at[b], cw_ref, cb_ref, dtb_ref, alog_ref, dsk_ref, ng_ref,
                   y_ref.at[b], xs_ref.at[b], state_ref.at[b], ybuf_ref.at[b])
    st_ref[...] = state_ref[...]


def _ssd_chunk(xbc_ref, z_ref, dt_ref, cw_ref, cb_ref, dtb_ref, alog_ref, dsk_ref, ng_ref,
               y_ref, xs_ref, state_ref, ybuf_ref):
    q = SSD_CHUNK
    xs_ref[8:8 + q, :] = xbc_ref[...]
    full = xs_ref[...]
    conv = cb_ref[...] + cw_ref[3:4, :] * full[8:8 + q, :]
    for s in range(1, 4):
        conv = conv + cw_ref[3 - s:4 - s, :] * pltpu.roll(full, s, axis=0)[8:8 + q, :]
    xs_ref[0:8, :] = full[q:q + 8, :]
    xc = conv * _sigmoid(conv)

    dt = _softplus(dt_ref[...] + dtb_ref[...])
    a = -jnp.exp(alog_ref[...])
    da = dt * a
    ri = lax.broadcasted_iota(jnp.int32, (q, q), 0)
    ci = lax.broadcasted_iota(jnp.int32, (q, q), 1)
    causal = ri >= ci
    tril = jnp.where(causal, 1.0, 0.0).astype(BF16)
    cum = sum(_dot(tril, part) for part in _split3(da))
    cum_t = cum.T
    dt_t = dt.T
    cum_last = cum[q - 1:q, :]
    w_end = jnp.exp(cum_last - cum) * dt
    e_cum = jnp.exp(cum)
    e_last = jnp.exp(cum_last)

    for g in range(SSD_GROUPS):
        bgt = xc[:, SSD_INNER + g * SSD_S:SSD_INNER + (g + 1) * SSD_S].T.astype(BF16)
        cg = xc[:, SSD_INNER + (SSD_GROUPS + g) * SSD_S:SSD_INNER + (SSD_GROUPS + g + 1) * SSD_S].astype(BF16)
        gmat = _dot(cg, bgt)
        for hh in range(SSD_HEADS // SSD_GROUPS):
            h = g * (SSD_HEADS // SSD_GROUPS) + hh
            sl = slice(h * SSD_P, (h + 1) * SSD_P)
            diff = cum[:, h:h + 1] - cum_t[h:h + 1, :]
            decay = jnp.exp(jnp.where(causal, diff, NEG_INF))
            scores = gmat * decay * dt_t[h:h + 1, :]
            xh = xc[:, sl]
            hin = state_ref[h]
            y = _dot(scores.astype(BF16), xh.astype(BF16))
            y = y + _dot(cg, hin.astype(BF16)) * e_cum[:, h:h + 1]
            y = y + dsk_ref[:, sl] * xh
            st = _dot(bgt, (xh * w_end[:, h:h + 1]).astype(BF16))
            state_ref[h] = e_last[:, h:h + 1] * hin + st
            ybuf_ref[:, sl] = y

    zf = z_ref[...].astype(F32)
    yg = ybuf_ref[...] * (zf * _sigmoid(zf))
    y_ref[...] = _rms(yg, ng_ref[...]).astype(BF16)


def _ssd_prompt(xbc, z, dt, layer, n, length, cw, cb, dtb, alog, dsk, ng):
    nc = length // SSD_CHUNK
    nb = 1
    blk = lambda w: pl.BlockSpec((nb, SSD_CHUNK, w), lambda b, c: (b, c, 0))
    return pl.pallas_call(
        _ssd_prompt_body,
        grid=(n // nb, nc),
        in_specs=[blk(SSD_CONV_CH), blk(SSD_INNER), blk(DT_PAD)]
                 + [_layer_spec(w, layer) for w in (cw, cb, dtb, alog, dsk, ng)],
        out_specs=[blk(SSD_INNER),
                   pl.BlockSpec((nb, SSD_HEADS, SSD_P, SSD_S), lambda b, c: (b, 0, 0, 0))],
        out_shape=[jax.ShapeDtypeStruct((n, length, SSD_INNER), BF16),
                   jax.ShapeDtypeStruct((n, SSD_HEADS, SSD_P, SSD_S), F32)],
        scratch_shapes=[pltpu.VMEM((nb, SSD_CHUNK + 8, SSD_CONV_CH), F32),
                        pltpu.VMEM((nb, SSD_HEADS, SSD_P, SSD_S), F32),
                        pltpu.VMEM((nb, SSD_CHUNK, SSD_INNER), F32)],
        compiler_params=_params("arbitrary", "arbitrary"),
        name="ssd_prompt",
    )(xbc.reshape(n, length, SSD_CONV_CH), z.reshape(n, length, SSD_INNER),
      dt.reshape(n, length, DT_PAD), cw, cb, dtb, alog, dsk, ng)


def _ssd_conv_sample_body(xbc_ref, cs_ref, dt_ref, cw_ref, cb_ref, dtb_ref, xc_ref, dts_ref):
    conv = cb_ref[...] + cw_ref[3:4, :] * xbc_ref[...]
    for k in range(3):
        conv = conv + cw_ref[k:k + 1, :] * cs_ref[:, k * SSD_CONV_CH:(k + 1) * SSD_CONV_CH]
    xc_ref[...] = conv * _sigmoid(conv)
    dts_ref[...] = _softplus(dt_ref[...] + dtb_ref[...])


def _ssd_conv_sample(xbc, conv_state, dt, layer, cw, cb, dtb):
    n = xbc.shape[0]
    cs = conv_state.reshape(DEPTH, n, 3 * SSD_CONV_CH)
    return pl.pallas_call(
        _ssd_conv_sample_body,
        grid=(1,),
        in_specs=[_full_spec(xbc), _layer_spec(cs, layer), _full_spec(dt)]
                 + [_layer_spec(w, layer) for w in (cw, cb, dtb)],
        out_specs=[pl.BlockSpec((n, SSD_CONV_CH), lambda i: (0, 0)),
                   pl.BlockSpec((n, DT_PAD), lambda i: (0, 0))],
        out_shape=[jax.ShapeDtypeStruct((n, SSD_CONV_CH), F32), jax.ShapeDtypeStruct((n, DT_PAD), F32)],
        compiler_params=_params("arbitrary"),
        name="ssd_conv_sample",
    )(xbc, cs, dt, cw, cb, dtb)


def _ssd_state_sample_body(h0_ref, x_ref, b_ref, c_ref, dt_ref, alog_ref, dsk_ref, y_ref, hn_ref):
    h0 = h0_ref[...]
    x, bv, cv, dt = x_ref[...], b_ref[...], c_ref[...], dt_ref[...]
    dec = jnp.exp(dt * (-jnp.exp(alog_ref[...])))
    y_off = jnp.sum(h0 * cv[None], axis=1, keepdims=True) * dec
    cb = jnp.sum(cv * bv, axis=0, keepdims=True)
    y_ref[...] = (cb * dt) * x + y_off + dsk_ref[...] * x
    hn_ref[...] = dec * h0 + (dt * x) * bv[None]


def _ssd_state_sample(state_t, layer, x3, bc, dtt, alog_h, dsk_h):
    n = x3.shape[-1]
    rep = SSD_HEADS // SSD_GROUPS
    lanes = lambda: pl.BlockSpec((None, 1, n), lambda h: (layer * SSD_HEADS + h, 0, 0))
    st_blk = (None, None, SSD_P, SSD_S, n)
    return pl.pallas_call(
        _ssd_state_sample_body,
        grid=(SSD_HEADS,),
        in_specs=[pl.BlockSpec(st_blk, lambda h: (layer, h, 0, 0, 0)),
                  pl.BlockSpec((None, SSD_P, 1, n), lambda h: (h, 0, 0, 0)),
                  pl.BlockSpec((None, SSD_S, n), lambda h: (h // rep, 0, 0)),
                  pl.BlockSpec((None, SSD_S, n), lambda h: (SSD_GROUPS + h // rep, 0, 0)),
                  pl.BlockSpec((None, 1, n), lambda h: (h, 0, 0)), lanes(), lanes()],
        out_specs=[pl.BlockSpec((None, SSD_P, 1, n), lambda h: (h, 0, 0, 0)),
                   pl.BlockSpec((None, SSD_P, SSD_S, n), lambda h: (h, 0, 0, 0))],
        out_shape=[jax.ShapeDtypeStruct((SSD_HEADS, SSD_P, 1, n), F32),
                   jax.ShapeDtypeStruct((SSD_HEADS, SSD_P, SSD_S, n), F32)],
        compiler_params=_params("arbitrary"),
        name="ssd_state_sample",
    )(state_t, x3, bc, bc, dtt, alog_h, dsk_h)


def _ssd_gate_sample_body(y_ref, z_ref, ng_ref, o_ref):
    zf = z_ref[...].astype(F32)
    o_ref[...] = _rms(y_ref[...] * (zf * _sigmoid(zf)), ng_ref[...]).astype(BF16)


def _ssd_gate_sample(y, z, layer, ng):
    n = y.shape[0]
    return pl.pallas_call(
        _ssd_gate_sample_body,
        grid=(1,),
        in_specs=[_full_spec(y), _full_spec(z), _layer_spec(ng, layer)],
        out_specs=pl.BlockSpec((n, SSD_INNER), lambda i: (0, 0)),
        out_shape=jax.ShapeDtypeStruct((n, SSD_INNER), BF16),
        compiler_params=_params("arbitrary"),
        name="ssd_gate_sample",
    )(y, z, ng)


def _cmul(a, b):
    return a[0] * b[0] - a[1] * b[1], a[0] * b[1] + a[1] * b[0]


def _s5_discretise(lr, li, dl):
    delta = jnp.exp(dl)
    mag = jnp.exp(lr * delta)
    ab = (mag * jnp.cos(li * delta), mag * jnp.sin(li * delta))
    nr, ni = ab[0] - 1.0, ab[1]
    den = lr * lr + li * li
    return ab, ((nr * lr + ni * li) / den, (ni * lr - nr * li) / den)


def _s5_powers(ab, n):
    pw = [(jnp.ones_like(ab[0]), jnp.zeros_like(ab[0]))]
    for _ in range(n):
        pw.append(_cmul(pw[-1], ab))
    return pw


def _s5_tables_body(lr_r, li_r, dl_r, btr_r, bti_r, cr_r, ci_r, lr_l, li_l, dl_l,
                    lr_t, li_t, dl_t, cr_t, ci_t, tw_ref, v_ref, v0_ref, sc_ref):
    tc, nl = S5_TC, LANES
    ab, f = _s5_discretise(lr_r[...], li_r[...], dl_r[...])
    bb = _cmul(f, (btr_r[...], bti_r[...]))
    pw = _s5_powers(ab, tc - 1)
    cc = (cr_r[...], ci_r[...])
    bbr, bbi = bb[0].astype(BF16), bb[1].astype(BF16)
    rg = lax.broadcasted_iota(jnp.int32, (nl, nl), 0) // S5_CH
    cg = lax.broadcasted_iota(jnp.int32, (nl, nl), 1) // S5_CH
    kblk = []
    for tau in range(tc):
        cp = _cmul(cc, pw[tau])
        k = _dot_nt(bbr, cp[0].astype(BF16)) - _dot_nt(bbi, cp[1].astype(BF16))
        kblk.append(jnp.where(rg == cg, k, 0.0).astype(BF16))
    zero = jnp.zeros((nl, nl), BF16)
    for j in range(tc):
        for i in range(tc):
            tw_ref[j * nl:(j + 1) * nl, i * nl:(i + 1) * nl] = kblk[i - j] if i >= j else zero
    ab_l, _ = _s5_discretise(lr_l[...], li_l[...], dl_l[...])
    pw_l = _s5_powers(ab_l, tc)
    pair = lambda x: jnp.concatenate([x, x], axis=1)
    widen = lambda x: jnp.concatenate([pair(x)] * (S5_BLK // (2 * S5_STATE)), axis=1)
    bwr, bwi = widen(bb[0]), widen(bb[1])
    wrow = lax.broadcasted_iota(jnp.int32, (nl, S5_BLK), 0) // S5_CH
    wcol = lax.broadcasted_iota(jnp.int32, (nl, S5_BLK), 1) // S5_STATE
    wmask = wrow == wcol
    for j in range(tc):
        pr, pi = pw_l[tc - 1 - j]
        tw_ref[j * nl:(j + 1) * nl, tc * nl:tc * nl + S5_BLK] = jnp.where(wmask, bwr * pr - bwi * pi, 0.0).astype(BF16)
        tw_ref[j * nl:(j + 1) * nl, tc * nl + S5_BLK:] = jnp.where(wmask, bwr * pi + bwi * pr, 0.0).astype(BF16)
    a8 = pw_l[tc]
    a16 = _cmul(a8, a8)
    a32 = _cmul(a16, a16)
    rows = [ab_l, a16, a32, a8]
    for _ in range(tc - 1):
        rows.append(_cmul(rows[-1], a8))
    sc_ref[...] = jnp.zeros_like(sc_ref)
    for k, rw in enumerate(rows):
        sc_ref[0, k:k + 1, :] = rw[0]
        sc_ref[1, k:k + 1, :] = rw[1]
    ab_t, _ = _s5_discretise(lr_t[...], li_t[...], dl_t[...])
    pw_t = _s5_powers(ab_t, tc)
    cc_t = (cr_t[...], ci_t[...])
    vrow = lax.broadcasted_iota(jnp.int32, (S5_BLK, nl), 0) // S5_STATE
    vcol = lax.broadcasted_iota(jnp.int32, (S5_BLK, nl), 1) // S5_CH
    vmask = vrow == vcol
    tall = lambda x: jnp.concatenate([x] * (S5_BLK // S5_STATE), axis=0)
    for tau in range(tc + 1):
        cp = _cmul(cc_t, pw_t[tau])
        vr = jnp.where(vmask, tall(cp[0]), 0.0).astype(BF16)
        vi = jnp.where(vmask, tall(-cp[1]), 0.0).astype(BF16)
        if tau == 0:
            v0_ref[0:S5_BLK, :] = vr
            v0_ref[S5_BLK:, :] = vi
        else:
            v_ref[0:S5_BLK, (tau - 1) * nl:tau * nl] = vr
            v_ref[S5_BLK:, (tau - 1) * nl:tau * nl] = vi


def _s5_tables(a_re, a_im, log_dt, b_re, b_im, c_re, c_im, d_skip):
    g, s, c, gb = S5_GROUPS, S5_STATE, S5_CH, S5_GB
    ldt = jnp.broadcast_to(log_dt[:, :, None], (DEPTH, g, s))
    rows = lambda a: jnp.repeat(a, c, axis=1)
    flat = lambda a: a.reshape(DEPTH, gb, 1, S5_BLK)
    cols = lambda a: jnp.repeat(jnp.swapaxes(a, 1, 2), c, axis=2)
    by_rows = [rows(a_re), rows(a_im), rows(ldt),
               jnp.swapaxes(b_re, 2, 3).reshape(DEPTH, g * c, s), jnp.swapaxes(b_im, 2, 3).reshape(DEPTH, g * c, s),
               c_re.reshape(DEPTH, g * c, s), c_im.reshape(DEPTH, g * c, s)]
    by_lane = [flat(a_re), flat(a_im), flat(ldt)]
    by_col = [cols(a_re), cols(a_im), cols(ldt),
              jnp.transpose(c_re, (0, 3, 1, 2)).reshape(DEPTH, s, g * c),
              jnp.transpose(c_im, (0, 3, 1, 2)).reshape(DEPTH, s, g * c)]
    in_specs = ([pl.BlockSpec((None, LANES, s), lambda l, b: (l, b, 0))] * len(by_rows)
                + [pl.BlockSpec((None, None, 1, S5_BLK), lambda l, b: (l, b, 0, 0))] * len(by_lane)
                + [pl.BlockSpec((None, s, LANES), lambda l, b: (l, 0, b))] * len(by_col))
    blk = lambda *shape: pl.BlockSpec((None, None) + shape, lambda l, b: (l, b) + (0,) * len(shape))
    tcw = S5_TC * LANES
    tw, v_blk, v0_blk, scan = pl.pallas_call(
        _s5_tables_body,
        grid=(DEPTH, gb),
        in_specs=in_specs,
        out_specs=[blk(tcw, tcw + 2 * S5_BLK), blk(2 * S5_BLK, tcw), blk(2 * S5_BLK, LANES), blk(2, 16, S5_BLK)],
        out_shape=[jax.ShapeDtypeStruct((DEPTH, gb, tcw, tcw + 2 * S5_BLK), BF16),
                   jax.ShapeDtypeStruct((DEPTH, gb, 2 * S5_BLK, tcw), BF16),
                   jax.ShapeDtypeStruct((DEPTH, gb, 2 * S5_BLK, LANES), BF16),
                   jax.ShapeDtypeStruct((DEPTH, gb, 2, 16, S5_BLK), F32)],
        compiler_params=_params("arbitrary", "arbitrary"),
        name="s5_tables",
    )(*by_rows, *by_lane, *by_col)
    return tw, v_blk, v0_blk, scan, d_skip.reshape(DEPTH, gb, 1, LANES)


def _s5_prompt_body(u_ref, tw_ref, v_ref, sc_ref, d_ref, y_ref, hout_ref,
                    sre_ref, sim_ref, hin_ref, cr_ref, ci_ref):
    r = pl.program_id(2)
    tr = sre_ref.shape[0]

    @pl.when(r == 0)
    def _():
        cr_ref[...] = jnp.zeros_like(cr_ref)
        ci_ref[...] = jnp.zeros_like(ci_ref)

    u = jnp.concatenate([u_ref[j] for j in range(S5_TC)], axis=1)
    tw = _dot(u, tw_ref[...])
    yd = tw[:, 0:S5_TC * 128]
    sre_ref[...] = tw[:, S5_TC * 128:S5_TC * 128 + S5_BLK]
    sim_ref[...] = tw[:, S5_TC * 128 + S5_BLK:]

    a8 = (sc_ref[0, 3:4, :], sc_ref[1, 3:4, :])
    a16 = (sc_ref[0, 1:2, :], sc_ref[1, 1:2, :])
    a32 = (sc_ref[0, 2:3, :], sc_ref[1, 2:3, :])
    apw = (sc_ref[0, 3:11, :], sc_ref[1, 3:11, :])
    row = lax.broadcasted_iota(jnp.int32, (8, S5_BLK), 0)

    sub = lax.broadcasted_iota(jnp.int32, (tr, S5_BLK), 0) % 8
    xr, xi = sre_ref[...], sim_ref[...]
    for k, (ar, ai) in ((1, a8), (2, a16), (4, a32)):
        pr = jnp.where(sub >= k, pltpu.roll(xr, k, axis=0), 0.0)
        pi = jnp.where(sub >= k, pltpu.roll(xi, k, axis=0), 0.0)
        xr, xi = xr + ar * pr - ai * pi, xi + ar * pi + ai * pr
    sre_ref[...] = xr
    sim_ref[...] = xi

    def tile(t, carry):
        c_re, c_im = carry
        rows = pl.ds(pl.multiple_of(t * 8, 8), 8)
        xr, xi = sre_ref[rows, :], sim_ref[rows, :]
        xr = xr + apw[0] * c_re - apw[1] * c_im
        xi = xi + apw[0] * c_im + apw[1] * c_re
        hin_ref[rows, 0:S5_BLK] = jnp.where(row >= 1, pltpu.roll(xr, 1, axis=0), c_re)
        hin_ref[rows, S5_BLK:] = jnp.where(row >= 1, pltpu.roll(xi, 1, axis=0), c_im)
        return xr[7:8, :], xi[7:8, :]

    c_re, c_im = lax.fori_loop(0, tr // 8, tile, (cr_ref[...], ci_ref[...]))
    cr_ref[...] = c_re
    ci_ref[...] = c_im
    hout_ref[:, 0:S5_BLK] = c_re
    hout_ref[:, S5_BLK:] = c_im

    d_row = jnp.concatenate([d_ref[...]] * S5_TC, axis=1)
    y = yd + _dot(hin_ref[...].astype(BF16), v_ref[...]) + d_row * u.astype(F32)
    for j in range(S5_TC):
        y_ref[j] = y[:, j * 128:(j + 1) * 128].astype(BF16)


def _s5_prompt(u8, layer, n, length, tw, v_blk, scan, d_blk, tr=512):
    rows = length // S5_TC
    tr = min(tr, rows)
    nr = rows // tr
    chunk_blk = pl.BlockSpec((S5_TC, tr, 128), lambda gb, b, r: (0, b * nr + r, gb))
    lay = lambda a: pl.BlockSpec((None, None) + tuple(a.shape[2:]),
                                 lambda gb, b, r: (layer, gb) + (0,) * (a.ndim - 2))
    y8, hout = pl.pallas_call(
        _s5_prompt_body,
        grid=(S5_GB, n, nr),
        in_specs=[chunk_blk, lay(tw), lay(v_blk), lay(scan), lay(d_blk)],
        out_specs=[chunk_blk,
                   pl.BlockSpec((None, None, 1, 2 * S5_BLK), lambda gb, b, r: (b, gb, 0, 0))],
        out_shape=[jax.ShapeDtypeStruct((S5_TC, n * rows, S5_W), BF16),
                   jax.ShapeDtypeStruct((n, S5_GB, 1, 2 * S5_BLK), F32)],
        scratch_shapes=[pltpu.VMEM((tr, S5_BLK), F32), pltpu.VMEM((tr, S5_BLK), F32),
                        pltpu.VMEM((tr, 2 * S5_BLK), F32),
                        pltpu.VMEM((1, S5_BLK), F32), pltpu.VMEM((1, S5_BLK), F32)],
        compiler_params=_params("arbitrary", "arbitrary", "arbitrary"),
        name="s5_prompt",
    )(u8, tw, v_blk, scan, d_blk)
    state = jnp.transpose(hout.reshape(n, S5_GB, 2, 8, S5_STATE), (0, 2, 1, 3, 4))
    return y8, state.reshape(n, 2, S5_GROUPS, S5_STATE)


def _s5_sample_body(u_ref, hr_ref, hi_ref, w_ref, v0_ref, sc_ref, d_ref, y_ref, nr_ref, ni_ref):
    u = u_ref[...]
    bu = _dot(u, w_ref[...])
    ar, ai = sc_ref[0, 0:1, :], sc_ref[1, 0:1, :]
    hr, hi = hr_ref[...], hi_ref[...]
    nr = ar * hr - ai * hi + bu[:, 0:S5_BLK]
    ni = ar * hi + ai * hr + bu[:, S5_BLK:]
    nr_ref[...] = nr
    ni_ref[...] = ni
    hcat = jnp.concatenate([nr, ni], axis=1).astype(BF16)
    y_ref[...] = (_dot(hcat, v0_ref[...]) + d_ref[...] * u.astype(F32)).astype(BF16)


def _s5_sample(u, state, layer, tw, v0_blk, scan, d_blk):
    n = u.shape[0]
    lay = lambda a: pl.BlockSpec((None, None) + tuple(a.shape[2:]),
                                 lambda gb: (layer, gb) + (0,) * (a.ndim - 2))
    st = lambda off: pl.BlockSpec((None, n, S5_BLK), lambda gb: (layer, 0, off + gb))
    half = S5_GROUPS * S5_STATE
    y, nr, ni = pl.pallas_call(
        _s5_sample_body,
        grid=(S5_GB,),
        in_specs=[pl.BlockSpec((n, 128), lambda gb: (0, gb)), st(0), st(S5_GB),
                  pl.BlockSpec((None, None, 128, 2 * S5_BLK), lambda gb: (layer, gb, S5_TC - 1, 1)),
                  lay(v0_blk), lay(scan), lay(d_blk)],
        out_specs=[pl.BlockSpec((n, 128), lambda gb: (0, gb)),
                   pl.BlockSpec((n, S5_BLK), lambda gb: (0, gb)),
                   pl.BlockSpec((n, S5_BLK), lambda gb: (0, gb))],
        out_shape=[jax.ShapeDtypeStruct((n, S5_W), BF16), jax.ShapeDtypeStruct((n, half), F32),
                   jax.ShapeDtypeStruct((n, half), F32)],
        compiler_params=_params("arbitrary"),
        name="s5_sample",
    )(u, state, state, tw, v0_blk, scan, d_blk)
    return y, jnp.stack([nr, ni], axis=1).reshape(n, 2, S5_GROUPS, S5_STATE)


def kernel(x_prompt, x_sample, cache_kv_w128, cache_kv_w512, cache_kv_w2048, state_ssd, state_conv, state_s5, p_prompt, p_sample, attn_rel_bias, ffn1_norm, ffn1_w_gate, ffn1_w_up, ffn1_w_down, mix_norm, w_in, ssd_conv_w, ssd_conv_b, ssd_dt_bias, ssd_a_log, ssd_d, ssd_norm, s5_a_re, s5_a_im, s5_b_re, s5_b_im, s5_c_re, s5_c_im, s5_d, s5_log_dt, w_s5_glu, w_branch_attn, w_branch_ssd, w_out, ffn2_norm, ffn2_w_gate, ffn2_w_up, ffn2_w_down, pe_norm, w_pe_gate, w_pe_proj, final_norm):
    n_p, len_p, _ = x_prompt.shape
    n_s = x_sample.shape[0]
    m_p = n_p * len_p
    assert x_sample.shape[1] == 1 and len_p % (QBLK * ATTN_PATTERNS[-1][1]) == 0
    caches = (cache_kv_w128, cache_kv_w512, cache_kv_w2048)
    dils = tuple(d for _, d in ATTN_PATTERNS)
    for cache, (w, d) in zip(caches, ATTN_PATTERNS):
        assert cache.shape[3] == w == QBLK * d
    caches_t = [jnp.transpose(c, (0, 1, 2, 4, 5, 3)) for c in caches]
    sample_bs = (16, 8, 2)

    bf = lambda a: a.astype(BF16)
    vec = lambda a: a.reshape(DEPTH, 1, a.shape[-1])
    w1g, w1u, w1d = bf(ffn1_w_gate), bf(ffn1_w_up), bf(ffn1_w_down)
    w2g, w2u, w2d = bf(ffn2_w_gate), bf(ffn2_w_up), bf(ffn2_w_down)
    mid_hi = QKV_W + MID_DT + SSD_HEADS
    w_qkv = jnp.swapaxes(w_in[..., :QKV_W].reshape(DEPTH, D_MODEL, 3, 3, GROUP_W), 2, 3)
    w_in_p = (bf(w_qkv.reshape(DEPTH, D_MODEL, QKV_W)),
              bf(jnp.pad(w_in[..., QKV_W:mid_hi], ((0, 0), (0, 0), (0, DT_PAD - SSD_HEADS)))),
              bf(w_in[..., mid_hi:]))
    wba, wbs, wglu, wout = bf(w_branch_attn), bf(w_branch_ssd), bf(w_s5_glu), bf(w_out)
    wpg, wpp = bf(w_pe_gate), bf(w_pe_proj)
    n1, nm, n2, npe = vec(ffn1_norm), vec(mix_norm), vec(ffn2_norm), vec(pe_norm)
    fin = final_norm.reshape(1, D_MODEL)
    cb = vec(ssd_conv_b)
    lane_pad = lambda a: jnp.pad(a, ((0, 0), (0, DT_PAD - a.shape[-1]))).reshape(DEPTH, 1, DT_PAD)
    dtb, alog = lane_pad(ssd_dt_bias), lane_pad(ssd_a_log)
    dsk = jnp.repeat(ssd_d, SSD_P, axis=-1).reshape(DEPTH, 1, SSD_INNER)
    ng = vec(ssd_norm)
    per_head = lambda a: jnp.broadcast_to(a.reshape(DEPTH * SSD_HEADS, 1, 1), (DEPTH * SSD_HEADS, 1, n_s))
    alog_h, dsk_h = per_head(ssd_a_log), per_head(ssd_d)

    tw, v_blk, v0_blk, scan, d_blk = _s5_tables(s5_a_re, s5_a_im, s5_log_dt, s5_b_re, s5_b_im,
                                                s5_c_re, s5_c_im, s5_d)

    bias_p = [_prompt_bias(attn_rel_bias, g, d) for g, (_, d) in enumerate(ATTN_PATTERNS)]
    bias_s = [_sample_bias(attn_rel_bias, g, d) for g, (_, d) in enumerate(ATTN_PATTERNS)]

    pp = p_prompt.reshape(DEPTH, m_p, PE_DIM)
    psm = p_sample.reshape(DEPTH, n_s, PE_DIM)
    st_ssd_t = jnp.transpose(state_ssd, (0, 2, 3, 4, 1))
    st_s5 = state_s5.reshape(DEPTH, n_s, 2 * S5_GROUPS * S5_STATE)

    xp = x_prompt.reshape(m_p, D_MODEL)
    xs = x_sample.reshape(n_s, D_MODEL)
    tm_p, tm_f, tm_m, tm_s = 512, 512, 256, n_s
    kv_p, kv_s = [[], [], []], [[], [], []]
    ssd_p, ssd_s, conv_p, conv_s, s5_p, s5_s = [], [], [], [], [], []

    def kv_prompt(qg, d):
        n, _, rows, _ = qg.shape
        t = qg[:, :, rows - QBLK:, GROUP_W:].reshape(n, d, QBLK, 2, HPG, HEAD_DIM)
        return jnp.transpose(t, (0, 3, 2, 1, 4, 5)).reshape(n, 2, QBLK * d, HPG, HEAD_DIM)

    for i in range(DEPTH):
        last = i == DEPTH - 1
        xp = _ffn(xp, i, tm_f, n1, w1g, w1u, w1d)
        *qgs, z, xbc, dt, u, gates = _inproj(xp, i, tm_p, nm, w_in_p, n_p, len_p, dils, True)
        parts = []
        for g, d in enumerate(dils):
            parts += list(_attn_prompt(qgs[g], bias_p[g], g))
            kv_p[g].append(kv_prompt(qgs[g], d))
        y_ssd, st = _ssd_prompt(xbc, z, dt, i, n_p, len_p, ssd_conv_w, cb, dtb, alog, dsk, ng)
        ssd_p.append(jnp.swapaxes(st, 2, 3))
        conv_p.append(xbc.reshape(n_p, len_p, SSD_CONV_CH)[:, len_p - 3:])
        y_s5, st5 = _s5_prompt(u, i, n_p, len_p, tw, v_blk, scan, d_blk)
        s5_p.append(st5)
        xp = _merge(xp, i, tm_m, parts, dils, len_p, y_ssd.reshape(m_p, SSD_INNER), y_s5, gates,
                    wba, wbs, wglu, wout, (n2, w2g, w2u, w2d), (pp, npe, wpg, wpp), fin if last else None)
        xs = _ffn(xs, i, tm_s, n1, w1g, w1u, w1d)
        *qgs, z, xbc, dt, u, gates = _inproj(xs, i, tm_s, nm, w_in_p, 1, n_s, (1, 1, 1), False)
        parts = []
        for g in range(len(dils)):
            qg = qgs[g].reshape(n_s, ATTN_W)
            o_s, l_s = _attn_sample(qg, caches_t[g], i, bias_s[g], g, sample_bs[g])
            parts += [o_s.reshape(1, 1, n_s, GROUP_W), l_s.reshape(1, 1, n_s, GROUP_W)]
            kv_s[g].append(qg[:, GROUP_W:].reshape(n_s, 2, 1, HPG, HEAD_DIM))
        xc, dts = _ssd_conv_sample(xbc, state_conv, dt, i, ssd_conv_w, cb, dtb)
        conv_s.append(jnp.concatenate([state_conv[i][:, 1:], xbc[:, None, :]], axis=1))
        xct = xc.T
        x3 = xct[:SSD_INNER].reshape(SSD_HEADS, SSD_P, 1, n_s)
        bc = xct[SSD_INNER:].reshape(2 * SSD_GROUPS, SSD_S, n_s)
        dtt = dts.T[:SSD_HEADS].reshape(SSD_HEADS, 1, n_s)
        yh, hn = _ssd_state_sample(st_ssd_t, i, x3, bc, dtt, alog_h, dsk_h)
        ssd_s.append(jnp.transpose(hn, (3, 0, 1, 2)))
        y_ssd = _ssd_gate_sample(yh.reshape(SSD_INNER, n_s).T, z, i, ng)
        y_s5, st5 = _s5_sample(u, st_s5, i, tw, v0_blk, scan, d_blk)
        s5_s.append(st5)
        xs = _merge(xs, i, tm_s, parts, (1, 1, 1), n_s, y_ssd, y_s5, gates, wba, wbs, wglu, wout,
                    (n2, w2g, w2u, w2d), (psm, npe, wpg, wpp), fin if last else None)

    return (xp.reshape(n_p, len_p, D_MODEL), xs.reshape(n_s, 1, D_MODEL),
            jnp.stack(kv_p[0], 0), jnp.stack(kv_s[0], 0),
            jnp.stack(kv_p[1], 0), jnp.stack(kv_s[1], 0),
            jnp.stack(kv_p[2], 0), jnp.stack(kv_s[2], 0),
            jnp.stack(ssd_p, 0), jnp.stack(ssd_s, 0),
            jnp.stack(conv_p, 0), jnp.stack(conv_s, 0),
            jnp.stack(s5_p, 0), jnp.stack(s5_s, 0))
```

```python
import functools
import math

import numpy as np
import jax
import jax.numpy as jnp
from jax import lax
from jax.experimental import pallas as pl
from jax.experimental.pallas import tpu as pltpu

F32 = jnp.float32
BF16 = jnp.bfloat16

D_MODEL = 1024
DEPTH = 4
HEAD_DIM = 64
ATTN_PATTERNS = ((128, 1), (512, 4), (2048, 16))
HPG = 4
GROUP_W = HPG * HEAD_DIM
ATTN_W = 3 * GROUP_W
QBLK = 128
N_BUCKETS = 32
BUCKET_MAX_DIST = 2048
NEG_INF = -1e30
SSD_HEADS = 12
SSD_P = 64
SSD_S = 64
SSD_GROUPS = 4
SSD_INNER = SSD_HEADS * SSD_P
SSD_CONV_CH = SSD_INNER + 2 * SSD_GROUPS * SSD_S
SSD_CHUNK = 256
S5_GROUPS = 48
S5_CH = 16
S5_STATE = 64
S5_W = S5_GROUPS * S5_CH
S5_TC = 8
S5_GB = 6
S5_BLK = 8 * S5_STATE
D_FF = 2816
PE_DIM = 256
RMS_EPS = 1e-6
LANES = 128
DT_PAD = LANES
QKV_W = 3 * ATTN_W
MID_XBC = SSD_INNER
MID_DT = MID_XBC + SSD_CONV_CH
MID_W = MID_DT + DT_PAD
REST_G = S5_W
REST_W = REST_G + 3 * D_MODEL
VMEM_LIMIT = 56 * 1024 * 1024


def _params(*sem):
    return pltpu.CompilerParams(dimension_semantics=sem, vmem_limit_bytes=VMEM_LIMIT)


def _rms(x, g):
    inv = lax.rsqrt(jnp.mean(x * x, axis=-1, keepdims=True) + RMS_EPS)
    return (x * inv) * g


def _dot(a, b):
    return jnp.dot(a, b, preferred_element_type=F32)


def _dot_nt(a, b):
    return lax.dot_general(a, b, (((1,), (1,)), ((), ())), preferred_element_type=F32)


def _dot_tn(a, b):
    return lax.dot_general(a, b, (((0,), (0,)), ((), ())), preferred_element_type=F32)


def _sigmoid(x):
    return jax.nn.sigmoid(x)


def _layer_spec(arr, layer):
    nd = arr.ndim - 1
    return pl.BlockSpec((None,) + tuple(arr.shape[1:]),
                        lambda *_: (layer,) + (0,) * nd,
                        pipeline_mode=pl.Buffered(1))


def _full_spec(arr):
    nd = arr.ndim
    return pl.BlockSpec(tuple(arr.shape), lambda *_: (0,) * nd, pipeline_mode=pl.Buffered(1))


def _ffn_body(*refs, has_pe, has_final):
    it = iter(refs)
    x_ref, g_ref, wg_ref, wu_ref, wd_ref = (next(it) for _ in range(5))
    if has_pe:
        p_ref, pn_ref, wpg_ref, wpp_ref = (next(it) for _ in range(4))
    if has_final:
        fn_ref = next(it)
    o_ref = next(it)
    act_ref = next(it)
    o_ref[...] = _ffn_math(x_ref[...], g_ref, wg_ref, wu_ref, wd_ref, act_ref,
                           (p_ref, pn_ref, wpg_ref, wpp_ref) if has_pe else None,
                           fn_ref if has_final else None)


def _ffn_math(x, g_ref, wg_ref, wu_ref, wd_ref, act_ref, pe_refs, fn_ref):
    h = _rms(x, g_ref[...]).astype(BF16)
    half = D_FF // 2
    for f0 in (0, half):
        gate = _dot(h, wg_ref[:, f0:f0 + half])
        up = _dot(h, wu_ref[:, f0:f0 + half])
        act_ref[:, f0:f0 + half] = (gate * _sigmoid(gate) * up).astype(BF16)
    y = x + 0.5 * _dot(act_ref[...], wd_ref[...])
    if pe_refs is not None:
        p_ref, pn_ref, wpg_ref, wpp_ref = pe_refs
        h2 = _rms(y, pn_ref[...]).astype(BF16)
        gt = _sigmoid(_dot(h2, wpg_ref[...]))
        y = y + gt * _dot(p_ref[...].astype(BF16), wpp_ref[...])
    if fn_ref is not None:
        y = _rms(y, fn_ref[...])
    return y


def _ffn(x, layer, tm, norm, wg, wu, wd, pe=None, final=None):
    m = x.shape[0]
    row = lambda i: (i, 0)
    args = [x, norm, wg, wu, wd]
    specs = [pl.BlockSpec((tm, D_MODEL), row), _layer_spec(norm, layer), _layer_spec(wg, layer),
             _layer_spec(wu, layer), _layer_spec(wd, layer)]
    if pe is not None:
        p_all, pn, wpg, wpp = pe
        args += [p_all, pn, wpg, wpp]
        specs += [pl.BlockSpec((None, tm, PE_DIM), lambda i: (layer, i, 0)), _layer_spec(pn, layer),
                  _layer_spec(wpg, layer), _layer_spec(wpp, layer)]
    if final is not None:
        args.append(final)
        specs.append(_full_spec(final))
    return pl.pallas_call(
        functools.partial(_ffn_body, has_pe=pe is not None, has_final=final is not None),
        grid=(m // tm,),
        in_specs=specs,
        out_specs=pl.BlockSpec((tm, D_MODEL), row),
        out_shape=jax.ShapeDtypeStruct((m, D_MODEL), F32),
        scratch_shapes=[pltpu.VMEM((tm, D_FF), BF16)],
        compiler_params=_params("arbitrary"),
        name="ffn",
    )(*args)


def _inproj_body(x_ref, g_ref, wq_ref, wm_ref, wr_ref, qa_ref, qb_ref, qc_ref, z_ref, xbc_ref, dt_ref, u_ref,
                 gates_ref, stage_ref, *, dils, prompt):
    h = _rms(x_ref[...], g_ref[...]).astype(BF16)
    tm = x_ref.shape[0]
    for g, (out_ref, d) in enumerate(zip((qa_ref, qb_ref, qc_ref), dils)):
        res = _dot(h, wq_ref[:, g * ATTN_W:(g + 1) * ATTN_W])
        if d == 1:
            out_ref[0] = res
        else:
            for c in range(ATTN_W // LANES):
                stage_ref[c] = res[:, c * LANES:(c + 1) * LANES]
            for r in range(d):
                for c in range(ATTN_W // LANES):
                    out_ref[r, :, c * LANES:(c + 1) * LANES] = stage_ref[c, pl.ds(r, tm // d, stride=d), :]
    z_ref[...] = _dot(h, wm_ref[:, 0:MID_XBC]).astype(BF16)
    xbc_ref[...] = _dot(h, wm_ref[:, MID_XBC:MID_DT])
    dt_ref[...] = _dot(h, wm_ref[:, MID_DT:MID_W])
    u = _dot(h, wr_ref[:, 0:REST_G])
    if prompt:
        for c in range(S5_W // LANES):
            stage_ref[c] = u[:, c * LANES:(c + 1) * LANES]
        for j in range(S5_TC):
            for c in range(S5_W // LANES):
                u_ref[j, :, c * LANES:(c + 1) * LANES] = (
                    stage_ref[c, pl.ds(j, tm // S5_TC, stride=S5_TC), :].astype(BF16))
    else:
        u_ref[...] = u.astype(BF16)
    for k in range(3):
        c0 = REST_G + k * D_MODEL
        gates_ref[:, k * D_MODEL:(k + 1) * D_MODEL] = _sigmoid(_dot(h, wr_ref[:, c0:c0 + D_MODEL])).astype(BF16)


def _inproj(x, layer, tm, norm, w, n, length, dils, prompt):
    m = x.shape[0]
    tps = length // tm
    row = lambda i: (i, 0)
    q_specs = [pl.BlockSpec((None, d, tm // d, ATTN_W), lambda i: (i // tps, 0, i % tps, 0)) for d in dils]
    q_shapes = [jax.ShapeDtypeStruct((n, d, length // d, ATTN_W), F32) for d in dils]
    flat = lambda wd, dt: (pl.BlockSpec((tm, wd), row), jax.ShapeDtypeStruct((m, wd), dt))
    if prompt:
        u_out = (pl.BlockSpec((S5_TC, tm // S5_TC, S5_W), lambda i: (0, i, 0)),
                 jax.ShapeDtypeStruct((S5_TC, m // S5_TC, S5_W), BF16))
    else:
        u_out = flat(S5_W, BF16)
    rest = [flat(SSD_INNER, BF16), flat(SSD_CONV_CH, F32), flat(DT_PAD, F32), u_out, flat(3 * D_MODEL, BF16)]
    return pl.pallas_call(
        functools.partial(_inproj_body, dils=dils, prompt=prompt),
        grid=(m // tm,),
        in_specs=[pl.BlockSpec((tm, D_MODEL), row), _layer_spec(norm, layer)] + [_layer_spec(a, layer) for a in w],
        out_specs=q_specs + [s for s, _ in rest],
        out_shape=q_shapes + [s for _, s in rest],
        scratch_shapes=[pltpu.VMEM((ATTN_W // LANES, tm, LANES), F32)],
        compiler_params=_params("arbitrary"),
        name="inproj",
    )(x, norm, *w)


def _merge_body(x_ref, o0, l0, o1, l1, o2, l2, yssd_ref, ys5_ref, gates_ref,
                wba_ref, wbs_ref, wglu_ref, wout_ref, g_ref, wg_ref, wu_ref, wd_ref, p_ref, pn_ref, wpg_ref, wpp_ref,
                *rest, dils, s5_chunked, has_final):
    fn_ref = rest[0] if has_final else None
    out_ref, stage_ref, s5_stage_ref, act_ref = rest[-4:]
    tm = x_ref.shape[0]
    parts = []
    for k, (ref, d) in enumerate(zip((o0, l0, o1, l1, o2, l2), (dils[0], dils[0], dils[1], dils[1], dils[2], dils[2]))):
        if d == 1:
            parts.append(ref[0])
        else:
            nl = GROUP_W // LANES
            for r in range(d):
                for c in range(nl):
                    stage_ref[k * nl + c, pl.ds(r, tm // d, stride=d), :] = ref[r, :, c * LANES:(c + 1) * LANES]
            parts.append(jnp.concatenate([stage_ref[k * nl + c] for c in range(nl)], axis=1))
    oa, la, ob, lb, oc, lc = parts
    mx = jnp.maximum(jnp.maximum(la, lb), lc)
    ea, eb, ec = jnp.exp(la - mx), jnp.exp(lb - mx), jnp.exp(lc - mx)
    attn = (ea * oa + eb * ob + ec * oc) / (ea + eb + ec)
    acc = gates_ref[:, 0:D_MODEL].astype(F32) * _dot(attn.astype(BF16), wba_ref[...])
    acc = acc + gates_ref[:, D_MODEL:2 * D_MODEL].astype(F32) * _dot(yssd_ref[...], wbs_ref[...])
    if s5_chunked:
        nl6 = S5_W // LANES
        for j in range(S5_TC):
            for c in range(nl6):
                s5_stage_ref[c, pl.ds(j, tm // S5_TC, stride=S5_TC), :] = (
                    ys5_ref[j, :, c * LANES:(c + 1) * LANES].astype(F32))
        ys5 = jnp.concatenate([s5_stage_ref[c] for c in range(nl6)], axis=1)
    else:
        ys5 = ys5_ref[...].astype(F32)
    gl = _dot(jax.nn.gelu(ys5).astype(BF16), wglu_ref[...])
    s5 = gl[:, :D_MODEL] * _sigmoid(gl[:, D_MODEL:])
    acc = acc + gates_ref[:, 2 * D_MODEL:].astype(F32) * s5
    x_mid = x_ref[...] + _dot(acc.astype(BF16), wout_ref[...])
    out_ref[...] = _ffn_math(x_mid, g_ref, wg_ref, wu_ref, wd_ref, act_ref,
                             (p_ref, pn_ref, wpg_ref, wpp_ref), fn_ref)


def _merge(x, layer, tm, attn_parts, dils, length, y_ssd, y_s5, gates, wba, wbs, wglu, wout, ffn, pe, final):
    m = x.shape[0]
    tps = length // tm
    row = lambda i: (i, 0)
    s5_chunked = y_s5.ndim == 3
    args = [x] + list(attn_parts) + [y_ssd, y_s5, gates, wba, wbs, wglu, wout] + list(ffn) + list(pe)
    part_specs = []
    for d in dils:
        part_specs += [pl.BlockSpec((None, d, tm // d, GROUP_W), lambda i: (i // tps, 0, i % tps, 0))] * 2
    s5_spec = (pl.BlockSpec((S5_TC, tm // S5_TC, S5_W), lambda i: (0, i, 0)) if s5_chunked
               else pl.BlockSpec((tm, S5_W), row))
    specs = ([pl.BlockSpec((tm, D_MODEL), row)] + part_specs
             + [pl.BlockSpec((tm, SSD_INNER), row), s5_spec, pl.BlockSpec((tm, 3 * D_MODEL), row)]
             + [_layer_spec(w, layer) for w in (wba, wbs, wglu, wout)]
             + [_layer_spec(w, layer) for w in ffn]
             + [pl.BlockSpec((None, tm, PE_DIM), lambda i: (layer, i, 0))]
             + [_layer_spec(w, layer) for w in pe[1:]])
    if final is not None:
        args.append(final)
        specs.append(_full_spec(final))
    return pl.pallas_call(
        functools.partial(_merge_body, dils=dils, s5_chunked=s5_chunked, has_final=final is not None),
        grid=(m // tm,),
        in_specs=specs,
        out_specs=pl.BlockSpec((tm, D_MODEL), row),
        out_shape=jax.ShapeDtypeStruct((m, D_MODEL), F32),
        scratch_shapes=[pltpu.VMEM((6 * GROUP_W // LANES, tm, LANES), F32),
                        pltpu.VMEM((S5_W // LANES, tm, LANES), F32),
                        pltpu.VMEM((tm, D_FF), BF16)],
        compiler_params=_params("arbitrary"),
        name="merge_ffn",
    )(*args)


def _t5_bucket(dist):
    max_exact = N_BUCKETS // 2
    d = np.asarray(dist).astype(np.int32)
    df = np.maximum(d, 1).astype(np.float32)
    large = max_exact + (np.log(df / max_exact) / math.log(BUCKET_MAX_DIST / max_exact)
                         * (N_BUCKETS - max_exact)).astype(np.int32)
    return np.where(d < max_exact, d, np.minimum(large, N_BUCKETS - 1)).astype(np.int32)


def _prompt_bias(rel_bias, g, d):
    w = 2 * QBLK
    tab = rel_bias[:, g * HPG:(g + 1) * HPG].astype(F32)
    near = jnp.transpose(tab[_t5_bucket(np.arange(QBLK, -1, -1) * d)], (1, 0))
    row0 = jnp.concatenate([near, jnp.full((HPG, w - QBLK - 1), NEG_INF, F32)], axis=1)
    x = jnp.concatenate([row0, row0, jnp.full((HPG, 1), NEG_INF, F32)], axis=1)
    flat = jnp.tile(x, (1, QBLK))[:, :QBLK * 2 * w]
    return flat.reshape(HPG, QBLK, 2 * w)[:, :, w:]


def _attn_prompt_body(q_ref, kp_ref, kc_ref, vp_ref, vc_ref, bias_ref, o_ref, l_ref):
    nsub = q_ref.shape[0] // QBLK
    first = pl.program_id(2) == 0
    lane = lax.broadcasted_iota(jnp.int32, (1, 2 * QBLK), 1)
    pen = jnp.where(jnp.logical_and(first, lane < QBLK), NEG_INF, 0.0).astype(F32)
    q = (q_ref[...] * (HEAD_DIM ** -0.5)).astype(BF16)
    k = jnp.concatenate([kp_ref[...], kc_ref[...]], axis=0).astype(BF16)
    v = jnp.concatenate([vp_ref[...], vc_ref[...]], axis=0).astype(BF16)
    for s in range(nsub):
        rq = slice(s * QBLK, (s + 1) * QBLK)
        rk = slice(s * QBLK, (s + 2) * QBLK)
        for h in range(HPG):
            sl = slice(h * HEAD_DIM, (h + 1) * HEAD_DIM)
            sc = _dot_nt(q[rq, sl], k[rk, sl]) + bias_ref[h]
            if s == 0:
                sc = sc + pen
            m = jnp.max(sc, axis=-1, keepdims=True)
            e = jnp.exp(sc - m)
            den = jnp.sum(e, axis=-1, keepdims=True)
            o_ref[rq, sl] = _dot(e.astype(BF16), v[rk, sl]) / den
            l_ref[rq, sl] = jnp.broadcast_to(m + jnp.log(den), (QBLK, HEAD_DIM))


def _attn_prompt(qkv_g, bias, g, tq=512):
    n, d, rows, _ = qkv_g.shape
    tq = min(tq, rows)
    nsub = tq // QBLK
    nb = rows // tq
    blk = (None, None, tq, GROUP_W)
    cur = lambda col: pl.BlockSpec(blk, lambda b, r, i: (b, r, i, col))
    prev = lambda col: pl.BlockSpec((None, None, QBLK, GROUP_W),
                                    lambda b, r, i: (b, r, jnp.maximum(i * nsub - 1, 0), col))
    out_spec = pl.BlockSpec(blk, lambda b, r, i: (b, r, i, 0))
    out_sds = jax.ShapeDtypeStruct((n, d, rows, GROUP_W), F32)
    return pl.pallas_call(
        _attn_prompt_body,
        grid=(n, d, nb),
        in_specs=[cur(0), prev(1), cur(1), prev(2), cur(2), _full_spec(bias)],
        out_specs=[out_spec, out_spec],
        out_shape=[out_sds, out_sds],
        compiler_params=_params("arbitrary", "arbitrary", "arbitrary"),
        name=f"attn_prompt_g{g}",
    )(qkv_g, qkv_g, qkv_g, qkv_g, qkv_g, bias)


def _sample_bias(rel_bias, g, d):
    steps = QBLK - np.arange(QBLK)
    tab = rel_bias[:, g * HPG:(g + 1) * HPG].astype(F32)
    on_grid = jnp.transpose(tab[_t5_bucket(steps * d)], (1, 0))[:, :, None]
    off_grid = jnp.full((HPG, QBLK, d - 1), NEG_INF, F32)
    cache_b = jnp.concatenate([on_grid, off_grid], axis=2).reshape(HPG, QBLK * d)
    self_b = jnp.broadcast_to(tab[0][:, None], (HPG, LANES))
    pad = lambda a: jnp.concatenate([a, jnp.zeros((8 - HPG, a.shape[1]), F32)], 0)
    return pad(cache_b), pad(self_b)


def _attn_sample_body(q_ref, kv_ref, cb_ref, sb_ref, o_ref, l_ref, *, bs):
    base = pl.program_id(0) * bs
    rowi = lax.broadcasted_iota(jnp.int32, (8, GROUP_W), 0)
    lane = lax.broadcasted_iota(jnp.int32, (8, GROUP_W), 1)
    own = (lane // HEAD_DIM) == rowi
    p = kv_ref.shape[-1]
    for j in range(bs):
        r = pl.ds(base + j, 1)
        q = q_ref[r, 0:GROUP_W] * (HEAD_DIM ** -0.5)
        kn = q_ref[r, GROUP_W:2 * GROUP_W]
        vn = q_ref[r, 2 * GROUP_W:3 * GROUP_W]
        qblk = jnp.where(own, jnp.broadcast_to(q, (8, GROUP_W)), 0.0)
        kt = kv_ref[j, 0].reshape(GROUP_W, p).astype(BF16)
        vt = kv_ref[j, 1].reshape(GROUP_W, p).astype(BF16)
        s = _dot(qblk.astype(BF16), kt) + cb_ref[...]
        s_self = jnp.sum(qblk * kn, axis=-1, keepdims=True) + sb_ref[:, 0:1]
        m = jnp.maximum(jnp.max(s, axis=-1, keepdims=True), s_self)
        e = jnp.exp(s - m)
        e_self = jnp.exp(s_self - m)
        den = jnp.sum(e, axis=-1, keepdims=True) + e_self
        o = (_dot_nt(e.astype(BF16), vt) + e_self * vn) / den
        o_ref[r, :] = jnp.sum(jnp.where(own, o, 0.0), axis=0, keepdims=True)
        lse = m + jnp.log(den)
        l_ref[r, :] = jnp.sum(jnp.where(own, lse, 0.0), axis=0, keepdims=True)


def _attn_sample(qkv_g, cache_t, layer, biases, g, bs):
    n = qkv_g.shape[0]
    p = cache_t.shape[-1]
    cb, sb = biases
    out_sds = jax.ShapeDtypeStruct((n, GROUP_W), F32)
    return pl.pallas_call(
        functools.partial(_attn_sample_body, bs=bs),
        grid=(n // bs,),
        in_specs=[_full_spec(qkv_g),
                  pl.BlockSpec((None, bs, 2, HPG, HEAD_DIM, p), lambda s: (layer, s, 0, 0, 0, 0)),
                  _full_spec(cb), _full_spec(sb)],
        out_specs=[pl.BlockSpec((n, GROUP_W), lambda s: (0, 0))] * 2,
        out_shape=[out_sds, out_sds],
        compiler_params=_params("arbitrary"),
        name=f"attn_sample_g{g}",
    )(qkv_g, cache_t, cb, sb)


def _softplus(x):
    return jnp.maximum(x, 0.0) + jnp.log(1.0 + jnp.exp(-jnp.abs(x)))


def _split3(x):
    hi = x.astype(BF16)
    r1 = x - hi.astype(F32)
    mid = r1.astype(BF16)
    lo = (r1 - mid.astype(F32)).astype(BF16)
    return hi, mid, lo


def _ssd_prompt_body(xbc_ref, z_ref, dt_ref, cw_ref, cb_ref, dtb_ref, alog_ref, dsk_ref, ng_ref,
                     y_ref, st_ref, xs_ref, state_ref, ybuf_ref):
    @pl.when(pl.program_id(1) == 0)
    def _():
        state_ref[...] = jnp.zeros_like(state_ref)
        xs_ref[:, 0:8, :] = jnp.zeros((xs_ref.shape[0], 8, SSD_CONV_CH), F32)

    for b in range(xbc_ref.shape[0]):
        _ssd_chunk(xbc_ref.at[b], z_ref.at[b], dt_ref.at[b], cw_ref, cb_ref, dtb_ref, alog_ref, dsk_ref, ng_ref,
                   y_ref.at[b], xs_ref.at[b], state_ref.at[b], ybuf_ref.at[b])
    st_ref[...] = state_ref[...]


def _ssd_chunk(xbc_ref, z_ref, dt_ref, cw_ref, cb_ref, dtb_ref, alog_ref, dsk_ref, ng_ref,
               y_ref, xs_ref, state_ref, ybuf_ref):
    q = SSD_CHUNK
    xs_ref[8:8 + q, :] = xbc_ref[...]
    full = xs_ref[...]
    conv = cb_ref[...] + cw_ref[3:4, :] * full[8:8 + q, :]
    for s in range(1, 4):
        conv = conv + cw_ref[3 - s:4 - s, :] * pltpu.roll(full, s, axis=0)[8:8 + q, :]
    xs_ref[0:8, :] = full[q:q + 8, :]
    xc = conv * _sigmoid(conv)

    dt = _softplus(dt_ref[...] + dtb_ref[...])
    a = -jnp.exp(alog_ref[...])
    da = dt * a
    ri = lax.broadcasted_iota(jnp.int32, (q, q), 0)
    ci = lax.broadcasted_iota(jnp.int32, (q, q), 1)
    causal = ri >= ci
    tril = jnp.where(causal, 1.0, 0.0).astype(BF16)
    cum = sum(_dot(tril, part) for part in _split3(da))
    cum_t = cum.T
    dt_t = dt.T
    cum_last = cum[q - 1:q, :]
    w_end = jnp.exp(cum_last - cum) * dt
    e_cum = jnp.exp(cum)
    e_last = jnp.exp(cum_last)

    for g in range(SSD_GROUPS):
        bgt = xc[:, SSD_INNER + g * SSD_S:SSD_INNER + (g + 1) * SSD_S].T.astype(BF16)
        cg = xc[:, SSD_INNER + (SSD_GROUPS + g) * SSD_S:SSD_INNER + (SSD_GROUPS + g + 1) * SSD_S].astype(BF16)
        gmat = _dot(cg, bgt)
        for hh in range(SSD_HEADS // SSD_GROUPS):
            h = g * (SSD_HEADS // SSD_GROUPS) + hh
            sl = slice(h * SSD_P, (h + 1) * SSD_P)
            diff = cum[:, h:h + 1] - cum_t[h:h + 1, :]
            decay = jnp.exp(jnp.where(causal, diff, NEG_INF))
            scores = gmat * decay * dt_t[h:h + 1, :]
            xh = xc[:, sl]
            hin = state_ref[h]
            y = _dot(scores.astype(BF16), xh.astype(BF16))
            y = y + _dot(cg, hin.astype(BF16)) * e_cum[:, h:h + 1]
            y = y + dsk_ref[:, sl] * xh
            st = _dot(bgt, (xh * w_end[:, h:h + 1]).astype(BF16))
            state_ref[h] = e_last[:, h:h + 1] * hin + st
            ybuf_ref[:, sl] = y

    zf = z_ref[...].astype(F32)
    yg = ybuf_ref[...] * (zf * _sigmoid(zf))
    y_ref[...] = _rms(yg, ng_ref[...]).astype(BF16)


def _ssd_prompt(xbc, z, dt, layer, n, length, cw, cb, dtb, alog, dsk, ng):
    nc = length // SSD_CHUNK
    nb = 1
    blk = lambda w: pl.BlockSpec((nb, SSD_CHUNK, w), lambda b, c: (b, c, 0))
    return pl.pallas_call(
        _ssd_prompt_body,
        grid=(n // nb, nc),
        in_specs=[blk(SSD_CONV_CH), blk(SSD_INNER), blk(DT_PAD)]
                 + [_layer_spec(w, layer) for w in (cw, cb, dtb, alog, dsk, ng)],
        out_specs=[blk(SSD_INNER),
                   pl.BlockSpec((nb, SSD_HEADS, SSD_P, SSD_S), lambda b, c: (b, 0, 0, 0))],
        out_shape=[jax.ShapeDtypeStruct((n, length, SSD_INNER), BF16),
                   jax.ShapeDtypeStruct((n, SSD_HEADS, SSD_P, SSD_S), F32)],
        scratch_shapes=[pltpu.VMEM((nb, SSD_CHUNK + 8, SSD_CONV_CH), F32),
                        pltpu.VMEM((nb, SSD_HEADS, SSD_P, SSD_S), F32),
                        pltpu.VMEM((nb, SSD_CHUNK, SSD_INNER), F32)],
        compiler_params=_params("arbitrary", "arbitrary"),
        name="ssd_prompt",
    )(xbc.reshape(n, length, SSD_CONV_CH), z.reshape(n, length, SSD_INNER),
      dt.reshape(n, length, DT_PAD), cw, cb, dtb, alog, dsk, ng)


def _ssd_conv_sample_body(xbc_ref, cs_ref, dt_ref, cw_ref, cb_ref, dtb_ref, xc_ref, dts_ref):
    conv = cb_ref[...] + cw_ref[3:4, :] * xbc_ref[...]
    for k in range(3):
        conv = conv + cw_ref[k:k + 1, :] * cs_ref[:, k * SSD_CONV_CH:(k + 1) * SSD_CONV_CH]
    xc_ref[...] = conv * _sigmoid(conv)
    dts_ref[...] = _softplus(dt_ref[...] + dtb_ref[...])


def _ssd_conv_sample(xbc, conv_state, dt, layer, cw, cb, dtb):
    n = xbc.shape[0]
    cs = conv_state.reshape(DEPTH, n, 3 * SSD_CONV_CH)
    return pl.pallas_call(
        _ssd_conv_sample_body,
        grid=(1,),
        in_specs=[_full_spec(xbc), _layer_spec(cs, layer), _full_spec(dt)]
                 + [_layer_spec(w, layer) for w in (cw, cb, dtb)],
        out_specs=[pl.BlockSpec((n, SSD_CONV_CH), lambda i: (0, 0)),
                   pl.BlockSpec((n, DT_PAD), lambda i: (0, 0))],
        out_shape=[jax.ShapeDtypeStruct((n, SSD_CONV_CH), F32), jax.ShapeDtypeStruct((n, DT_PAD), F32)],
        compiler_params=_params("arbitrary"),
        name="ssd_conv_sample",
    )(xbc, cs, dt, cw, cb, dtb)


def _ssd_state_sample_body(h0_ref, x_ref, b_ref, c_ref, dt_ref, alog_ref, dsk_ref, y_ref, hn_ref):
    h0 = h0_ref[...]
    x, bv, cv, dt = x_ref[...], b_ref[...], c_ref[...], dt_ref[...]
    dec = jnp.exp(dt * (-jnp.exp(alog_ref[...])))
    y_off = jnp.sum(h0 * cv[None], axis=1, keepdims=True) * dec
    cb = jnp.sum(cv * bv, axis=0, keepdims=True)
    y_ref[...] = (cb * dt) * x + y_off + dsk_ref[...] * x
    hn_ref[...] = dec * h0 + (dt * x) * bv[None]


def _ssd_state_sample(state_t, layer, x3, bc, dtt, alog_h, dsk_h):
    n = x3.shape[-1]
    rep = SSD_HEADS // SSD_GROUPS
    lanes = lambda: pl.BlockSpec((None, 1, n), lambda h: (layer * SSD_HEADS + h, 0, 0))
    st_blk = (None, None, SSD_P, SSD_S, n)
    return pl.pallas_call(
        _ssd_state_sample_body,
        grid=(SSD_HEADS,),
        in_specs=[pl.BlockSpec(st_blk, lambda h: (layer, h, 0, 0, 0)),
                  pl.BlockSpec((None, SSD_P, 1, n), lambda h: (h, 0, 0, 0)),
                  pl.BlockSpec((None, SSD_S, n), lambda h: (h // rep, 0, 0)),
                  pl.BlockSpec((None, SSD_S, n), lambda h: (SSD_GROUPS + h // rep, 0, 0)),
                  pl.BlockSpec((None, 1, n), lambda h: (h, 0, 0)), lanes(), lanes()],
        out_specs=[pl.BlockSpec((None, SSD_P, 1, n), lambda h: (h, 0, 0, 0)),
                   pl.BlockSpec((None, SSD_P, SSD_S, n), lambda h: (h, 0, 0, 0))],
        out_shape=[jax.ShapeDtypeStruct((SSD_HEADS, SSD_P, 1, n), F32),
                   jax.ShapeDtypeStruct((SSD_HEADS, SSD_P, SSD_S, n), F32)],
        compiler_params=_params("arbitrary"),
        name="ssd_state_sample",
    )(state_t, x3, bc, bc, dtt, alog_h, dsk_h)


def _ssd_gate_sample_body(y_ref, z_ref, ng_ref, o_ref):
    zf = z_ref[...].astype(F32)
    o_ref[...] = _rms(y_ref[...] * (zf * _sigmoid(zf)), ng_ref[...]).astype(BF16)


def _ssd_gate_sample(y, z, layer, ng):
    n = y.shape[0]
    return pl.pallas_call(
        _ssd_gate_sample_body,
        grid=(1,),
        in_specs=[_full_spec(y), _full_spec(z), _layer_spec(ng, layer)],
        out_specs=pl.BlockSpec((n, SSD_INNER), lambda i: (0, 0)),
        out_shape=jax.ShapeDtypeStruct((n, SSD_INNER), BF16),
        compiler_params=_params("arbitrary"),
        name="ssd_gate_sample",
    )(y, z, ng)


def _cmul(a, b):
    return a[0] * b[0] - a[1] * b[1], a[0] * b[1] + a[1] * b[0]


def _s5_discretise(lr, li, dl):
    delta = jnp.exp(dl)
    mag = jnp.exp(lr * delta)
    ab = (mag * jnp.cos(li * delta), mag * jnp.sin(li * delta))
    nr, ni = ab[0] - 1.0, ab[1]
    den = lr * lr + li * li
    return ab, ((nr * lr + ni * li) / den, (ni * lr - nr * li) / den)


def _s5_powers(ab, n):
    pw = [(jnp.ones_like(ab[0]), jnp.zeros_like(ab[0]))]
    for _ in range(n):
        pw.append(_cmul(pw[-1], ab))
    return pw


def _s5_tables_body(lr_r, li_r, dl_r, btr_r, bti_r, cr_r, ci_r, lr_l, li_l, dl_l,
                    lr_t, li_t, dl_t, cr_t, ci_t, tw_ref, v_ref, v0_ref, sc_ref):
    tc, nl = S5_TC, LANES
    ab, f = _s5_discretise(lr_r[...], li_r[...], dl_r[...])
    bb = _cmul(f, (btr_r[...], bti_r[...]))
    pw = _s5_powers(ab, tc - 1)
    cc = (cr_r[...], ci_r[...])
    bbr, bbi = bb[0].astype(BF16), bb[1].astype(BF16)
    rg = lax.broadcasted_iota(jnp.int32, (nl, nl), 0) // S5_CH
    cg = lax.broadcasted_iota(jnp.int32, (nl, nl), 1) // S5_CH
    kblk = []
    for tau in range(tc):
        cp = _cmul(cc, pw[tau])
        k = _dot_nt(bbr, cp[0].astype(BF16)) - _dot_nt(bbi, cp[1].astype(BF16))
        kblk.append(jnp.where(rg == cg, k, 0.0).astype(BF16))
    zero = jnp.zeros((nl, nl), BF16)
    for j in range(tc):
        for i in range(tc):
            tw_ref[j * nl:(j + 1) * nl, i * nl:(i + 1) * nl] = kblk[i - j] if i >= j else zero
    ab_l, _ = _s5_discretise(lr_l[...], li_l[...], dl_l[...])
    pw_l = _s5_powers(ab_l, tc)
    pair = lambda x: jnp.concatenate([x, x], axis=1)
    widen = lambda x: jnp.concatenate([pair(x)] * (S5_BLK // (2 * S5_STATE)), axis=1)
    bwr, bwi = widen(bb[0]), widen(bb[1])
    wrow = lax.broadcasted_iota(jnp.int32, (nl, S5_BLK), 0) // S5_CH
    wcol = lax.broadcasted_iota(jnp.int32, (nl, S5_BLK), 1) // S5_STATE
    wmask = wrow == wcol
    for j in range(tc):
        pr, pi = pw_l[tc - 1 - j]
        tw_ref[j * nl:(j + 1) * nl, tc * nl:tc * nl + S5_BLK] = jnp.where(wmask, bwr * pr - bwi * pi, 0.0).astype(BF16)
        tw_ref[j * nl:(j + 1) * nl, tc * nl + S5_BLK:] = jnp.where(wmask, bwr * pi + bwi * pr, 0.0).astype(BF16)
    a8 = pw_l[tc]
    a16 = _cmul(a8, a8)
    a32 = _cmul(a16, a16)
    rows = [ab_l, a16, a32, a8]
    for _ in range(tc - 1):
        rows.append(_cmul(rows[-1], a8))
    sc_ref[...] = jnp.zeros_like(sc_ref)
    for k, rw in enumerate(rows):
        sc_ref[0, k:k + 1, :] = rw[0]
        sc_ref[1, k:k + 1, :] = rw[1]
    ab_t, _ = _s5_discretise(lr_t[...], li_t[...], dl_t[...])
    pw_t = _s5_powers(ab_t, tc)
    cc_t = (cr_t[...], ci_t[...])
    vrow = lax.broadcasted_iota(jnp.int32, (S5_BLK, nl), 0) // S5_STATE
    vcol = lax.broadcasted_iota(jnp.int32, (S5_BLK, nl), 1) // S5_CH
    vmask = vrow == vcol
    tall = lambda x: jnp.concatenate([x] * (S5_BLK // S5_STATE), axis=0)
    for tau in range(tc + 1):
        cp = _cmul(cc_t, pw_t[tau])
        vr = jnp.where(vmask, tall(cp[0]), 0.0).astype(BF16)
        vi = jnp.where(vmask, tall(-cp[1]), 0.0).astype(BF16)
        if tau == 0:
            v0_ref[0:S5_BLK, :] = vr
            v0_ref[S5_BLK:, :] = vi
        else:
            v_ref[0:S5_BLK, (tau - 1) * nl:tau * nl] = vr
            v_ref[S5_BLK:, (tau - 1) * nl:tau * nl] = vi


def _s5_tables(a_re, a_im, log_dt, b_re, b_im, c_re, c_im, d_skip):
    g, s, c, gb = S5_GROUPS, S5_STATE, S5_CH, S5_GB
    ldt = jnp.broadcast_to(log_dt[:, :, None], (DEPTH, g, s))
    rows = lambda a: jnp.repeat(a, c, axis=1)
    flat = lambda a: a.reshape(DEPTH, gb, 1, S5_BLK)
    cols = lambda a: jnp.repeat(jnp.swapaxes(a, 1, 2), c, axis=2)
    by_rows = [rows(a_re), rows(a_im), rows(ldt),
               jnp.swapaxes(b_re, 2, 3).reshape(DEPTH, g * c, s), jnp.swapaxes(b_im, 2, 3).reshape(DEPTH, g * c, s),
               c_re.reshape(DEPTH, g * c, s), c_im.reshape(DEPTH, g * c, s)]
    by_lane = [flat(a_re), flat(a_im), flat(ldt)]
    by_col = [cols(a_re), cols(a_im), cols(ldt),
              jnp.transpose(c_re, (0, 3, 1, 2)).reshape(DEPTH, s, g * c),
              jnp.transpose(c_im, (0, 3, 1, 2)).reshape(DEPTH, s, g * c)]
    in_specs = ([pl.BlockSpec((None, LANES, s), lambda l, b: (l, b, 0))] * len(by_rows)
                + [pl.BlockSpec((None, None, 1, S5_BLK), lambda l, b: (l, b, 0, 0))] * len(by_lane)
                + [pl.BlockSpec((None, s, LANES), lambda l, b: (l, 0, b))] * len(by_col))
    blk = lambda *shape: pl.BlockSpec((None, None) + shape, lambda l, b: (l, b) + (0,) * len(shape))
    tcw = S5_TC * LANES
    tw, v_blk, v0_blk, scan = pl.pallas_call(
        _s5_tables_body,
        grid=(DEPTH, gb),
        in_specs=in_specs,
        out_specs=[blk(tcw, tcw + 2 * S5_BLK), blk(2 * S5_BLK, tcw), blk(2 * S5_BLK, LANES), blk(2, 16, S5_BLK)],
        out_shape=[jax.ShapeDtypeStruct((DEPTH, gb, tcw, tcw + 2 * S5_BLK), BF16),
                   jax.ShapeDtypeStruct((DEPTH, gb, 2 * S5_BLK, tcw), BF16),
                   jax.ShapeDtypeStruct((DEPTH, gb, 2 * S5_BLK, LANES), BF16),
                   jax.ShapeDtypeStruct((DEPTH, gb, 2, 16, S5_BLK), F32)],
        compiler_params=_params("arbitrary", "arbitrary"),
        name="s5_tables",
    )(*by_rows, *by_lane, *by_col)
    return tw, v_blk, v0_blk, scan, d_skip.reshape(DEPTH, gb, 1, LANES)


def _s5_prompt_body(u_ref, tw_ref, v_ref, sc_ref, d_ref, y_ref, hout_ref,
                    sre_ref, sim_ref, hin_ref, cr_ref, ci_ref):
    r = pl.program_id(2)
    tr = sre_ref.shape[0]

    @pl.when(r == 0)
    def _():
        cr_ref[...] = jnp.zeros_like(cr_ref)
        ci_ref[...] = jnp.zeros_like(ci_ref)

    u = jnp.concatenate([u_ref[j] for j in range(S5_TC)], axis=1)
    tw = _dot(u, tw_ref[...])
    yd = tw[:, 0:S5_TC * 128]
    sre_ref[...] = tw[:, S5_TC * 128:S5_TC * 128 + S5_BLK]
    sim_ref[...] = tw[:, S5_TC * 128 + S5_BLK:]

    a8 = (sc_ref[0, 3:4, :], sc_ref[1, 3:4, :])
    a16 = (sc_ref[0, 1:2, :], sc_ref[1, 1:2, :])
    a32 = (sc_ref[0, 2:3, :], sc_ref[1, 2:3, :])
    apw = (sc_ref[0, 3:11, :], sc_ref[1, 3:11, :])
    row = lax.broadcasted_iota(jnp.int32, (8, S5_BLK), 0)

    sub = lax.broadcasted_iota(jnp.int32, (tr, S5_BLK), 0) % 8
    xr, xi = sre_ref[...], sim_ref[...]
    for k, (ar, ai) in ((1, a8), (2, a16), (4, a32)):
        pr = jnp.where(sub >= k, pltpu.roll(xr, k, axis=0), 0.0)
        pi = jnp.where(sub >= k, pltpu.roll(xi, k, axis=0), 0.0)
        xr, xi = xr + ar * pr - ai * pi, xi + ar * pi + ai * pr
    sre_ref[...] = xr
    sim_ref[...] = xi

    def tile(t, carry):
        c_re, c_im = carry
        rows = pl.ds(pl.multiple_of(t * 8, 8), 8)
        xr, xi = sre_ref[rows, :], sim_ref[rows, :]
        xr = xr + apw[0] * c_re - apw[1] * c_im
        xi = xi + apw[0] * c_im + apw[1] * c_re
        hin_ref[rows, 0:S5_BLK] = jnp.where(row >= 1, pltpu.roll(xr, 1, axis=0), c_re)
        hin_ref[rows, S5_BLK:] = jnp.where(row >= 1, pltpu.roll(xi, 1, axis=0), c_im)
        return xr[7:8, :], xi[7:8, :]

    c_re, c_im = lax.fori_loop(0, tr // 8, tile, (cr_ref[...], ci_ref[...]))
    cr_ref[...] = c_re
    ci_ref[...] = c_im
    hout_ref[:, 0:S5_BLK] = c_re
    hout_ref[:, S5_BLK:] = c_im

    d_row = jnp.concatenate([d_ref[...]] * S5_TC, axis=1)
    y = yd + _dot(hin_ref[...].astype(BF16), v_ref[...]) + d_row * u.astype(F32)
    for j in range(S5_TC):
        y_ref[j] = y[:, j * 128:(j + 1) * 128].astype(BF16)


def _s5_prompt(u8, layer, n, length, tw, v_blk, scan, d_blk, tr=1024):
    rows = length // S5_TC
    tr = min(tr, rows)
    nr = rows // tr
    chunk_blk = pl.BlockSpec((S5_TC, tr, 128), lambda gb, b, r: (0, b * nr + r, gb))
    lay = lambda a: pl.BlockSpec((None, None) + tuple(a.shape[2:]),
                                 lambda gb, b, r: (layer, gb) + (0,) * (a.ndim - 2))
    y8, hout = pl.pallas_call(
        _s5_prompt_body,
        grid=(S5_GB, n, nr),
        in_specs=[chunk_blk, lay(tw), lay(v_blk), lay(scan), lay(d_blk)],
        out_specs=[chunk_blk,
                   pl.BlockSpec((None, None, 1, 2 * S5_BLK), lambda gb, b, r: (b, gb, 0, 0))],
        out_shape=[jax.ShapeDtypeStruct((S5_TC, n * rows, S5_W), BF16),
                   jax.ShapeDtypeStruct((n, S5_GB, 1, 2 * S5_BLK), F32)],
        scratch_shapes=[pltpu.VMEM((tr, S5_BLK), F32), pltpu.VMEM((tr, S5_BLK), F32),
                        pltpu.VMEM((tr, 2 * S5_BLK), F32),
                        pltpu.VMEM((1, S5_BLK), F32), pltpu.VMEM((1, S5_BLK), F32)],
        compiler_params=_params("arbitrary", "arbitrary", "arbitrary"),
        name="s5_prompt",
    )(u8, tw, v_blk, scan, d_blk)
    state = jnp.transpose(hout.reshape(n, S5_GB, 2, 8, S5_STATE), (0, 2, 1, 3, 4))
    return y8, state.reshape(n, 2, S5_GROUPS, S5_STATE)


def _s5_sample_body(u_ref, hr_ref, hi_ref, w_ref, v0_ref, sc_ref, d_ref, y_ref, nr_ref, ni_ref):
    u = u_ref[...]
    bu = _dot(u, w_ref[...])
    ar, ai = sc_ref[0, 0:1, :], sc_ref[1, 0:1, :]
    hr, hi = hr_ref[...], hi_ref[...]
    nr = ar * hr - ai * hi + bu[:, 0:S5_BLK]
    ni = ar * hi + ai * hr + bu[:, S5_BLK:]
    nr_ref[...] = nr
    ni_ref[...] = ni
    hcat = jnp.concatenate([nr, ni], axis=1).astype(BF16)
    y_ref[...] = (_dot(hcat, v0_ref[...]) + d_ref[...] * u.astype(F32)).astype(BF16)


def _s5_sample(u, state, layer, tw, v0_blk, scan, d_blk):
    n = u.shape[0]
    lay = lambda a: pl.BlockSpec((None, None) + tuple(a.shape[2:]),
                                 lambda gb: (layer, gb) + (0,) * (a.ndim - 2))
    st = lambda off: pl.BlockSpec((None, n, S5_BLK), lambda gb: (layer, 0, off + gb))
    half = S5_GROUPS * S5_STATE
    y, nr, ni = pl.pallas_call(
        _s5_sample_body,
        grid=(S5_GB,),
        in_specs=[pl.BlockSpec((n, 128), lambda gb: (0, gb)), st(0), st(S5_GB),
                  pl.BlockSpec((None, None, 128, 2 * S5_BLK), lambda gb: (layer, gb, S5_TC - 1, 1)),
                  lay(v0_blk), lay(scan), lay(d_blk)],
        out_specs=[pl.BlockSpec((n, 128), lambda gb: (0, gb)),
                   pl.BlockSpec((n, S5_BLK), lambda gb: (0, gb)),
                   pl.BlockSpec((n, S5_BLK), lambda gb: (0, gb))],
        out_shape=[jax.ShapeDtypeStruct((n, S5_W), BF16), jax.ShapeDtypeStruct((n, half), F32),
                   jax.ShapeDtypeStruct((n, half), F32)],
        compiler_params=_params("arbitrary"),
        name="s5_sample",
    )(u, state, state, tw, v0_blk, scan, d_blk)
    return y, jnp.stack([nr, ni], axis=1).reshape(n, 2, S5_GROUPS, S5_STATE)


def kernel(x_prompt, x_sample, cache_kv_w128, cache_kv_w512, cache_kv_w2048, state_ssd, state_conv, state_s5, p_prompt, p_sample, attn_rel_bias, ffn1_norm, ffn1_w_gate, ffn1_w_up, ffn1_w_down, mix_norm, w_in, ssd_conv_w, ssd_conv_b, ssd_dt_bias, ssd_a_log, ssd_d, ssd_norm, s5_a_re, s5_a_im, s5_b_re, s5_b_im, s5_c_re, s5_c_im, s5_d, s5_log_dt, w_s5_glu, w_branch_attn, w_branch_ssd, w_out, ffn2_norm, ffn2_w_gate, ffn2_w_up, ffn2_w_down, pe_norm, w_pe_gate, w_pe_proj, final_norm):
    n_p, len_p, _ = x_prompt.shape
    n_s = x_sample.shape[0]
    m_p = n_p * len_p
    assert x_sample.shape[1] == 1 and len_p % (QBLK * ATTN_PATTERNS[-1][1]) == 0
    caches = (cache_kv_w128, cache_kv_w512, cache_kv_w2048)
    dils = tuple(d for _, d in ATTN_PATTERNS)
    for cache, (w, d) in zip(caches, ATTN_PATTERNS):
        assert cache.shape[3] == w == QBLK * d
    caches_t = [jnp.transpose(c, (0, 1, 2, 4, 5, 3)) for c in caches]
    sample_bs = (16, 8, 2)

    bf = lambda a: a.astype(BF16)
    vec = lambda a: a.reshape(DEPTH, 1, a.shape[-1])
    w1g, w1u, w1d = bf(ffn1_w_gate), bf(ffn1_w_up), bf(ffn1_w_down)
    w2g, w2u, w2d = bf(ffn2_w_gate), bf(ffn2_w_up), bf(ffn2_w_down)
    mid_hi = QKV_W + MID_DT + SSD_HEADS
    w_qkv = jnp.swapaxes(w_in[..., :QKV_W].reshape(DEPTH, D_MODEL, 3, 3, GROUP_W), 2, 3)
    w_in_p = (bf(w_qkv.reshape(DEPTH, D_MODEL, QKV_W)),
              bf(jnp.pad(w_in[..., QKV_W:mid_hi], ((0, 0), (0, 0), (0, DT_PAD - SSD_HEADS)))),
              bf(w_in[..., mid_hi:]))
    wba, wbs, wglu, wout = bf(w_branch_attn), bf(w_branch_ssd), bf(w_s5_glu), bf(w_out)
    wpg, wpp = bf(w_pe_gate), bf(w_pe_proj)
    n1, nm, n2, npe = vec(ffn1_norm), vec(mix_norm), vec(ffn2_norm), vec(pe_norm)
    fin = final_norm.reshape(1, D_MODEL)
    cb = vec(ssd_conv_b)
    lane_pad = lambda a: jnp.pad(a, ((0, 0), (0, DT_PAD - a.shape[-1]))).reshape(DEPTH, 1, DT_PAD)
    dtb, alog = lane_pad(ssd_dt_bias), lane_pad(ssd_a_log)
    dsk = jnp.repeat(ssd_d, SSD_P, axis=-1).reshape(DEPTH, 1, SSD_INNER)
    ng = vec(ssd_norm)
    per_head = lambda a: jnp.broadcast_to(a.reshape(DEPTH * SSD_HEADS, 1, 1), (DEPTH * SSD_HEADS, 1, n_s))
    alog_h, dsk_h = per_head(ssd_a_log), per_head(ssd_d)

    tw, v_blk, v0_blk, scan, d_blk = _s5_tables(s5_a_re, s5_a_im, s5_log_dt, s5_b_re, s5_b_im,
                                                s5_c_re, s5_c_im, s5_d)

    bias_p = [_prompt_bias(attn_rel_bias, g, d) for g, (_, d) in enumerate(ATTN_PATTERNS)]
    bias_s = [_sample_bias(attn_rel_bias, g, d) for g, (_, d) in enumerate(ATTN_PATTERNS)]

    pp = p_prompt.reshape(DEPTH, m_p, PE_DIM)
    psm = p_sample.reshape(DEPTH, n_s, PE_DIM)
    st_ssd_t = jnp.transpose(state_ssd, (0, 2, 3, 4, 1))
    st_s5 = state_s5.reshape(DEPTH, n_s, 2 * S5_GROUPS * S5_STATE)

    xp = x_prompt.reshape(m_p, D_MODEL)
    xs = x_sample.reshape(n_s, D_MODEL)
    tm_p, tm_f, tm_m, tm_s = 256, 512, 256, n_s
    kv_p, kv_s = [[], [], []], [[], [], []]
    ssd_p, ssd_s, conv_p, conv_s, s5_p, s5_s = [], [], [], [], [], []

    def kv_prompt(qg, d):
        n, _, rows, _ = qg.shape
        t = qg[:, :, rows - QBLK:, GROUP_W:].reshape(n, d, QBLK, 2, HPG, HEAD_DIM)
        return jnp.transpose(t, (0, 3, 2, 1, 4, 5)).reshape(n, 2, QBLK * d, HPG, HEAD_DIM)

    for i in range(DEPTH):
        last = i == DEPTH - 1
        xp = _ffn(xp, i, tm_f, n1, w1g, w1u, w1d)
        *qgs, z, xbc, dt, u, gates = _inproj(xp, i, tm_p, nm, w_in_p, n_p, len_p, dils, True)
        parts = []
        for g, d in enumerate(dils):
            parts += list(_attn_prompt(qgs[g], bias_p[g], g))
            kv_p[g].append(kv_prompt(qgs[g], d))
        y_ssd, st = _ssd_prompt(xbc, z, dt, i, n_p, len_p, ssd_conv_w, cb, dtb, alog, dsk, ng)
        ssd_p.append(jnp.swapaxes(st, 2, 3))
        conv_p.append(xbc.reshape(n_p, len_p, SSD_CONV_CH)[:, len_p - 3:])
        y_s5, st5 = _s5_prompt(u, i, n_p, len_p, tw, v_blk, scan, d_blk)
        s5_p.append(st5)
        xp = _merge(xp, i, tm_m, parts, dils, len_p, y_ssd.reshape(m_p, SSD_INNER), y_s5, gates,
                    wba, wbs, wglu, wout, (n2, w2g, w2u, w2d), (pp, npe, wpg, wpp), fin if last else None)
        xs = _ffn(xs, i, tm_s, n1, w1g, w1u, w1d)
        *qgs, z, xbc, dt, u, gates = _inproj(xs, i, tm_s, nm, w_in_p, 1, n_s, (1, 1, 1), False)
        parts = []
        for g in range(len(dils)):
            qg = qgs[g].reshape(n_s, ATTN_W)
            o_s, l_s = _attn_sample(qg, caches_t[g], i, bias_s[g], g, sample_bs[g])
            parts += [o_s.reshape(1, 1, n_s, GROUP_W), l_s.reshape(1, 1, n_s, GROUP_W)]
            kv_s[g].append(qg[:, GROUP_W:].reshape(n_s, 2, 1, HPG, HEAD_DIM))
        xc, dts = _ssd_conv_sample(xbc, state_conv, dt, i, ssd_conv_w, cb, dtb)
        conv_s.append(jnp.concatenate([state_conv[i][:, 1:], xbc[:, None, :]], axis=1))
        xct = xc.T
        x3 = xct[:SSD_INNER].reshape(SSD_HEADS, SSD_P, 1, n_s)
        bc = xct[SSD_INNER:].reshape(2 * SSD_GROUPS, SSD_S, n_s)
        dtt = dts.T[:SSD_HEADS].reshape(SSD_HEADS, 1, n_s)
        yh, hn = _ssd_state_sample(st_ssd_t, i, x3, bc, dtt, alog_h, dsk_h)
        ssd_s.append(jnp.transpose(hn, (3, 0, 1, 2)))
        y_ssd = _ssd_gate_sample(yh.reshape(SSD_INNER, n_s).T, z, i, ng)
        y_s5, st5 = _s5_sample(u, st_s5, i, tw, v0_blk, scan, d_blk)
        s5_s.append(st5)
        xs = _merge(xs, i, tm_s, parts, (1, 1, 1), n_s, y_ssd, y_s5, gates, wba, wbs, wglu, wout,
                    (n2, w2g, w2u, w2d), (psm, npe, wpg, wpp), fin if last else None)

    return (xp.reshape(n_p, len_p, D_MODEL), xs.reshape(n_s, 1, D_MODEL),
            jnp.stack(kv_p[0], 0), jnp.stack(kv_s[0], 0),
            jnp.stack(kv_p[1], 0), jnp.stack(kv_s[1], 0),
            jnp.stack(kv_p[2], 0), jnp.stack(kv_s[2], 0),
            jnp.stack(ssd_p, 0), jnp.stack(ssd_s, 0),
            jnp.stack(conv_p, 0), jnp.stack(conv_s, 0),
            jnp.stack(s5_p, 0), jnp.stack(s5_s, 0))
```
